```python
import jax, jax.numpy as jnp
from jax import lax
import numpy as np

D_MODEL = 1024
BATCH = 4
SEQ = 8192
DEPTH = 1

PLE_DIM = 256
ATT_GROUPS = ((128, 1), (512, 4), (2048, 16))
N_ATT_GROUPS = 3
ATT_SLOTS = 4
ATT_HEADS = N_ATT_GROUPS * ATT_SLOTS
ATT_HEAD_DIM = 128
ATT_WIDTH = ATT_HEADS * ATT_HEAD_DIM
ATT_OUT_WIDTH = ATT_SLOTS * ATT_HEAD_DIM
ROPE_THETA = 500000.0
ROPE_DIM = ATT_HEAD_DIM // 4
MLSTM_HEADS = 4
MLSTM_HEAD_DIM = 128
MLSTM_WIDTH = MLSTM_HEADS * MLSTM_HEAD_DIM
MLSTM_CHUNK = 128
CONV_WIDTH = 4
D_FF = 2816
NORM_EPS = 1e-6
IN_WIDTHS = (ATT_WIDTH, ATT_WIDTH, ATT_WIDTH,
             MLSTM_WIDTH, MLSTM_WIDTH, MLSTM_WIDTH, MLSTM_WIDTH,
             MLSTM_HEADS, MLSTM_HEADS,
             D_MODEL, D_MODEL)
D_IN = sum(IN_WIDTHS)

kernel_name = 'hybrid_dilated_attn_mlstm_macaron_block'


def rms_norm(x, g):
    xf = x.astype(jnp.float32)
    y = xf * lax.rsqrt(jnp.mean(xf * xf, axis=-1, keepdims=True) + NORM_EPS)
    return (y * g.astype(jnp.float32)).astype(x.dtype)


def swiglu(x, w_in, w_out):
    a, b = jnp.split(x @ w_in, 2, axis=-1)
    return (jax.nn.silu(a) * b) @ w_out


def split_columns(z):
    offs, acc = [], 0
    for w in IN_WIDTHS[:-1]:
        acc += w
        offs.append(acc)
    return jnp.split(z, offs, axis=-1)


def rope_partial(x, pos):
    half = ROPE_DIM // 2
    inv_freq = 1.0 / (ROPE_THETA ** (jnp.arange(half, dtype=jnp.float32) / half))
    ang = pos[:, None] * inv_freq[None, :]
    cos = jnp.cos(ang)[:, None, None, :]
    sin = jnp.sin(ang)[:, None, None, :]
    x1, x2, rest = x[..., :half], x[..., half:ROPE_DIM], x[..., ROPE_DIM:]
    return jnp.concatenate([x1 * cos - x2 * sin, x2 * cos + x1 * sin, rest], axis=-1)


def dilated_window_attention(q, k, v, window, dilation):
    B, S, H, E = q.shape
    wd = window // dilation
    n = S // dilation
    nb = -(-n // wd)
    npad = nb * wd

    def to_sub(t):
        t = t.reshape(B, n, dilation, H, E).transpose(0, 2, 3, 1, 4)
        return jnp.pad(t, ((0, 0), (0, 0), (0, 0), (0, npad - n), (0, 0)))

    def kv_blocks(t):
        t = jnp.pad(to_sub(t), ((0, 0), (0, 0), (0, 0), (wd, 0), (0, 0)))
        t = t.reshape(B, dilation, H, nb + 1, wd, E)
        return jnp.concatenate([t[:, :, :, :-1], t[:, :, :, 1:]], axis=4)

    qb = to_sub(q).reshape(B, dilation, H, nb, wd, E)
    kb, vb = kv_blocks(k), kv_blocks(v)
    s = jnp.einsum('brhnqe,brhnke->brhnqk', qb, kb) * (E ** -0.5)
    qi = jnp.arange(wd)[:, None]
    kj = jnp.arange(2 * wd)[None, :]
    blk = jnp.arange(nb)[:, None, None]
    mask = (kj >= qi) & (kj <= qi + wd) & ((blk > 0) | (kj >= wd))
    s = jnp.where(mask, s, -jnp.inf)
    m = jnp.max(s, axis=-1, keepdims=True)
    e = jnp.exp(s - m)
    den = jnp.sum(e, axis=-1)
    o = jnp.einsum('brhnqk,brhnke->brhnqe', e, vb) / den[..., None]
    lse = m[..., 0] + jnp.log(den)

    def from_sub(t):
        t = t.reshape(B, dilation, H, npad, *t.shape[5:])[:, :, :, :n]
        t = jnp.moveaxis(t, 3, 1)
        return t.reshape(B, S, H, *t.shape[4:])

    return from_sub(o), from_sub(lse)


def dilated_attention_mixer(q, k, v, q_gain, k_gain):
    B, S, _ = q.shape
    shape = (B, S, N_ATT_GROUPS, ATT_SLOTS, ATT_HEAD_DIM)
    pos = jnp.arange(S, dtype=jnp.float32)
    q = rope_partial(rms_norm(q.reshape(shape).astype(jnp.float32), q_gain), pos)
    k = rope_partial(rms_norm(k.reshape(shape).astype(jnp.float32), k_gain), pos)
    v = v.reshape(shape).astype(jnp.float32)
    outs, lses = [], []
    for g, (window, dilation) in enumerate(ATT_GROUPS):
        o, l = dilated_window_attention(q[:, :, g], k[:, :, g], v[:, :, g], window, dilation)
        outs.append(o)
        lses.append(l)
    alpha = jax.nn.softmax(jnp.stack(lses, axis=0), axis=0)
    out = jnp.einsum('gbsh,gbshe->bshe', alpha, jnp.stack(outs, axis=0))
    return out.reshape(B, S, ATT_OUT_WIDTH)


def causal_dwconv(x, w, b):
    K = w.shape[0]
    S = x.shape[1]
    xp = jnp.pad(x, ((0, 0), (K - 1, 0), (0, 0)))
    out = b
    for j in range(K):
        out = out + xp[:, j:j + S] * w[j]
    return out


def mlstm_chunkwise(q, k, v, logi, logf):
    B, S, H, E = q.shape
    L = MLSTM_CHUNK
    nc = S // L

    def chunks(t):
        return t.reshape(B, nc, L, H, E).transpose(1, 0, 3, 2, 4)

    def gchunks(t):
        return t.reshape(B, nc, L, H).transpose(1, 0, 3, 2)

    causal = jnp.tril(jnp.ones((L, L), dtype=bool))

    def step(carry, inp):
        C, n, m = carry
        qc, kc, vc, li, lf = inp
        b = jnp.cumsum(lf, axis=-1)
        Dm = b[..., :, None] - b[..., None, :] + li[..., None, :]
        Dm = jnp.where(causal, Dm, -jnp.inf)
        inter = b + m[..., None]
        m_t = jnp.maximum(inter, jnp.max(Dm, axis=-1))
        W = jnp.exp(Dm - m_t[..., None])
        a = jnp.exp(inter - m_t)
        qk = jnp.einsum('bhte,bhse->bhts', qc, kc) * W
        num = a[..., None] * jnp.einsum('bhte,bhef->bhtf', qc, C) + jnp.einsum('bhts,bhsf->bhtf', qk, vc)
        den = a * jnp.einsum('bhte,bhe->bht', qc, n) + jnp.sum(qk, axis=-1)
        h = num / jnp.maximum(jnp.abs(den), jnp.exp(-m_t))[..., None]
        bL = b[..., -1]
        g = bL[..., None] - b + li
        m_new = jnp.maximum(bL + m, jnp.max(g, axis=-1))
        wk = jnp.exp(g - m_new[..., None])
        decay = jnp.exp(bL + m - m_new)
        C_new = decay[..., None, None] * C + jnp.einsum('bhs,bhse,bhsf->bhef', wk, kc, vc)
        n_new = decay[..., None] * n + jnp.einsum('bhs,bhse->bhe', wk, kc)
        return (C_new, n_new, m_new), h

    init = (jnp.zeros((B, H, E, E), jnp.float32),
            jnp.zeros((B, H, E), jnp.float32),
            jnp.zeros((B, H), jnp.float32))
    _, hs = lax.scan(step, init, (chunks(q), chunks(k), chunks(v), gchunks(logi), gchunks(logf)))
    return hs.transpose(1, 0, 3, 2, 4).reshape(B, S, H, E)


def mlstm_mixer(q, k, v, o_pre, i_pre, f_pre, conv_w, conv_b, i_bias, f_bias):
    B, S, _ = q.shape
    qk = jax.nn.silu(causal_dwconv(jnp.concatenate([q, k], axis=-1), conv_w, conv_b))
    q, k = jnp.split(qk, 2, axis=-1)
    shape = (B, S, MLSTM_HEADS, MLSTM_HEAD_DIM)
    qh = q.reshape(shape).astype(jnp.float32)
    kh = k.reshape(shape).astype(jnp.float32) * (MLSTM_HEAD_DIM ** -0.5)
    vh = v.reshape(shape).astype(jnp.float32)
    logi = (i_pre + i_bias).astype(jnp.float32)
    logf = jax.nn.log_sigmoid((f_pre + f_bias).astype(jnp.float32))
    h = mlstm_chunkwise(qh, kh, vh, logi, logf).reshape(B, S, MLSTM_WIDTH)
    return (jax.nn.sigmoid(o_pre.astype(jnp.float32)) * h).astype(q.dtype)


def hybrid_layer(x, p_i, ffn1_norm, ffn1_w_in, ffn1_w_out, mix_norm, w_in, q_gain, k_gain,
                 conv_w, conv_b, i_bias, f_bias, w_up_att, w_up_mlstm, w_out,
                 ffn2_norm, ffn2_w_in, ffn2_w_out, ple_norm, w_ple_gate, w_ple_proj):
    h = x + 0.5 * swiglu(rms_norm(x, ffn1_norm), ffn1_w_in, ffn1_w_out)
    z = rms_norm(h, mix_norm) @ w_in
    aq, ak, av, mq, mk, mv, mo, mi, mf, ga, gb = split_columns(z)
    y_att = dilated_attention_mixer(aq, ak, av, q_gain, k_gain).astype(x.dtype) @ w_up_att
    y_ml = mlstm_mixer(mq, mk, mv, mo, mi, mf, conv_w, conv_b, i_bias, f_bias) @ w_up_mlstm
    merged = jax.nn.sigmoid(ga) * y_att + jax.nn.sigmoid(gb) * y_ml
    h = h + merged @ w_out
    h = h + 0.5 * swiglu(rms_norm(h, ffn2_norm), ffn2_w_in, ffn2_w_out)
    h = h + (p_i @ w_ple_proj) * jax.nn.sigmoid(rms_norm(h, ple_norm) @ w_ple_gate)
    return h


def setup_inputs(seed: int = 0) -> dict:
    key = jax.random.key(seed)
    ks = jax.random.split(key, 24)
    f32 = jnp.float32

    def nrm(k, shape, scale):
        return jax.random.normal(k, shape, f32) * scale

    def gain(k, shape):
        return 1.0 + 0.02 * jax.random.normal(k, shape, f32)

    return {
        'x': nrm(ks[0], (BATCH, SEQ, D_MODEL), 1.0),
        'p': nrm(ks[1], (DEPTH, BATCH, SEQ, PLE_DIM), 1.0),
        'ffn1_norm': gain(ks[2], (DEPTH, D_MODEL)),
        'ffn1_w_in': nrm(ks[3], (DEPTH, D_MODEL, 2 * D_FF), D_MODEL ** -0.5),
        'ffn1_w_out': nrm(ks[4], (DEPTH, D_FF, D_MODEL), D_FF ** -0.5),
        'mix_norm': gain(ks[5], (DEPTH, D_MODEL)),
        'w_in': nrm(ks[6], (DEPTH, D_MODEL, D_IN), D_MODEL ** -0.5),
        'q_gain': gain(ks[7], (DEPTH, ATT_HEAD_DIM)),
        'k_gain': gain(ks[8], (DEPTH, ATT_HEAD_DIM)),
        'conv_w': nrm(ks[9], (DEPTH, CONV_WIDTH, 2 * MLSTM_WIDTH), CONV_WIDTH ** -0.5),
        'conv_b': nrm(ks[10], (DEPTH, 2 * MLSTM_WIDTH), 0.02),
        'i_bias': nrm(ks[11], (DEPTH, MLSTM_HEADS), 0.1),
        'f_bias': jnp.linspace(3.0, 6.0, MLSTM_HEADS, dtype=f32)[None, :] + nrm(ks[12], (DEPTH, MLSTM_HEADS), 0.1),
        'w_up_att': nrm(ks[13], (DEPTH, ATT_OUT_WIDTH, D_MODEL), ATT_OUT_WIDTH ** -0.5),
        'w_up_mlstm': nrm(ks[14], (DEPTH, MLSTM_WIDTH, D_MODEL), MLSTM_WIDTH ** -0.5),
        'w_out': nrm(ks[15], (DEPTH, D_MODEL, D_MODEL), D_MODEL ** -0.5),
        'ffn2_norm': gain(ks[16], (DEPTH, D_MODEL)),
        'ffn2_w_in': nrm(ks[17], (DEPTH, D_MODEL, 2 * D_FF), D_MODEL ** -0.5),
        'ffn2_w_out': nrm(ks[18], (DEPTH, D_FF, D_MODEL), D_FF ** -0.5),
        'ple_norm': gain(ks[19], (DEPTH, D_MODEL)),
        'w_ple_gate': nrm(ks[20], (DEPTH, D_MODEL, D_MODEL), D_MODEL ** -0.5),
        'w_ple_proj': nrm(ks[21], (DEPTH, PLE_DIM, D_MODEL), PLE_DIM ** -0.5),
    }


def reference(x, p, ffn1_norm, ffn1_w_in, ffn1_w_out, mix_norm, w_in, q_gain, k_gain,
              conv_w, conv_b, i_bias, f_bias, w_up_att, w_up_mlstm, w_out,
              ffn2_norm, ffn2_w_in, ffn2_w_out, ple_norm, w_ple_gate, w_ple_proj):
    h = x
    for i in range(DEPTH):
        h = hybrid_layer(h, p[i], ffn1_norm[i], ffn1_w_in[i], ffn1_w_out[i], mix_norm[i], w_in[i],
                         q_gain[i], k_gain[i], conv_w[i], conv_b[i], i_bias[i], f_bias[i],
                         w_up_att[i], w_up_mlstm[i], w_out[i], ffn2_norm[i], ffn2_w_in[i],
                         ffn2_w_out[i], ple_norm[i], w_ple_gate[i], w_ple_proj[i])
    return h
```

```python
import functools
import math

import jax
import jax.numpy as jnp
from jax import lax
from jax.experimental import pallas as pl
from jax.experimental.pallas import tpu as pltpu

D_MODEL = 1024
PLE_DIM = 256
ATT_GROUPS = ((128, 1), (512, 4), (2048, 16))
N_ATT_GROUPS = 3
ATT_SLOTS = 4
ATT_HEADS = N_ATT_GROUPS * ATT_SLOTS
HEAD_DIM = 128
ATT_WIDTH = ATT_HEADS * HEAD_DIM
ATT_OUT_WIDTH = ATT_SLOTS * HEAD_DIM
ROPE_THETA = 500000.0
ROPE_DIM = HEAD_DIM // 4
ROPE_HALF = ROPE_DIM // 2
MLSTM_HEADS = 4
MLSTM_WIDTH = MLSTM_HEADS * HEAD_DIM
MLSTM_CHUNK = 128
CONV_WIDTH = 4
D_FF = 2816
NORM_EPS = 1e-6

FF_CHUNK = 256
N_FF_CHUNKS = D_FF // FF_CHUNK
ATT_BLOCK = 128
GATE_LANES = 128
CONV_PAD = 8
V7X_VMEM_LIMIT = 56 * 1024 * 1024

BF16 = jnp.bfloat16
F32 = jnp.float32


def _rms_norm(x, g):
    ms = jnp.mean(x * x, axis=-1, keepdims=True)
    return x * lax.rsqrt(ms + NORM_EPS) * g


def _dot(a, b):
    return jnp.dot(a, b, preferred_element_type=F32)


def _dot_nt(a, b):
    return lax.dot_general(a, b, (((1,), (1,)), ((), ())), preferred_element_type=F32)


def _dot_tn(a, b):
    return lax.dot_general(a, b, (((0,), (0,)), ((), ())), preferred_element_type=F32)


def _resident(shape):
    nd = len(shape)
    return pl.BlockSpec(shape, lambda *_: (0,) * nd, pipeline_mode=pl.Buffered(1))


def _params(semantics):
    return pltpu.CompilerParams(dimension_semantics=semantics, vmem_limit_bytes=V7X_VMEM_LIMIT)


def _swiglu_update(x_ref, g_ref, w_in_ref, w_out_ref, acc_ref):
    u = _rms_norm(x_ref[...], g_ref[...]).astype(BF16)
    acc_ref[...] = jnp.zeros_like(acc_ref)

    def body(c, carry):
        ab = _dot(u, w_in_ref[c])
        a = ab[:, :FF_CHUNK]
        b = ab[:, FF_CHUNK:]
        hid = (a * jax.nn.sigmoid(a) * b).astype(BF16)
        acc_ref[...] += _dot(hid, w_out_ref[c])
        return carry

    lax.fori_loop(0, N_FF_CHUNKS, body, 0)
    return x_ref[...] + 0.5 * acc_ref[...]


def _ffn_kernel(x_ref, g_ref, w_in_ref, w_out_ref, o_ref, acc_ref):
    o_ref[...] = _swiglu_update(x_ref, g_ref, w_in_ref, w_out_ref, acc_ref)


def _ffn_ple_kernel(x_ref, p_ref, g_ref, w_in_ref, w_out_ref, pg_ref, w_pg_ref, w_pe_ref, o_ref, acc_ref):
    h = _swiglu_update(x_ref, g_ref, w_in_ref, w_out_ref, acc_ref)
    u = _rms_norm(h, pg_ref[...]).astype(BF16)
    gate = jax.nn.sigmoid(_dot(u, w_pg_ref[...]))
    pe = _dot(p_ref[...].astype(BF16), w_pe_ref[...])
    o_ref[...] = h + pe * gate


def _prep_ffn_weights(w_in, w_out):
    a = w_in[:, :D_FF].reshape(D_MODEL, N_FF_CHUNKS, FF_CHUNK)
    b = w_in[:, D_FF:].reshape(D_MODEL, N_FF_CHUNKS, FF_CHUNK)
    w_in_c = jnp.concatenate([a, b], axis=-1).transpose(1, 0, 2).astype(BF16)
    w_out_c = w_out.reshape(N_FF_CHUNKS, FF_CHUNK, D_MODEL).astype(BF16)
    return w_in_c, w_out_c


def _ffn_call(x2d, norm_g, w_in_c, w_out_c, tm, ple=None):
    n_rows = x2d.shape[0]
    row = lambda w: pl.BlockSpec((tm, w), lambda i: (i, 0))
    g2 = norm_g.reshape(1, D_MODEL)
    if ple is None:
        kern = _ffn_kernel
        args = (x2d, g2, w_in_c, w_out_c)
        specs = [row(D_MODEL), _resident(g2.shape), _resident(w_in_c.shape), _resident(w_out_c.shape)]
        name = "ffn"
    else:
        p2d, ple_g, w_pg, w_pe = ple
        pg2 = ple_g.reshape(1, D_MODEL)
        kern = _ffn_ple_kernel
        args = (x2d, p2d, g2, w_in_c, w_out_c, pg2, w_pg, w_pe)
        specs = [row(D_MODEL), row(PLE_DIM), _resident(g2.shape), _resident(w_in_c.shape),
                 _resident(w_out_c.shape), _resident(pg2.shape), _resident(w_pg.shape), _resident(w_pe.shape)]
        name = "ffn_ple"
    return pl.pallas_call(
        kern,
        grid=(n_rows // tm,),
        in_specs=specs,
        out_specs=row(D_MODEL),
        out_shape=jax.ShapeDtypeStruct((n_rows, D_MODEL), F32),
        scratch_shapes=[pltpu.VMEM((tm, D_MODEL), F32)],
        compiler_params=_params(("parallel",)),
        name=name,
    )(*args)


def _log_sigmoid(x):
    return jnp.minimum(x, 0.0) - jnp.log1p(jnp.exp(-jnp.abs(x)))


def _in_proj_kernel(h_ref, g_ref, wq_ref, wk_ref, wv_ref, wm_ref, wg_ref, wgt_ref, qg_ref, kg_ref,
                    cos_ref, sin_ref, gb_ref, gbt_ref,
                    q_ref, k_ref, v_ref, mq_ref, mk_ref, mv_ref, gr_ref, gt_ref):
    u = _rms_norm(h_ref[...], g_ref[...]).astype(BF16)
    cos = cos_ref[...]
    sin = sin_ref[...]

    def qk_heads(w_ref, gain_ref, out_ref, scale):
        z = _dot(u, w_ref[...])
        gain = gain_ref[...]
        for hd in range(ATT_HEADS):
            zh = z[:, hd * HEAD_DIM:(hd + 1) * HEAD_DIM]
            y = _rms_norm(zh, gain)
            y = y * cos + pltpu.roll(y, HEAD_DIM // 2, 1) * sin
            if scale != 1.0:
                y = y * scale
            out_ref[:, hd * HEAD_DIM:(hd + 1) * HEAD_DIM] = y.astype(BF16)

    qk_heads(wq_ref, qg_ref, q_ref, HEAD_DIM ** -0.5)
    qk_heads(wk_ref, kg_ref, k_ref, 1.0)
    v_ref[...] = _dot(u, wv_ref[...]).astype(BF16)
    zm = _dot(u, wm_ref[...])
    mq_ref[...] = zm[:, :MLSTM_WIDTH]
    mk_ref[...] = zm[:, MLSTM_WIDTH:2 * MLSTM_WIDTH]
    mv_ref[...] = zm[:, 2 * MLSTM_WIDTH:].astype(BF16)

    zr = _dot(u, wg_ref[...]) + gb_ref[...]
    lane = lax.broadcasted_iota(jnp.int32, zr.shape, 1)
    gr_ref[...] = jnp.where(lane < MLSTM_HEADS, zr, _log_sigmoid(zr))
    zt = _dot_nt(wgt_ref[...], u) + gbt_ref[...]
    sub = lax.broadcasted_iota(jnp.int32, zt.shape, 0)
    gt_ref[...] = jnp.where(sub < MLSTM_HEADS, zt, _log_sigmoid(zt))


def _rope_lane_perm():
    return (list(range(0, ROPE_HALF)) + list(range(ROPE_DIM, ROPE_DIM + 48))
            + list(range(ROPE_HALF, ROPE_DIM)) + list(range(ROPE_DIM + 48, HEAD_DIM)))


def _rope_tables(seq):
    pos = jnp.arange(seq, dtype=F32)
    inv_freq = 1.0 / (ROPE_THETA ** (jnp.arange(ROPE_HALF, dtype=F32) / ROPE_HALF))
    ang = pos[:, None] * inv_freq[None, :]
    cos, sin = jnp.cos(ang), jnp.sin(ang)
    ones = jnp.ones((seq, HEAD_DIM // 2 - ROPE_HALF), F32)
    zeros = jnp.zeros_like(ones)
    cos_t = jnp.concatenate([cos, ones, cos, ones], axis=1)
    sin_t = jnp.concatenate([-sin, zeros, sin, zeros], axis=1)
    return cos_t, sin_t


def _in_proj_call(h2d, mix_norm, w_in, q_gain, k_gain, i_bias, f_bias, batch, seq, tm):
    n_rows = h2d.shape[0]
    perm = jnp.asarray(_rope_lane_perm(), dtype=jnp.int32)
    head_perm = (jnp.arange(ATT_HEADS, dtype=jnp.int32)[:, None] * HEAD_DIM + perm[None, :]).reshape(-1)
    o = 0
    wq = w_in[:, o:o + ATT_WIDTH][:, head_perm].astype(BF16); o += ATT_WIDTH
    wk = w_in[:, o:o + ATT_WIDTH][:, head_perm].astype(BF16); o += ATT_WIDTH
    wv = w_in[:, o:o + ATT_WIDTH].astype(BF16); o += ATT_WIDTH
    wm = w_in[:, o:o + 3 * MLSTM_WIDTH].astype(BF16); o += 4 * MLSTM_WIDTH
    w_gate = w_in[:, o:o + 2 * MLSTM_HEADS]
    wg = jnp.pad(w_gate, ((0, 0), (0, GATE_LANES - 2 * MLSTM_HEADS))).astype(BF16)
    wgt = w_gate.T.astype(BF16)
    gate_bias = jnp.concatenate([i_bias, f_bias]).astype(F32)
    gb = jnp.pad(gate_bias, (0, GATE_LANES - 2 * MLSTM_HEADS)).reshape(1, GATE_LANES)
    gbt = gate_bias.reshape(2 * MLSTM_HEADS, 1)
    qg = q_gain[perm].reshape(1, HEAD_DIM)
    kg = k_gain[perm].reshape(1, HEAD_DIM)
    cos_t, sin_t = _rope_tables(seq)
    g2 = mix_norm.reshape(1, D_MODEL)
    tiles_per_seq = seq // tm

    row = lambda w: pl.BlockSpec((tm, w), lambda i: (i, 0))
    pos = pl.BlockSpec((tm, HEAD_DIM), lambda i: (i % tiles_per_seq, 0))
    in_specs = [row(D_MODEL), _resident(g2.shape), _resident(wq.shape), _resident(wk.shape), _resident(wv.shape),
                _resident(wm.shape), _resident(wg.shape), _resident(wgt.shape), _resident(qg.shape),
                _resident(kg.shape), pos, pos, _resident(gb.shape), _resident(gbt.shape)]
    out_specs = [row(ATT_WIDTH), row(ATT_WIDTH), row(ATT_WIDTH), row(MLSTM_WIDTH), row(MLSTM_WIDTH),
                 row(MLSTM_WIDTH), row(GATE_LANES), pl.BlockSpec((2 * MLSTM_HEADS, tm), lambda i: (0, i))]
    out_shape = [jax.ShapeDtypeStruct((n_rows, ATT_WIDTH), BF16)] * 3 + [
        jax.ShapeDtypeStruct((n_rows, MLSTM_WIDTH), F32), jax.ShapeDtypeStruct((n_rows, MLSTM_WIDTH), F32),
        jax.ShapeDtypeStruct((n_rows, MLSTM_WIDTH), BF16), jax.ShapeDtypeStruct((n_rows, GATE_LANES), F32),
        jax.ShapeDtypeStruct((2 * MLSTM_HEADS, n_rows), F32)]
    return pl.pallas_call(
        _in_proj_kernel,
        grid=(n_rows // tm,),
        in_specs=in_specs,
        out_specs=out_specs,
        out_shape=out_shape,
        compiler_params=_params(("parallel",)),
        name="in_proj",
    )(h2d, g2, wq, wk, wv, wm, wg, wgt, qg, kg, cos_t, sin_t, gb, gbt)


def _attention_kernel(q_ref, kp_ref, kc_ref, vp_ref, vc_ref, o_ref, lse_ref):
    blk = pl.program_id(2)
    qi = lax.broadcasted_iota(jnp.int32, (ATT_BLOCK, 2 * ATT_BLOCK), 0)
    kj = lax.broadcasted_iota(jnp.int32, (ATT_BLOCK, 2 * ATT_BLOCK), 1)
    first_key = jnp.where(blk > 0, 0, ATT_BLOCK)
    mask = (kj >= qi) & (kj <= qi + ATT_BLOCK) & (kj >= first_key)
    lane = lax.broadcasted_iota(jnp.int32, (ATT_BLOCK, GATE_LANES), 1)
    lse_slab = jnp.zeros((ATT_BLOCK, GATE_LANES), F32)
    for j in range(ATT_SLOTS):
        sl = slice(j * HEAD_DIM, (j + 1) * HEAD_DIM)
        q = q_ref[0, :, sl]
        k = jnp.concatenate([kp_ref[0, :, sl], kc_ref[0, :, sl]], axis=0)
        v = jnp.concatenate([vp_ref[0, :, sl], vc_ref[0, :, sl]], axis=0)
        s = jnp.where(mask, _dot_nt(q, k), -jnp.inf)
        m = jnp.max(s, axis=-1, keepdims=True)
        e = jnp.exp(s - m)
        den = jnp.sum(e, axis=-1, keepdims=True)
        o_ref[0, :, sl] = _dot(e.astype(BF16), v) / den
        lse_slab = jnp.where(lane == j, m + jnp.log(den), lse_slab)
    lse_ref[0] = lse_slab


def _attention_call(q, k, v, group, dilation, batch, seq):
    n = seq // dilation
    nb = n // ATT_BLOCK
    q3 = q.reshape(batch, n, dilation * ATT_WIDTH)
    k3 = k.reshape(batch, n, dilation * ATT_WIDTH)
    v3 = v.reshape(batch, n, dilation * ATT_WIDTH)
    groups_per_row = ATT_WIDTH // ATT_OUT_WIDTH
    cur = pl.BlockSpec((1, ATT_BLOCK, ATT_OUT_WIDTH), lambda b, r, i: (b, i, r * groups_per_row + group))
    prev = pl.BlockSpec((1, ATT_BLOCK, ATT_OUT_WIDTH),
                        lambda b, r, i: (b, jnp.maximum(i - 1, 0), r * groups_per_row + group))
    o, lse = pl.pallas_call(
        _attention_kernel,
        grid=(batch, dilation, nb),
        in_specs=[cur, prev, cur, prev, cur],
        out_specs=[pl.BlockSpec((1, ATT_BLOCK, ATT_OUT_WIDTH), lambda b, r, i: (b, i, r)),
                   pl.BlockSpec((1, ATT_BLOCK, GATE_LANES), lambda b, r, i: (b, i, r))],
        out_shape=[jax.ShapeDtypeStruct((batch, n, dilation * ATT_OUT_WIDTH), F32),
                   jax.ShapeDtypeStruct((batch, n, dilation * GATE_LANES), F32)],
        compiler_params=_params(("parallel", "parallel", "arbitrary")),
        name=f"attention_g{group}",
    )(q3, k3, k3, v3, v3)
    return o.reshape(batch * seq, ATT_OUT_WIDTH), lse.reshape(batch * seq, GATE_LANES)


def _mlstm_kernel(mq_ref, mk_ref, mv_ref, gr_ref, gt_ref, cw_ref, cb_ref, o_ref,
                  xpad_ref, state_ref, m_ref):
    L = MLSTM_CHUNK
    H = MLSTM_HEADS
    W = MLSTM_WIDTH

    @pl.when(pl.program_id(1) == 0)
    def _():
        xpad_ref[0:CONV_PAD, :] = jnp.zeros((CONV_PAD, 2 * W), F32)
        state_ref[...] = jnp.zeros_like(state_ref)
        m_ref[...] = jnp.zeros_like(m_ref)

    xpad_ref[CONV_PAD:CONV_PAD + L, 0:W] = mq_ref[0]
    xpad_ref[CONV_PAD:CONV_PAD + L, W:2 * W] = mk_ref[0]
    acc = cb_ref[...]
    for j in range(CONV_WIDTH):
        off = CONV_PAD - (CONV_WIDTH - 1) + j
        acc = acc + xpad_ref[off:off + L, :] * cw_ref[j:j + 1, :]
    qk = acc * jax.nn.sigmoid(acc)
    xpad_ref[0:CONV_PAD, :] = xpad_ref[L:L + CONV_PAD, :]

    gr = gr_ref[...]
    gt = gt_ref[...]
    ti = lax.broadcasted_iota(jnp.int32, (L, L), 0)
    si = lax.broadcasted_iota(jnp.int32, (L, L), 1)
    causal = ti >= si
    tri = causal.astype(F32)
    b_rows = lax.dot_general(gt, tri, (((1,), (1,)), ((), ())), precision=lax.Precision.HIGHEST,
                             preferred_element_type=F32)
    b_cols = jnp.dot(tri, gr, precision=lax.Precision.HIGHEST, preferred_element_type=F32)
    ones = jnp.ones((L, HEAD_DIM), BF16)

    for hd in range(H):
        sl = slice(hd * HEAD_DIM, (hd + 1) * HEAD_DIM)
        q = qk[:, sl].astype(BF16)
        kf = qk[:, W + hd * HEAD_DIM:W + (hd + 1) * HEAD_DIM] * (HEAD_DIM ** -0.5)
        k = kf.astype(BF16)
        vext = jnp.concatenate([mv_ref[0, :, sl], ones], axis=1)
        li_row = gt[hd:hd + 1, :]
        li_col = gr[:, hd:hd + 1]
        b_row = b_rows[H + hd:H + hd + 1, :]
        b_col = b_cols[:, H + hd:H + hd + 1]
        m_prev = m_ref[hd, 0:1, 0:1]
        state = state_ref[hd]

        dm = jnp.where(causal, b_col - b_row + li_row, -jnp.inf)
        inter = b_col + m_prev
        m_t = jnp.maximum(inter, jnp.max(dm, axis=-1, keepdims=True))
        wgt = jnp.exp(dm - m_t)
        a = jnp.exp(inter - m_t)
        s = _dot_nt(q, k) * wgt
        ext = a * _dot(q, state.astype(BF16)) + _dot(s.astype(BF16), vext)
        num = ext[:, :HEAD_DIM]
        den = ext[:, HEAD_DIM:]
        o_ref[0, :, sl] = num / jnp.maximum(jnp.abs(den), jnp.exp(-m_t))

        b_last = b_col[L - 1:L, :]
        g = b_last - b_col + li_col
        m_new = jnp.maximum(b_last + m_prev, jnp.max(g, axis=0, keepdims=True))
        wk = jnp.exp(g - m_new)
        decay = jnp.exp(b_last + m_prev - m_new)
        state_ref[hd] = decay * state + _dot_tn((kf * wk).astype(BF16), vext)
        m_ref[hd] = jnp.broadcast_to(m_new, m_ref.shape[1:])


def _mlstm_call(mq, mk, mv, gr, gt, conv_w, conv_b, batch, seq):
    nc = seq // MLSTM_CHUNK
    W = MLSTM_WIDTH
    blk = lambda: pl.BlockSpec((1, MLSTM_CHUNK, W), lambda b, c: (b, c, 0))
    cb2 = conv_b.reshape(1, 2 * W)
    out = pl.pallas_call(
        _mlstm_kernel,
        grid=(batch, nc),
        in_specs=[blk(), blk(), blk(),
                  pl.BlockSpec((MLSTM_CHUNK, GATE_LANES), lambda b, c: (b * nc + c, 0)),
                  pl.BlockSpec((2 * MLSTM_HEADS, MLSTM_CHUNK), lambda b, c: (0, b * nc + c)),
                  _resident(conv_w.shape), _resident(cb2.shape)],
        out_specs=blk(),
        out_shape=jax.ShapeDtypeStruct((batch, seq, W), F32),
        scratch_shapes=[pltpu.VMEM((MLSTM_CHUNK + CONV_PAD, 2 * W), F32),
                        pltpu.VMEM((MLSTM_HEADS, HEAD_DIM, 2 * HEAD_DIM), F32),
                        pltpu.VMEM((MLSTM_HEADS, 8, 128), F32)],
        compiler_params=_params(("parallel", "arbitrary")),
        name="mlstm",
    )(mq.reshape(batch, seq, W), mk.reshape(batch, seq, W), mv.reshape(batch, seq, W), gr, gt, conv_w, cb2)
    return out.reshape(batch * seq, W)


def _mix_out_kernel(h_ref, o0_ref, o1_ref, o2_ref, l0_ref, l1_ref, l2_ref, ml_ref, g_ref, wg_ref,
                    wua_ref, wub_ref, wo_ref, out_ref):
    h = h_ref[...]
    u = _rms_norm(h, g_ref[...]).astype(BF16)
    zg = _dot(u, wg_ref[...])
    mo = zg[:, :MLSTM_WIDTH]
    ga = zg[:, MLSTM_WIDTH:MLSTM_WIDTH + D_MODEL]
    gb = zg[:, MLSTM_WIDTH + D_MODEL:]

    o_refs = (o0_ref, o1_ref, o2_ref)
    lses = (l0_ref[...], l1_ref[...], l2_ref[...])
    merged_heads = []
    for j in range(ATT_SLOTS):
        sl = slice(j * HEAD_DIM, (j + 1) * HEAD_DIM)
        ls = [l[:, j:j + 1] for l in lses]
        mx = jnp.maximum(jnp.maximum(ls[0], ls[1]), ls[2])
        ws = [jnp.exp(l - mx) for l in ls]
        inv = 1.0 / (ws[0] + ws[1] + ws[2])
        acc = (ws[0] * inv) * o_refs[0][:, sl]
        acc = acc + (ws[1] * inv) * o_refs[1][:, sl]
        acc = acc + (ws[2] * inv) * o_refs[2][:, sl]
        merged_heads.append(acc.astype(BF16))
    att = jnp.concatenate(merged_heads, axis=1)
    y_att = _dot(att, wua_ref[...])
    y_ml = _dot((jax.nn.sigmoid(mo) * ml_ref[...]).astype(BF16), wub_ref[...])
    merged = jax.nn.sigmoid(ga) * y_att + jax.nn.sigmoid(gb) * y_ml
    out_ref[...] = h + _dot(merged.astype(BF16), wo_ref[...])


def _mix_out_call(h2d, os_, lses, ml, mix_norm, w_in, w_up_att, w_up_mlstm, w_out, tm):
    n_rows = h2d.shape[0]
    g2 = mix_norm.reshape(1, D_MODEL)
    o_mo = 3 * ATT_WIDTH + 3 * MLSTM_WIDTH
    o_g = o_mo + MLSTM_WIDTH + 2 * MLSTM_HEADS
    wg = jnp.concatenate([w_in[:, o_mo:o_mo + MLSTM_WIDTH], w_in[:, o_g:o_g + 2 * D_MODEL]], axis=1).astype(BF16)
    wua = w_up_att.astype(BF16)
    wub = w_up_mlstm.astype(BF16)
    wo = w_out.astype(BF16)
    row = lambda w: pl.BlockSpec((tm, w), lambda i: (i, 0))
    return pl.pallas_call(
        _mix_out_kernel,
        grid=(n_rows // tm,),
        in_specs=[row(D_MODEL)] + [row(ATT_OUT_WIDTH)] * 3 + [row(GATE_LANES)] * 3 + [row(MLSTM_WIDTH),
                  _resident(g2.shape), _resident(wg.shape), _resident(wua.shape), _resident(wub.shape),
                  _resident(wo.shape)],
        out_specs=row(D_MODEL),
        out_shape=jax.ShapeDtypeStruct((n_rows, D_MODEL), F32),
        compiler_params=_params(("parallel",)),
        name="mix_out",
    )(h2d, *os_, *lses, ml, g2, wg, wua, wub, wo)


def _layer(x2d, p2d, batch, seq, ffn1_norm, ffn1_w_in, ffn1_w_out, mix_norm, w_in, q_gain, k_gain, conv_w, conv_b,
           i_bias, f_bias, w_up_att, w_up_mlstm, w_out, ffn2_norm, ffn2_w_in, ffn2_w_out, ple_norm, w_ple_gate,
           w_ple_proj):
    tm = 512
    w1_in, w1_out = _prep_ffn_weights(ffn1_w_in, ffn1_w_out)
    h = _ffn_call(x2d, ffn1_norm, w1_in, w1_out, tm)
    q, k, v, mq, mk, mv, gr, gt = _in_proj_call(h, mix_norm, w_in, q_gain, k_gain, i_bias, f_bias, batch, seq, tm)
    os_, lses = [], []
    for g, (window, dilation) in enumerate(ATT_GROUPS):
        assert window // dilation == ATT_BLOCK
        o, lse = _attention_call(q, k, v, g, dilation, batch, seq)
        os_.append(o)
        lses.append(lse)
    ml = _mlstm_call(mq, mk, mv, gr, gt, conv_w, conv_b, batch, seq)
    h = _mix_out_call(h, os_, lses, ml, mix_norm, w_in, w_up_att, w_up_mlstm, w_out, tm)
    w2_in, w2_out = _prep_ffn_weights(ffn2_w_in, ffn2_w_out)
    ple = (p2d, ple_norm, w_ple_gate.astype(BF16), w_ple_proj.astype(BF16))
    return _ffn_call(h, ffn2_norm, w2_in, w2_out, tm, ple=ple)


def kernel(x, p, ffn1_norm, ffn1_w_in, ffn1_w_out, mix_norm, w_in, q_gain, k_gain, conv_w, conv_b, i_bias, f_bias,
           w_up_att, w_up_mlstm, w_out, ffn2_norm, ffn2_w_in, ffn2_w_out, ple_norm, w_ple_gate, w_ple_proj):
    batch, seq, d = x.shape
    depth = p.shape[0]
    assert d == D_MODEL and seq % (ATT_GROUPS[-1][1] * ATT_BLOCK) == 0
    h = x.reshape(batch * seq, d)
    for i in range(depth):
        h = _layer(h, p[i].reshape(batch * seq, PLE_DIM), batch, seq, ffn1_norm[i], ffn1_w_in[i], ffn1_w_out[i],
                   mix_norm[i], w_in[i], q_gain[i], k_gain[i], conv_w[i], conv_b[i], i_bias[i], f_bias[i],
                   w_up_att[i], w_up_mlstm[i], w_out[i], ffn2_norm[i], ffn2_w_in[i], ffn2_w_out[i],
                   ple_norm[i], w_ple_gate[i], w_ple_proj[i])
    return h.reshape(batch, seq, d)
```

```python
import functools

import jax
import jax.numpy as jnp
from jax import lax
from jax.experimental import pallas as pl
from jax.experimental.pallas import tpu as pltpu

D_MODEL = 1024
PLE_DIM = 256
ATT_GROUPS = ((128, 1), (512, 4), (2048, 16))
N_ATT_GROUPS = 3
ATT_SLOTS = 4
ATT_HEADS = N_ATT_GROUPS * ATT_SLOTS
HEAD_DIM = 128
ATT_WIDTH = ATT_HEADS * HEAD_DIM
ATT_OUT_WIDTH = ATT_SLOTS * HEAD_DIM
ROPE_THETA = 500000.0
ROPE_DIM = HEAD_DIM // 4
ROPE_HALF = ROPE_DIM // 2
MLSTM_HEADS = 4
MLSTM_WIDTH = MLSTM_HEADS * HEAD_DIM
MLSTM_CHUNK = 128
CONV_WIDTH = 4
D_FF = 2816
NORM_EPS = 1e-6

FF_CHUNKS = ((0, 512), (512, 512), (1024, 512), (1536, 512), (2048, 512), (2560, 256))
ATT_BLOCK = 128
ATT_BLOCKS_PER_STEP = (8, 2, 1)
ATT_SLABS = ATT_SLOTS + 1
PERM_ROWS = 256
GATE_LANES = 128
CONV_PAD = 8
V7X_VMEM_LIMIT = 56 * 1024 * 1024

BF16 = jnp.bfloat16
F32 = jnp.float32


def _rms_norm(x, g):
    ms = jnp.mean(x * x, axis=-1, keepdims=True)
    return x * lax.rsqrt(ms + NORM_EPS) * g


def _silu(x):
    return x * jax.nn.sigmoid(x)


def _dot(a, b):
    return jnp.dot(a, b, preferred_element_type=F32)


def _dot_nt(a, b):
    return lax.dot_general(a, b, (((1,), (1,)), ((), ())), preferred_element_type=F32)


def _dot_tn(a, b):
    return lax.dot_general(a, b, (((0,), (0,)), ((), ())), preferred_element_type=F32)


def _resident(shape):
    nd = len(shape)
    return pl.BlockSpec(shape, lambda *_: (0,) * nd, pipeline_mode=pl.Buffered(1))


def _params(semantics):
    return pltpu.CompilerParams(dimension_semantics=semantics, vmem_limit_bytes=V7X_VMEM_LIMIT)


def _swiglu_update(x_ref, g_ref, w_in_ref, w_out_ref, acc_ref):
    u = _rms_norm(x_ref[...], g_ref[...]).astype(BF16)
    for idx, (off, width) in enumerate(FF_CHUNKS):
        a = _dot(u, w_in_ref[:, off:off + width])
        b = _dot(u, w_in_ref[:, D_FF + off:D_FF + off + width])
        part = _dot((_silu(a) * b).astype(BF16), w_out_ref[off:off + width, :])
        if idx == 0:
            acc_ref[...] = part
        else:
            acc_ref[...] += part
    return x_ref[...] + 0.5 * acc_ref[...]


def _ffn_kernel(x_ref, g_ref, w_in_ref, w_out_ref, o_ref, acc_ref):
    o_ref[...] = _swiglu_update(x_ref, g_ref, w_in_ref, w_out_ref, acc_ref)


def _ffn_ple_kernel(x_ref, p_ref, g_ref, w_in_ref, w_out_ref, pg_ref, w_pg_ref, w_pe_ref, o_ref, acc_ref):
    h = _swiglu_update(x_ref, g_ref, w_in_ref, w_out_ref, acc_ref)
    u = _rms_norm(h, pg_ref[...]).astype(BF16)
    gate = jax.nn.sigmoid(_dot(u, w_pg_ref[...]))
    pe = _dot(p_ref[...].astype(BF16), w_pe_ref[...])
    o_ref[...] = h + pe * gate


def _ffn_call(x2d, norm_g, w_in, w_out, tm, ple=None):
    n_rows = x2d.shape[0]
    row = lambda w: pl.BlockSpec((tm, w), lambda i: (i, 0))
    g2 = norm_g.reshape(1, D_MODEL)
    w_in = w_in.astype(BF16)
    w_out = w_out.astype(BF16)
    if ple is None:
        kern = _ffn_kernel
        args = (x2d, g2, w_in, w_out)
        specs = [row(D_MODEL), _resident(g2.shape), _resident(w_in.shape), _resident(w_out.shape)]
        name = "ffn"
    else:
        p2d, ple_g, w_pg, w_pe = ple
        pg2 = ple_g.reshape(1, D_MODEL)
        kern = _ffn_ple_kernel
        args = (x2d, p2d, g2, w_in, w_out, pg2, w_pg, w_pe)
        specs = [row(D_MODEL), row(PLE_DIM), _resident(g2.shape), _resident(w_in.shape),
                 _resident(w_out.shape), _resident(pg2.shape), _resident(w_pg.shape), _resident(w_pe.shape)]
        name = "ffn_ple"
    return pl.pallas_call(
        kern,
        grid=(n_rows // tm,),
        in_specs=specs,
        out_specs=row(D_MODEL),
        out_shape=jax.ShapeDtypeStruct((n_rows, D_MODEL), F32),
        scratch_shapes=[pltpu.VMEM((tm, D_MODEL), F32)],
        compiler_params=_params(("parallel",)),
        name=name,
    )(*args)


def _log_sigmoid(x):
    return jnp.minimum(x, 0.0) - jnp.log1p(jnp.exp(-jnp.abs(x)))


def _in_proj_kernel(tiles_per_seq, h_ref, hp_ref, g_ref, watt_ref, wm_ref, wg_ref, wgt_ref, qg_ref, kg_ref,
                    cos_ref, sin_ref, perm_ref, cw_ref, cb_ref, gb_ref, gbt_ref,
                    q0_ref, k0_ref, v0_ref, q1_ref, k1_ref, v1_ref, q2_ref, k2_ref, v2_ref,
                    mq_ref, mk_ref, mv_ref, gr_ref, gt_ref, xpad_ref):
    tm = h_ref.shape[0]
    gnorm = g_ref[...]
    u = _rms_norm(h_ref[...], gnorm).astype(BF16)
    qgain = qg_ref[...]
    kgain = kg_ref[...]
    att_refs = ((q0_ref, k0_ref, v0_ref), (q1_ref, k1_ref, v1_ref), (q2_ref, k2_ref, v2_ref))

    def qk_head(zh, gain, cos, sin, scale):
        y = _rms_norm(zh, gain)
        y = y * cos + pltpu.roll(y, HEAD_DIM // 2, 1) * sin
        if scale != 1.0:
            y = y * scale
        return y.astype(BF16)

    for grp, (_, dil) in enumerate(ATT_GROUPS):
        if dil == 1:
            ug = u
        else:
            pm = perm_ref[grp - 1]
            ug = jnp.concatenate(
                [_dot(pm, u[kb * PERM_ROWS:(kb + 1) * PERM_ROWS, :]) for kb in range(tm // PERM_ROWS)],
                axis=0).astype(BF16)
        z = _dot(ug, watt_ref[grp])
        cos = cos_ref[grp]
        sin = sin_ref[grp]
        outs = []
        for part, (gain, scale) in enumerate(((qgain, HEAD_DIM ** -0.5), (kgain, 1.0))):
            base = part * ATT_OUT_WIDTH
            outs.append(jnp.concatenate(
                [qk_head(z[:, base + j * HEAD_DIM:base + (j + 1) * HEAD_DIM], gain, cos, sin, scale)
                 for j in range(ATT_SLOTS)], axis=1))
        outs.append(z[:, 2 * ATT_OUT_WIDTH:].astype(BF16))
        rows_per_res = PERM_ROWS // dil
        for o_ref, val in zip(att_refs[grp], outs):
            if dil == 1:
                o_ref[0, 0] = val
            else:
                for kb in range(tm // PERM_ROWS):
                    for r in range(dil):
                        src = kb * PERM_ROWS + r * rows_per_res
                        o_ref[0, r, kb * rows_per_res:(kb + 1) * rows_per_res, :] = val[src:src + rows_per_res, :]

    zm = _dot(u, wm_ref[...])
    mv_ref[...] = zm[:, 2 * MLSTM_WIDTH:].astype(BF16)
    up = _rms_norm(hp_ref[...], gnorm).astype(BF16)
    zp = _dot(up, wm_ref[:, 0:2 * MLSTM_WIDTH])
    has_prev = (pl.program_id(0) % tiles_per_seq) > 0
    xpad_ref[0:CONV_PAD, :] = jnp.where(has_prev, zp, 0.0)
    xpad_ref[CONV_PAD:CONV_PAD + tm, :] = zm[:, 0:2 * MLSTM_WIDTH]
    acc = cb_ref[...]
    for j in range(CONV_WIDTH):
        off = CONV_PAD - (CONV_WIDTH - 1) + j
        acc = acc + xpad_ref[off:off + tm, :] * cw_ref[j:j + 1, :]
    qk = _silu(acc)
    mq_ref[...] = qk[:, 0:MLSTM_WIDTH].astype(BF16)
    mk_ref[...] = (qk[:, MLSTM_WIDTH:] * (HEAD_DIM ** -0.5)).astype(BF16)

    zr = _dot(u, wg_ref[...]) + gb_ref[...]
    lane = lax.broadcasted_iota(jnp.int32, zr.shape, 1)
    gr_ref[...] = jnp.where(lane < MLSTM_HEADS, zr, _log_sigmoid(zr))
    zt = _dot_nt(wgt_ref[...], u) + gbt_ref[...]
    sub = lax.broadcasted_iota(jnp.int32, zt.shape, 0)
    gt_ref[...] = jnp.where(sub < MLSTM_HEADS, zt, _log_sigmoid(zt))


def _rope_lane_perm():
    return (list(range(0, ROPE_HALF)) + list(range(ROPE_DIM, ROPE_DIM + 48))
            + list(range(ROPE_HALF, ROPE_DIM)) + list(range(ROPE_DIM + 48, HEAD_DIM)))


def _rope_tables(seq):
    pos = jnp.arange(seq, dtype=F32)
    inv_freq = 1.0 / (ROPE_THETA ** (jnp.arange(ROPE_HALF, dtype=F32) / ROPE_HALF))
    ang = pos[:, None] * inv_freq[None, :]
    cos, sin = jnp.cos(ang), jnp.sin(ang)
    ones = jnp.ones((seq, HEAD_DIM // 2 - ROPE_HALF), F32)
    zeros = jnp.zeros_like(ones)
    cos_t = jnp.concatenate([cos, ones, cos, ones], axis=1)
    sin_t = jnp.concatenate([-sin, zeros, sin, zeros], axis=1)
    return cos_t, sin_t


def _deinterleave_rows(t, dil):
    n, w = t.shape
    return t.reshape(n // PERM_ROWS, PERM_ROWS // dil, dil, w).transpose(0, 2, 1, 3).reshape(n, w)


def _in_proj_call(h2d, mix_norm, w_in, q_gain, k_gain, conv_w, conv_b, i_bias, f_bias, batch, seq, tm):
    n_rows = h2d.shape[0]
    perm = jnp.asarray(_rope_lane_perm(), dtype=jnp.int32)
    slot_perm = (jnp.arange(ATT_SLOTS, dtype=jnp.int32)[:, None] * HEAD_DIM + perm[None, :]).reshape(-1)
    wq, wk, wv = (w_in[:, i * ATT_WIDTH:(i + 1) * ATT_WIDTH] for i in range(3))
    w_att = []
    for g in range(N_ATT_GROUPS):
        cols = slice(g * ATT_OUT_WIDTH, (g + 1) * ATT_OUT_WIDTH)
        w_att.append(jnp.concatenate([wq[:, cols][:, slot_perm], wk[:, cols][:, slot_perm], wv[:, cols]], axis=1))
    w_att = jnp.stack(w_att).astype(BF16)
    o = 3 * ATT_WIDTH
    wm = w_in[:, o:o + 3 * MLSTM_WIDTH].astype(BF16)
    o += 4 * MLSTM_WIDTH
    w_gate = w_in[:, o:o + 2 * MLSTM_HEADS]
    wg = jnp.pad(w_gate, ((0, 0), (0, GATE_LANES - 2 * MLSTM_HEADS))).astype(BF16)
    wgt = w_gate.T.astype(BF16)
    gate_bias = jnp.concatenate([i_bias, f_bias]).astype(F32)
    gb = jnp.pad(gate_bias, (0, GATE_LANES - 2 * MLSTM_HEADS)).reshape(1, GATE_LANES)
    gbt = gate_bias.reshape(2 * MLSTM_HEADS, 1)
    qg = q_gain[perm].reshape(1, HEAD_DIM)
    kg = k_gain[perm].reshape(1, HEAD_DIM)
    cos_t, sin_t = _rope_tables(seq)
    cos_g = jnp.stack([_deinterleave_rows(cos_t, d) for _, d in ATT_GROUPS])
    sin_g = jnp.stack([_deinterleave_rows(sin_t, d) for _, d in ATT_GROUPS])
    eye = jnp.eye(PERM_ROWS, dtype=F32)
    perms = jnp.stack([_deinterleave_rows(eye, d) for _, d in ATT_GROUPS[1:]]).astype(BF16)
    g2 = mix_norm.reshape(1, D_MODEL)
    cb2 = conv_b.reshape(1, 2 * MLSTM_WIDTH)
    tiles_per_seq = seq // tm

    row = lambda w: pl.BlockSpec((tm, w), lambda i: (i, 0))
    prev_rows = pl.BlockSpec((CONV_PAD, D_MODEL), lambda i: (jnp.maximum(i * (tm // CONV_PAD) - 1, 0), 0))
    pos = pl.BlockSpec((N_ATT_GROUPS, tm, HEAD_DIM), lambda i: (0, i % tiles_per_seq, 0))
    in_specs = [row(D_MODEL), prev_rows, _resident(g2.shape), _resident(w_att.shape), _resident(wm.shape),
                _resident(wg.shape), _resident(wgt.shape), _resident(qg.shape), _resident(kg.shape), pos, pos,
                _resident(perms.shape), _resident(conv_w.shape), _resident(cb2.shape), _resident(gb.shape),
                _resident(gbt.shape)]
    out_specs, out_shape = [], []
    for _, d in ATT_GROUPS:
        spec = pl.BlockSpec((1, d, tm // d, ATT_OUT_WIDTH),
                            lambda i: (i // tiles_per_seq, 0, i % tiles_per_seq, 0))
        out_specs += [spec] * 3
        out_shape += [jax.ShapeDtypeStruct((batch, d, seq // d, ATT_OUT_WIDTH), BF16)] * 3
    out_specs += [row(MLSTM_WIDTH)] * 3 + [row(GATE_LANES),
                                             pl.BlockSpec((2 * MLSTM_HEADS, tm), lambda i: (0, i))]
    out_shape += [jax.ShapeDtypeStruct((n_rows, MLSTM_WIDTH), BF16)] * 3 + [
        jax.ShapeDtypeStruct((n_rows, GATE_LANES), F32), jax.ShapeDtypeStruct((2 * MLSTM_HEADS, n_rows), F32)]
    outs = pl.pallas_call(
        functools.partial(_in_proj_kernel, tiles_per_seq),
        grid=(n_rows // tm,),
        in_specs=in_specs,
        out_specs=out_specs,
        out_shape=out_shape,
        scratch_shapes=[pltpu.VMEM((tm + CONV_PAD, 2 * MLSTM_WIDTH), F32)],
        compiler_params=_params(("parallel",)),
        name="in_proj",
    )(h2d, h2d, g2, w_att, wm, wg, wgt, qg, kg, cos_g, sin_g, perms, conv_w, cb2, gb, gbt)
    qkv = [tuple(outs[3 * g:3 * g + 3]) for g in range(N_ATT_GROUPS)]
    return qkv, outs[9:]


def _attention_kernel(dil, nblk, q_ref, kc_ref, vc_ref, kp_ref, vp_ref, o_ref):
    qi = lax.broadcasted_iota(jnp.int32, (ATT_BLOCK, 2 * ATT_BLOCK), 0)
    kj = lax.broadcasted_iota(jnp.int32, (ATT_BLOCK, 2 * ATT_BLOCK), 1)
    band = (kj >= qi) & (kj <= qi + ATT_BLOCK)
    bias = jnp.where(band, 0.0, -jnp.inf)
    first_key = jnp.where(pl.program_id(1) > 0, 0, ATT_BLOCK)
    bias_head = jnp.where(kj >= first_key, bias, -jnp.inf)
    lane = lax.broadcasted_iota(jnp.int32, (ATT_BLOCK, HEAD_DIM), 1)

    def tile(r, blk, q, k, v, tile_bias):
        if dil == 1:
            rows = pl.ds(blk * ATT_BLOCK, ATT_BLOCK)
        else:
            rows = pl.ds(blk * ATT_BLOCK * dil + r, ATT_BLOCK, stride=dil)
        lse_slab = jnp.zeros((ATT_BLOCK, HEAD_DIM), F32)
        for j in range(ATT_SLOTS):
            sl = slice(j * HEAD_DIM, (j + 1) * HEAD_DIM)
            s = _dot_nt(q[:, sl], k[:, sl]) + tile_bias
            m = jnp.max(s, axis=-1, keepdims=True)
            e = jnp.exp(s - m)
            den = jnp.sum(e, axis=-1, keepdims=True)
            o_ref[j, rows, :] = _dot(e.astype(BF16), v[:, sl]) / den
            lse_slab = jnp.where(lane == j, m + jnp.log(den), lse_slab)
        o_ref[ATT_SLOTS, rows, :] = lse_slab

    def residue(r, carry):
        k0 = jnp.concatenate([kp_ref[0, r], kc_ref[0, r, 0:ATT_BLOCK, :]], axis=0)
        v0 = jnp.concatenate([vp_ref[0, r], vc_ref[0, r, 0:ATT_BLOCK, :]], axis=0)
        tile(r, 0, q_ref[0, r, 0:ATT_BLOCK, :], k0, v0, bias_head)
        for blk in range(1, nblk):
            keys = slice((blk - 1) * ATT_BLOCK, (blk + 1) * ATT_BLOCK)
            tile(r, blk, q_ref[0, r, blk * ATT_BLOCK:(blk + 1) * ATT_BLOCK, :],
                 kc_ref[0, r, keys, :], vc_ref[0, r, keys, :], bias)
        return carry

    if dil == 1:
        residue(0, 0)
    else:
        lax.fori_loop(0, dil, residue, 0, unroll=2)


def _attention_call(q, k, v, group, dilation, nblk, batch, seq):
    n = seq // dilation
    steps = n // (nblk * ATT_BLOCK)
    span = nblk * ATT_BLOCK * dilation
    cur = pl.BlockSpec((1, dilation, nblk * ATT_BLOCK, ATT_OUT_WIDTH), lambda b, i: (b, 0, i, 0))
    prev = pl.BlockSpec((1, dilation, ATT_BLOCK, ATT_OUT_WIDTH),
                        lambda b, i: (b, 0, jnp.maximum(i * nblk - 1, 0), 0))
    return pl.pallas_call(
        functools.partial(_attention_kernel, dilation, nblk),
        grid=(batch, steps),
        in_specs=[cur, cur, cur, prev, prev],
        out_specs=pl.BlockSpec((ATT_SLABS, span, HEAD_DIM), lambda b, i: (0, b * steps + i, 0)),
        out_shape=jax.ShapeDtypeStruct((ATT_SLABS, batch * seq, HEAD_DIM), F32),
        compiler_params=_params(("parallel", "arbitrary")),
        name=f"attention_g{group}",
    )(q, k, v, k, v)


def _mlstm_kernel(mq_ref, mk_ref, mv_ref, gr_ref, gt_ref, o_ref, state_ref, m_ref):
    L = MLSTM_CHUNK
    H = MLSTM_HEADS

    @pl.when(pl.program_id(1) == 0)
    def _():
        state_ref[...] = jnp.zeros_like(state_ref)
        m_ref[...] = jnp.zeros_like(m_ref)

    gr = gr_ref[...]
    gt = gt_ref[...]
    ti = lax.broadcasted_iota(jnp.int32, (L, L), 0)
    si = lax.broadcasted_iota(jnp.int32, (L, L), 1)
    causal = ti >= si
    tri = causal.astype(F32)
    b_rows = lax.dot_general(gt, tri, (((1,), (1,)), ((), ())), precision=lax.Precision.HIGHEST,
                             preferred_element_type=F32)
    b_cols = jnp.dot(tri, gr, precision=lax.Precision.HIGHEST, preferred_element_type=F32)
    ones = jnp.ones((L, HEAD_DIM), BF16)

    for hd in range(H):
        sl = slice(hd * HEAD_DIM, (hd + 1) * HEAD_DIM)
        q = mq_ref[0, :, sl]
        k = mk_ref[0, :, sl]
        vext = jnp.concatenate([mv_ref[0, :, sl], ones], axis=1)
        li_row = gt[hd:hd + 1, :]
        li_col = gr[:, hd:hd + 1]
        b_row = b_rows[H + hd:H + hd + 1, :]
        b_col = b_cols[:, H + hd:H + hd + 1]
        m_prev = m_ref[hd, 0:1, 0:1]
        state = state_ref[hd]

        dm = jnp.where(causal, b_col - b_row + li_row, -jnp.inf)
        inter = b_col + m_prev
        m_t = jnp.maximum(inter, jnp.max(dm, axis=-1, keepdims=True))
        wgt = jnp.exp(dm - m_t)
        a = jnp.exp(inter - m_t)
        s = _dot_nt(q, k) * wgt
        ext = a * _dot(q, state.astype(BF16)) + _dot(s.astype(BF16), vext)
        num = ext[:, :HEAD_DIM]
        den = ext[:, HEAD_DIM:]
        o_ref[0, :, sl] = num / jnp.maximum(jnp.abs(den), jnp.exp(-m_t))

        b_last = b_col[L - 1:L, :]
        g = b_last - b_col + li_col
        m_new = jnp.maximum(b_last + m_prev, jnp.max(g, axis=0, keepdims=True))
        wk = jnp.exp(g - m_new)
        decay = jnp.exp(b_last + m_prev - m_new)
        state_ref[hd] = decay * state + _dot_tn((k * wk).astype(BF16), vext)
        m_ref[hd] = jnp.broadcast_to(m_new, m_ref.shape[1:])


def _mlstm_call(mq, mk, mv, gr, gt, batch, seq):
    nc = seq // MLSTM_CHUNK
    W = MLSTM_WIDTH
    blk = lambda: pl.BlockSpec((1, MLSTM_CHUNK, W), lambda b, c: (b, c, 0))
    out = pl.pallas_call(
        _mlstm_kernel,
        grid=(batch, nc),
        in_specs=[blk(), blk(), blk(),
                  pl.BlockSpec((MLSTM_CHUNK, GATE_LANES), lambda b, c: (b * nc + c, 0)),
                  pl.BlockSpec((2 * MLSTM_HEADS, MLSTM_CHUNK), lambda b, c: (0, b * nc + c))],
        out_specs=blk(),
        out_shape=jax.ShapeDtypeStruct((batch, seq, W), F32),
        scratch_shapes=[pltpu.VMEM((MLSTM_HEADS, HEAD_DIM, 2 * HEAD_DIM), F32),
                        pltpu.VMEM((MLSTM_HEADS, 8, 128), F32)],
        compiler_params=_params(("parallel", "arbitrary")),
        name="mlstm",
    )(mq.reshape(batch, seq, W), mk.reshape(batch, seq, W), mv.reshape(batch, seq, W), gr, gt)
    return out.reshape(batch * seq, W)


def _mix_out_kernel(h_ref, a0_ref, a1_ref, a2_ref, ml_ref, g_ref, wg_ref, wua_ref, wub_ref, wo_ref, out_ref):
    h = h_ref[...]
    u = _rms_norm(h, g_ref[...]).astype(BF16)
    zg = _dot(u, wg_ref[...])
    mo = zg[:, :MLSTM_WIDTH]
    ga = zg[:, MLSTM_WIDTH:MLSTM_WIDTH + D_MODEL]
    gb = zg[:, MLSTM_WIDTH + D_MODEL:]

    a_refs = (a0_ref, a1_ref, a2_ref)
    lses = [a[ATT_SLOTS] for a in a_refs]
    merged_heads = []
    for j in range(ATT_SLOTS):
        ls = [l[:, j:j + 1] for l in lses]
        mx = jnp.maximum(jnp.maximum(ls[0], ls[1]), ls[2])
        ws = [jnp.exp(l - mx) for l in ls]
        inv = 1.0 / (ws[0] + ws[1] + ws[2])
        acc = (ws[0] * inv) * a_refs[0][j]
        acc = acc + (ws[1] * inv) * a_refs[1][j]
        acc = acc + (ws[2] * inv) * a_refs[2][j]
        merged_heads.append(acc.astype(BF16))
    att = jnp.concatenate(merged_heads, axis=1)
    y_att = _dot(att, wua_ref[...])
    y_ml = _dot((jax.nn.sigmoid(mo) * ml_ref[...]).astype(BF16), wub_ref[...])
    merged = jax.nn.sigmoid(ga) * y_att + jax.nn.sigmoid(gb) * y_ml
    out_ref[...] = h + _dot(merged.astype(BF16), wo_ref[...])


def _mix_out_call(h2d, att, ml, mix_norm, w_in, w_up_att, w_up_mlstm, w_out, tm):
    n_rows = h2d.shape[0]
    g2 = mix_norm.reshape(1, D_MODEL)
    o_mo = 3 * ATT_WIDTH + 3 * MLSTM_WIDTH
    o_g = o_mo + MLSTM_WIDTH + 2 * MLSTM_HEADS
    wg = jnp.concatenate([w_in[:, o_mo:o_mo + MLSTM_WIDTH], w_in[:, o_g:o_g + 2 * D_MODEL]], axis=1).astype(BF16)
    wua = w_up_att.astype(BF16)
    wub = w_up_mlstm.astype(BF16)
    wo = w_out.astype(BF16)
    row = lambda w: pl.BlockSpec((tm, w), lambda i: (i, 0))
    slabs = pl.BlockSpec((ATT_SLABS, tm, HEAD_DIM), lambda i: (0, i, 0))
    return pl.pallas_call(
        _mix_out_kernel,
        grid=(n_rows // tm,),
        in_specs=[row(D_MODEL), slabs, slabs, slabs, row(MLSTM_WIDTH),
                  _resident(g2.shape), _resident(wg.shape), _resident(wua.shape), _resident(wub.shape),
                  _resident(wo.shape)],
        out_specs=row(D_MODEL),
        out_shape=jax.ShapeDtypeStruct((n_rows, D_MODEL), F32),
        compiler_params=_params(("parallel",)),
        name="mix_out",
    )(h2d, *att, ml, g2, wg, wua, wub, wo)


def _layer(x2d, p2d, batch, seq, ffn1_norm, ffn1_w_in, ffn1_w_out, mix_norm, w_in, q_gain, k_gain, conv_w, conv_b,
           i_bias, f_bias, w_up_att, w_up_mlstm, w_out, ffn2_norm, ffn2_w_in, ffn2_w_out, ple_norm, w_ple_gate,
           w_ple_proj):
    tm = 512
    h = _ffn_call(x2d, ffn1_norm, ffn1_w_in, ffn1_w_out, tm)
    qkv, (mq, mk, mv, gr, gt) = _in_proj_call(h, mix_norm, w_in, q_gain, k_gain, conv_w, conv_b, i_bias, f_bias,
                                              batch, seq, tm)
    att = []
    for g, (window, dilation) in enumerate(ATT_GROUPS):
        assert window // dilation == ATT_BLOCK
        att.append(_attention_call(*qkv[g], g, dilation, ATT_BLOCKS_PER_STEP[g], batch, seq))
    ml = _mlstm_call(mq, mk, mv, gr, gt, batch, seq)
    h = _mix_out_call(h, att, ml, mix_norm, w_in, w_up_att, w_up_mlstm, w_out, tm)
    ple = (p2d, ple_norm, w_ple_gate.astype(BF16), w_ple_proj.astype(BF16))
    return _ffn_call(h, ffn2_norm, ffn2_w_in, ffn2_w_out, tm, ple=ple)


def kernel(x, p, ffn1_norm, ffn1_w_in, ffn1_w_out, mix_norm, w_in, q_gain, k_gain, conv_w, conv_b, i_bias, f_bias,
           w_up_att, w_up_mlstm, w_out, ffn2_norm, ffn2_w_in, ffn2_w_out, ple_norm, w_ple_gate, w_ple_proj):
    batch, seq, d = x.shape
    depth = p.shape[0]
    assert d == D_MODEL and seq % (ATT_GROUPS[-1][1] * ATT_BLOCK) == 0
    h = x.reshape(batch * seq, d)
    for i in range(depth):
        h = _layer(h, p[i].reshape(batch * seq, PLE_DIM), batch, seq, ffn1_norm[i], ffn1_w_in[i], ffn1_w_out[i],
                   mix_norm[i], w_in[i], q_gain[i], k_gain[i], conv_w[i], conv_b[i], i_bias[i], f_bias[i],
                   w_up_att[i], w_up_mlstm[i], w_out[i], ffn2_norm[i], ffn2_w_in[i], ffn2_w_out[i],
                   ple_norm[i], w_ple_gate[i], w_ple_proj[i])
    return h.reshape(batch, seq, d)
```

```python
import functools

import jax
import jax.numpy as jnp
from jax import lax
from jax.experimental import pallas as pl
from jax.experimental.pallas import tpu as pltpu

D_MODEL = 1024
PLE_DIM = 256
ATT_GROUPS = ((128, 1), (512, 4), (2048, 16))
N_ATT_GROUPS = 3
ATT_SLOTS = 4
ATT_HEADS = N_ATT_GROUPS * ATT_SLOTS
HEAD_DIM = 128
ATT_WIDTH = ATT_HEADS * HEAD_DIM
ATT_OUT_WIDTH = ATT_SLOTS * HEAD_DIM
ROPE_THETA = 500000.0
ROPE_DIM = HEAD_DIM // 4
ROPE_HALF = ROPE_DIM // 2
MLSTM_HEADS = 4
MLSTM_WIDTH = MLSTM_HEADS * HEAD_DIM
MLSTM_CHUNK = 128
CONV_WIDTH = 4
D_FF = 2816
NORM_EPS = 1e-6

FF_CHUNKS = ((0, 512), (512, 512), (1024, 512), (1536, 512), (2048, 512), (2560, 256))
ATT_BLOCK = 128
ATT_BLOCKS_PER_STEP = (8, 2, 1)
ATT_TILES_PER_ITER = 8
ATT_SLABS = ATT_SLOTS + 1
PERM_ROWS = 256
GATE_LANES = 128
CONV_PAD = 8
V7X_VMEM_LIMIT = 56 * 1024 * 1024

BF16 = jnp.bfloat16
F32 = jnp.float32


def _rms_norm(x, g):
    ms = jnp.mean(x * x, axis=-1, keepdims=True)
    return x * lax.rsqrt(ms + NORM_EPS) * g


def _silu(x):
    return x * jax.nn.sigmoid(x)


def _dot(a, b):
    return jnp.dot(a, b, preferred_element_type=F32)


def _dot_nt(a, b):
    return lax.dot_general(a, b, (((1,), (1,)), ((), ())), preferred_element_type=F32)


def _dot_tn(a, b):
    return lax.dot_general(a, b, (((0,), (0,)), ((), ())), preferred_element_type=F32)


def _resident(shape):
    nd = len(shape)
    return pl.BlockSpec(shape, lambda *_: (0,) * nd, pipeline_mode=pl.Buffered(1))


def _params(semantics):
    return pltpu.CompilerParams(dimension_semantics=semantics, vmem_limit_bytes=V7X_VMEM_LIMIT)


def _swiglu_update(x_ref, g_ref, w_in_ref, w_out_ref, acc_ref):
    u = _rms_norm(x_ref[...], g_ref[...]).astype(BF16)
    for idx, (off, width) in enumerate(FF_CHUNKS):
        a = _dot(u, w_in_ref[:, off:off + width])
        b = _dot(u, w_in_ref[:, D_FF + off:D_FF + off + width])
        part = _dot((_silu(a) * b).astype(BF16), w_out_ref[off:off + width, :])
        if idx == 0:
            acc_ref[...] = part
        else:
            acc_ref[...] += part
    return x_ref[...] + 0.5 * acc_ref[...]


def _ffn_kernel(x_ref, g_ref, w_in_ref, w_out_ref, o_ref, acc_ref):
    o_ref[...] = _swiglu_update(x_ref, g_ref, w_in_ref, w_out_ref, acc_ref)


def _ffn_ple_kernel(x_ref, p_ref, g_ref, w_in_ref, w_out_ref, pg_ref, w_pg_ref, w_pe_ref, o_ref, acc_ref):
    h = _swiglu_update(x_ref, g_ref, w_in_ref, w_out_ref, acc_ref)
    u = _rms_norm(h, pg_ref[...]).astype(BF16)
    gate = jax.nn.sigmoid(_dot(u, w_pg_ref[...]))
    pe = _dot(p_ref[...].astype(BF16), w_pe_ref[...])
    o_ref[...] = h + pe * gate


def _ffn_call(x2d, norm_g, w_in, w_out, tm, ple=None):
    n_rows = x2d.shape[0]
    row = lambda w: pl.BlockSpec((tm, w), lambda i: (i, 0))
    g2 = norm_g.reshape(1, D_MODEL)
    w_in = w_in.astype(BF16)
    w_out = w_out.astype(BF16)
    if ple is None:
        kern = _ffn_kernel
        args = (x2d, g2, w_in, w_out)
        specs = [row(D_MODEL), _resident(g2.shape), _resident(w_in.shape), _resident(w_out.shape)]
        name = "ffn"
    else:
        p2d, ple_g, w_pg, w_pe = ple
        pg2 = ple_g.reshape(1, D_MODEL)
        kern = _ffn_ple_kernel
        args = (x2d, p2d, g2, w_in, w_out, pg2, w_pg, w_pe)
        specs = [row(D_MODEL), row(PLE_DIM), _resident(g2.shape), _resident(w_in.shape),
                 _resident(w_out.shape), _resident(pg2.shape), _resident(w_pg.shape), _resident(w_pe.shape)]
        name = "ffn_ple"
    return pl.pallas_call(
        kern,
        grid=(n_rows // tm,),
        in_specs=specs,
        out_specs=row(D_MODEL),
        out_shape=jax.ShapeDtypeStruct((n_rows, D_MODEL), F32),
        scratch_shapes=[pltpu.VMEM((tm, D_MODEL), F32)],
        compiler_params=_params(("parallel",)),
        name=name,
    )(*args)


def _log_sigmoid(x):
    return jnp.minimum(x, 0.0) - jnp.log1p(jnp.exp(-jnp.abs(x)))


def _in_proj_kernel(h_ref, hp_ref, g_ref, watt_ref, wm_ref, wg_ref, wgt_ref, qg_ref, kg_ref,
                    cos_ref, sin_ref, perm_ref, cw_ref, cb_ref, gb_ref, gbt_ref,
                    q0_ref, k0_ref, v0_ref, q1_ref, k1_ref, v1_ref, q2_ref, k2_ref, v2_ref,
                    mq_ref, mk_ref, mv_ref, gr_ref, gt_ref, xpad_ref):
    tm = h_ref.shape[0]
    gnorm = g_ref[...]
    u = _rms_norm(h_ref[...], gnorm).astype(BF16)
    qgain = qg_ref[...]
    kgain = kg_ref[...]
    att_refs = ((q0_ref, k0_ref, v0_ref), (q1_ref, k1_ref, v1_ref), (q2_ref, k2_ref, v2_ref))

    def qk_head(zh, gain, cos, sin, scale):
        y = _rms_norm(zh, gain)
        y = y * cos + pltpu.roll(y, HEAD_DIM // 2, 1) * sin
        if scale != 1.0:
            y = y * scale
        return y.astype(BF16)

    for grp, (_, dil) in enumerate(ATT_GROUPS):
        if dil == 1:
            ug = u
        else:
            pm = perm_ref[grp - 1]
            ug = jnp.concatenate(
                [_dot(pm, u[kb * PERM_ROWS:(kb + 1) * PERM_ROWS, :]) for kb in range(tm // PERM_ROWS)],
                axis=0).astype(BF16)
        z = _dot(ug, watt_ref[grp])
        cos = cos_ref[grp]
        sin = sin_ref[grp]
        outs = []
        for part, (gain, scale) in enumerate(((qgain, HEAD_DIM ** -0.5), (kgain, 1.0))):
            base = part * ATT_OUT_WIDTH
            outs.append(jnp.concatenate(
                [qk_head(z[:, base + j * HEAD_DIM:base + (j + 1) * HEAD_DIM], gain, cos, sin, scale)
                 for j in range(ATT_SLOTS)], axis=1))
        outs.append(z[:, 2 * ATT_OUT_WIDTH:].astype(BF16))
        rows_per_res = PERM_ROWS // dil
        for o_ref, val in zip(att_refs[grp], outs):
            if dil == 1:
                o_ref[0, 0] = val
            else:
                for kb in range(tm // PERM_ROWS):
                    for r in range(dil):
                        src = kb * PERM_ROWS + r * rows_per_res
                        o_ref[0, r, kb * rows_per_res:(kb + 1) * rows_per_res, :] = val[src:src + rows_per_res, :]

    zm = _dot(u, wm_ref[...])
    mv_ref[...] = zm[:, 2 * MLSTM_WIDTH:].astype(BF16)
    up = _rms_norm(hp_ref[...], gnorm).astype(BF16)
    zp = _dot(up, wm_ref[:, 0:2 * MLSTM_WIDTH])
    has_prev = pl.program_id(0) > 0
    xpad_ref[0:CONV_PAD, :] = jnp.where(has_prev, zp, 0.0)
    xpad_ref[CONV_PAD:CONV_PAD + tm, :] = zm[:, 0:2 * MLSTM_WIDTH]
    acc = cb_ref[...]
    for j in range(CONV_WIDTH):
        off = CONV_PAD - (CONV_WIDTH - 1) + j
        acc = acc + xpad_ref[off:off + tm, :] * cw_ref[j:j + 1, :]
    qk = _silu(acc)
    mq_ref[...] = qk[:, 0:MLSTM_WIDTH].astype(BF16)
    mk_ref[...] = (qk[:, MLSTM_WIDTH:] * (HEAD_DIM ** -0.5)).astype(BF16)

    zr = _dot(u, wg_ref[...]) + gb_ref[...]
    lane = lax.broadcasted_iota(jnp.int32, zr.shape, 1)
    gr_ref[...] = jnp.where(lane < MLSTM_HEADS, zr, _log_sigmoid(zr))
    zt = _dot_nt(wgt_ref[...], u) + gbt_ref[...]
    sub = lax.broadcasted_iota(jnp.int32, zt.shape, 0)
    gt_ref[0] = jnp.where(sub < MLSTM_HEADS, zt, _log_sigmoid(zt))


def _rope_lane_perm():
    return (list(range(0, ROPE_HALF)) + list(range(ROPE_DIM, ROPE_DIM + 48))
            + list(range(ROPE_HALF, ROPE_DIM)) + list(range(ROPE_DIM + 48, HEAD_DIM)))


def _deinterleave_rows(t, dil):
    n, w = t.shape
    return t.reshape(n // PERM_ROWS, PERM_ROWS // dil, dil, w).transpose(0, 2, 1, 3).reshape(n, w)


def _rope_tables(seq):
    pos = jnp.arange(seq, dtype=F32).reshape(seq, 1)
    pos_g = jnp.stack([_deinterleave_rows(pos, d) for _, d in ATT_GROUPS])
    inv_freq = 1.0 / (ROPE_THETA ** (jnp.arange(ROPE_HALF, dtype=F32) / ROPE_HALF))
    rest = jnp.zeros((HEAD_DIM // 2 - ROPE_HALF,), F32)
    freq = jnp.concatenate([inv_freq, rest, inv_freq, rest])
    sign = jnp.concatenate([-jnp.ones((ROPE_HALF,), F32), rest, jnp.ones((ROPE_HALF,), F32), rest])
    ang = pos_g * freq
    return jnp.cos(ang), jnp.sin(ang) * sign


def _in_proj_call(h2d, mix_norm, w_in, q_gain, k_gain, conv_w, conv_b, i_bias, f_bias, batch, seq, tm):
    n_rows = h2d.shape[0]
    perm = jnp.asarray(_rope_lane_perm(), dtype=jnp.int32)
    slot_perm = (jnp.arange(ATT_SLOTS, dtype=jnp.int32)[:, None] * HEAD_DIM + perm[None, :]).reshape(-1)
    wq, wk, wv = (w_in[:, i * ATT_WIDTH:(i + 1) * ATT_WIDTH] for i in range(3))
    w_att = []
    for g in range(N_ATT_GROUPS):
        cols = slice(g * ATT_OUT_WIDTH, (g + 1) * ATT_OUT_WIDTH)
        w_att.append(jnp.concatenate([wq[:, cols][:, slot_perm], wk[:, cols][:, slot_perm], wv[:, cols]], axis=1))
    w_att = jnp.stack(w_att).astype(BF16)
    o = 3 * ATT_WIDTH
    wm = w_in[:, o:o + 3 * MLSTM_WIDTH].astype(BF16)
    o += 4 * MLSTM_WIDTH
    w_gate = w_in[:, o:o + 2 * MLSTM_HEADS]
    wg = jnp.pad(w_gate, ((0, 0), (0, GATE_LANES - 2 * MLSTM_HEADS))).astype(BF16)
    wgt = w_gate.T.astype(BF16)
    gate_bias = jnp.concatenate([i_bias, f_bias]).astype(F32)
    gb = jnp.pad(gate_bias, (0, GATE_LANES - 2 * MLSTM_HEADS)).reshape(1, GATE_LANES)
    gbt = gate_bias.reshape(2 * MLSTM_HEADS, 1)
    qg = q_gain[perm].reshape(1, HEAD_DIM)
    kg = k_gain[perm].reshape(1, HEAD_DIM)
    cos_g, sin_g = _rope_tables(seq)
    eye = jnp.eye(PERM_ROWS, dtype=F32)
    perms = jnp.stack([_deinterleave_rows(eye, d) for _, d in ATT_GROUPS[1:]]).astype(BF16)
    g2 = mix_norm.reshape(1, D_MODEL)
    cb2 = conv_b.reshape(1, 2 * MLSTM_WIDTH)
    tiles_per_seq = seq // tm

    row = lambda w: pl.BlockSpec((tm, w), lambda t, b: (b * tiles_per_seq + t, 0))
    prev_rows = pl.BlockSpec(
        (CONV_PAD, D_MODEL), lambda t, b: (jnp.maximum((b * tiles_per_seq + t) * (tm // CONV_PAD) - 1, 0), 0))
    pos = pl.BlockSpec((N_ATT_GROUPS, tm, HEAD_DIM), lambda t, b: (0, t, 0))
    in_specs = [row(D_MODEL), prev_rows, _resident(g2.shape), _resident(w_att.shape), _resident(wm.shape),
                _resident(wg.shape), _resident(wgt.shape), _resident(qg.shape), _resident(kg.shape), pos, pos,
                _resident(perms.shape), _resident(conv_w.shape), _resident(cb2.shape), _resident(gb.shape),
                _resident(gbt.shape)]
    out_specs, out_shape = [], []
    for _, d in ATT_GROUPS:
        spec = pl.BlockSpec((1, d, tm // d, ATT_OUT_WIDTH), lambda t, b: (b, 0, t, 0))
        out_specs += [spec] * 3
        out_shape += [jax.ShapeDtypeStruct((batch, d, seq // d, ATT_OUT_WIDTH), BF16)] * 3
    out_specs += [row(MLSTM_WIDTH)] * 3 + [row(GATE_LANES),
                                             pl.BlockSpec((1, 2 * MLSTM_HEADS, tm), lambda t, b: (b, 0, t))]
    out_shape += [jax.ShapeDtypeStruct((n_rows, MLSTM_WIDTH), BF16)] * 3 + [
        jax.ShapeDtypeStruct((n_rows, GATE_LANES), F32), jax.ShapeDtypeStruct((batch, 2 * MLSTM_HEADS, seq), F32)]
    outs = pl.pallas_call(
        _in_proj_kernel,
        grid=(tiles_per_seq, batch),
        in_specs=in_specs,
        out_specs=out_specs,
        out_shape=out_shape,
        scratch_shapes=[pltpu.VMEM((tm + CONV_PAD, 2 * MLSTM_WIDTH), F32)],
        compiler_params=_params(("parallel", "parallel")),
        name="in_proj",
    )(h2d, h2d, g2, w_att, wm, wg, wgt, qg, kg, cos_g, sin_g, perms, conv_w, cb2, gb, gbt)
    qkv = [tuple(outs[3 * g:3 * g + 3]) for g in range(N_ATT_GROUPS)]
    return qkv, outs[9:]


def _attention_kernel(dil, nblk, q_ref, kc_ref, vc_ref, kp_ref, vp_ref, o_ref):
    qi = lax.broadcasted_iota(jnp.int32, (ATT_BLOCK, 2 * ATT_BLOCK), 0)
    kj = lax.broadcasted_iota(jnp.int32, (ATT_BLOCK, 2 * ATT_BLOCK), 1)
    band = (kj >= qi) & (kj <= qi + ATT_BLOCK)
    bias = jnp.where(band, 0.0, -jnp.inf)
    first_key = jnp.where(pl.program_id(1) > 0, 0, ATT_BLOCK)
    bias_head = jnp.where(kj >= first_key, bias, -jnp.inf)
    lane = lax.broadcasted_iota(jnp.int32, (ATT_BLOCK, HEAD_DIM), 1)

    heads = [slice(j * HEAD_DIM, (j + 1) * HEAD_DIM) for j in range(ATT_SLOTS)]

    def scores(r, blk):
        q = q_ref[0, r, blk * ATT_BLOCK:(blk + 1) * ATT_BLOCK, :]
        if blk == 0:
            k = jnp.concatenate([kp_ref[0, r], kc_ref[0, r, 0:ATT_BLOCK, :]], axis=0)
            tile_bias = bias_head
        else:
            k = kc_ref[0, r, (blk - 1) * ATT_BLOCK:(blk + 1) * ATT_BLOCK, :]
            tile_bias = bias
        return [_dot_nt(q[:, sl], k[:, sl]) + tile_bias for sl in heads]

    def softmax(s_list):
        out = []
        for s in s_list:
            m = jnp.max(s, axis=-1, keepdims=True)
            e = jnp.exp(s - m)
            den = jnp.sum(e, axis=-1, keepdims=True)
            out.append((e.astype(BF16), den, m + jnp.log(den)))
        return out

    def weighted_values(r, blk, p_list):
        if blk == 0:
            v = jnp.concatenate([vp_ref[0, r], vc_ref[0, r, 0:ATT_BLOCK, :]], axis=0)
        else:
            v = vc_ref[0, r, (blk - 1) * ATT_BLOCK:(blk + 1) * ATT_BLOCK, :]
        return [_dot(p, v[:, sl]) for (p, _, _), sl in zip(p_list, heads)]

    def store(r, blk, p_list, pv_list):
        if dil == 1:
            rows = pl.ds(blk * ATT_BLOCK, ATT_BLOCK)
        else:
            rows = pl.ds(blk * ATT_BLOCK * dil + r, ATT_BLOCK, stride=dil)
        lse_slab = jnp.zeros((ATT_BLOCK, HEAD_DIM), F32)
        for j, ((_, den, lse), pv) in enumerate(zip(p_list, pv_list)):
            o_ref[j, rows, :] = pv / den
            lse_slab = jnp.where(lane == j, lse, lse_slab)
        o_ref[ATT_SLOTS, rows, :] = lse_slab

    def run_tiles(tiles):
        s_next = scores(*tiles[0])
        for idx, (r, blk) in enumerate(tiles):
            s_cur = s_next
            if idx + 1 < len(tiles):
                s_next = scores(*tiles[idx + 1])
            p_list = softmax(s_cur)
            store(r, blk, p_list, weighted_values(r, blk, p_list))

    res_per_iter = ATT_TILES_PER_ITER // nblk
    if dil == res_per_iter:
        run_tiles([(r, blk) for r in range(dil) for blk in range(nblk)])
    else:
        def body(it, carry):
            run_tiles([(it * res_per_iter + rr, blk) for rr in range(res_per_iter) for blk in range(nblk)])
            return carry
        lax.fori_loop(0, dil // res_per_iter, body, 0)


def _attention_call(q, k, v, group, dilation, nblk, batch, seq):
    n = seq // dilation
    steps = n // (nblk * ATT_BLOCK)
    span = nblk * ATT_BLOCK * dilation
    cur = pl.BlockSpec((1, dilation, nblk * ATT_BLOCK, ATT_OUT_WIDTH), lambda b, i: (b, 0, i, 0))
    prev = pl.BlockSpec((1, dilation, ATT_BLOCK, ATT_OUT_WIDTH),
                        lambda b, i: (b, 0, jnp.maximum(i * nblk - 1, 0), 0))
    return pl.pallas_call(
        functools.partial(_attention_kernel, dilation, nblk),
        grid=(batch, steps),
        in_specs=[cur, cur, cur, prev, prev],
        out_specs=pl.BlockSpec((ATT_SLABS, span, HEAD_DIM), lambda b, i: (0, b * steps + i, 0)),
        out_shape=jax.ShapeDtypeStruct((ATT_SLABS, batch * seq, HEAD_DIM), F32),
        compiler_params=_params(("parallel", "arbitrary")),
        name=f"attention_g{group}",
    )(q, k, v, k, v)


def _mlstm_kernel(mq_ref, mk_ref, mv_ref, gr_ref, gt_ref, o_ref, state_ref, m_ref):
    L = MLSTM_CHUNK
    H = MLSTM_HEADS

    @pl.when(pl.program_id(0) == 0)
    def _():
        state_ref[...] = jnp.zeros_like(state_ref)
        m_ref[...] = jnp.zeros_like(m_ref)

    ti = lax.broadcasted_iota(jnp.int32, (L, L), 0)
    si = lax.broadcasted_iota(jnp.int32, (L, L), 1)
    causal = ti >= si
    tri = causal.astype(F32)
    ones = jnp.ones((L, HEAD_DIM), BF16)

    nbatch = mq_ref.shape[0]
    streams = [divmod(bh, H) for bh in range(nbatch * H)]
    lanes = lambda hd: slice(hd * HEAD_DIM, (hd + 1) * HEAD_DIM)

    grs = [gr_ref[bi] for bi in range(nbatch)]
    gts = [gt_ref[bi] for bi in range(nbatch)]
    b_rows = [lax.dot_general(gt, tri, (((1,), (1,)), ((), ())), precision=lax.Precision.HIGHEST,
                              preferred_element_type=F32) for gt in gts]
    b_cols = [jnp.dot(tri, gr, precision=lax.Precision.HIGHEST, preferred_element_type=F32) for gr in grs]

    qs = [mq_ref[bi, :, lanes(hd)] for bi, hd in streams]
    ks = [mk_ref[bi, :, lanes(hd)] for bi, hd in streams]
    vexts = [jnp.concatenate([mv_ref[bi, :, lanes(hd)], ones], axis=1) for bi, hd in streams]
    states = [state_ref[bh] for bh in range(len(streams))]
    m_prevs = [m_ref[bh, 0:1, 0:1] for bh in range(len(streams))]
    qk = [_dot_nt(q, k) for q, k in zip(qs, ks)]
    q_state = [_dot(q, st.astype(BF16)) for q, st in zip(qs, states)]

    upd, decays, m_news = [], [], []
    for bh, (bi, hd) in enumerate(streams):
        b_col = b_cols[bi][:, H + hd:H + hd + 1]
        b_last = b_col[L - 1:L, :]
        g = b_last - b_col + grs[bi][:, hd:hd + 1]
        m_new = jnp.maximum(b_last + m_prevs[bh], jnp.max(g, axis=0, keepdims=True))
        wk = jnp.exp(g - m_new)
        decays.append(jnp.exp(b_last + m_prevs[bh] - m_new))
        m_news.append(m_new)
        upd.append(_dot_tn((ks[bh] * wk).astype(BF16), vexts[bh]))

    s_list, a_list, floor_list = [], [], []
    for bh, (bi, hd) in enumerate(streams):
        b_col = b_cols[bi][:, H + hd:H + hd + 1]
        b_row = b_rows[bi][H + hd:H + hd + 1, :]
        dm = jnp.where(causal, b_col - b_row + gts[bi][hd:hd + 1, :], -jnp.inf)
        inter = b_col + m_prevs[bh]
        m_t = jnp.maximum(inter, jnp.max(dm, axis=-1, keepdims=True))
        s_list.append((qk[bh] * jnp.exp(dm - m_t)).astype(BF16))
        a_list.append(jnp.exp(inter - m_t))
        floor_list.append(jnp.exp(-m_t))

    sv = [_dot(s, vext) for s, vext in zip(s_list, vexts)]
    for bh, (bi, hd) in enumerate(streams):
        ext = a_list[bh] * q_state[bh] + sv[bh]
        num = ext[:, :HEAD_DIM]
        den = ext[:, HEAD_DIM:]
        o_ref[bi, :, lanes(hd)] = num / jnp.maximum(jnp.abs(den), floor_list[bh])
        state_ref[bh] = decays[bh] * states[bh] + upd[bh]
        m_ref[bh] = jnp.broadcast_to(m_news[bh], m_ref.shape[1:])


def _mlstm_call(mq, mk, mv, gr, gt, batch, seq):
    nc = seq // MLSTM_CHUNK
    W = MLSTM_WIDTH
    blk = lambda w: pl.BlockSpec((batch, MLSTM_CHUNK, w), lambda c: (0, c, 0))
    out = pl.pallas_call(
        _mlstm_kernel,
        grid=(nc,),
        in_specs=[blk(W), blk(W), blk(W), blk(GATE_LANES),
                  pl.BlockSpec((batch, 2 * MLSTM_HEADS, MLSTM_CHUNK), lambda c: (0, 0, c))],
        out_specs=blk(W),
        out_shape=jax.ShapeDtypeStruct((batch, seq, W), F32),
        scratch_shapes=[pltpu.VMEM((batch * MLSTM_HEADS, HEAD_DIM, 2 * HEAD_DIM), F32),
                        pltpu.VMEM((batch * MLSTM_HEADS, 8, 128), F32)],
        compiler_params=_params(("arbitrary",)),
        name="mlstm",
    )(mq.reshape(batch, seq, W), mk.reshape(batch, seq, W), mv.reshape(batch, seq, W),
      gr.reshape(batch, seq, GATE_LANES), gt)
    return out.reshape(batch * seq, W)


def _mix_out_kernel(h_ref, a0_ref, a1_ref, a2_ref, ml_ref, g_ref, wg_ref, wua_ref, wub_ref, wo_ref, out_ref):
    h = h_ref[...]
    u = _rms_norm(h, g_ref[...]).astype(BF16)
    zg = _dot(u, wg_ref[...])
    mo = zg[:, :MLSTM_WIDTH]
    ga = zg[:, MLSTM_WIDTH:MLSTM_WIDTH + D_MODEL]
    gb = zg[:, MLSTM_WIDTH + D_MODEL:]

    a_refs = (a0_ref, a1_ref, a2_ref)
    lses = [a[ATT_SLOTS] for a in a_refs]
    merged_heads = []
    for j in range(ATT_SLOTS):
        ls = [l[:, j:j + 1] for l in lses]
        mx = jnp.maximum(jnp.maximum(ls[0], ls[1]), ls[2])
        ws = [jnp.exp(l - mx) for l in ls]
        inv = 1.0 / (ws[0] + ws[1] + ws[2])
        acc = (ws[0] * inv) * a_refs[0][j]
        acc = acc + (ws[1] * inv) * a_refs[1][j]
        acc = acc + (ws[2] * inv) * a_refs[2][j]
        merged_heads.append(acc.astype(BF16))
    att = jnp.concatenate(merged_heads, axis=1)
    y_att = _dot(att, wua_ref[...])
    y_ml = _dot((jax.nn.sigmoid(mo) * ml_ref[...]).astype(BF16), wub_ref[...])
    merged = jax.nn.sigmoid(ga) * y_att + jax.nn.sigmoid(gb) * y_ml
    out_ref[...] = h + _dot(merged.astype(BF16), wo_ref[...])


def _mix_out_call(h2d, att, ml, mix_norm, w_in, w_up_att, w_up_mlstm, w_out, tm):
    n_rows = h2d.shape[0]
    g2 = mix_norm.reshape(1, D_MODEL)
    o_mo = 3 * ATT_WIDTH + 3 * MLSTM_WIDTH
    o_g = o_mo + MLSTM_WIDTH + 2 * MLSTM_HEADS
    wg = jnp.concatenate([w_in[:, o_mo:o_mo + MLSTM_WIDTH], w_in[:, o_g:o_g + 2 * D_MODEL]], axis=1).astype(BF16)
    wua = w_up_att.astype(BF16)
    wub = w_up_mlstm.astype(BF16)
    wo = w_out.astype(BF16)
    row = lambda w: pl.BlockSpec((tm, w), lambda i: (i, 0))
    slabs = pl.BlockSpec((ATT_SLABS, tm, HEAD_DIM), lambda i: (0, i, 0))
    return pl.pallas_call(
        _mix_out_kernel,
        grid=(n_rows // tm,),
        in_specs=[row(D_MODEL), slabs, slabs, slabs, row(MLSTM_WIDTH),
                  _resident(g2.shape), _resident(wg.shape), _resident(wua.shape), _resident(wub.shape),
                  _resident(wo.shape)],
        out_specs=row(D_MODEL),
        out_shape=jax.ShapeDtypeStruct((n_rows, D_MODEL), F32),
        compiler_params=_params(("parallel",)),
        name="mix_out",
    )(h2d, *att, ml, g2, wg, wua, wub, wo)


def _layer(x2d, p2d, batch, seq, ffn1_norm, ffn1_w_in, ffn1_w_out, mix_norm, w_in, q_gain, k_gain, conv_w, conv_b,
           i_bias, f_bias, w_up_att, w_up_mlstm, w_out, ffn2_norm, ffn2_w_in, ffn2_w_out, ple_norm, w_ple_gate,
           w_ple_proj):
    tm = 512
    h = _ffn_call(x2d, ffn1_norm, ffn1_w_in, ffn1_w_out, tm)
    qkv, (mq, mk, mv, gr, gt) = _in_proj_call(h, mix_norm, w_in, q_gain, k_gain, conv_w, conv_b, i_bias, f_bias,
                                              batch, seq, tm)
    att = []
    for g, (window, dilation) in enumerate(ATT_GROUPS):
        assert window // dilation == ATT_BLOCK
        att.append(_attention_call(*qkv[g], g, dilation, ATT_BLOCKS_PER_STEP[g], batch, seq))
    ml = _mlstm_call(mq, mk, mv, gr, gt, batch, seq)
    h = _mix_out_call(h, att, ml, mix_norm, w_in, w_up_att, w_up_mlstm, w_out, tm)
    ple = (p2d, ple_norm, w_ple_gate.astype(BF16), w_ple_proj.astype(BF16))
    return _ffn_call(h, ffn2_norm, ffn2_w_in, ffn2_w_out, tm, ple=ple)


def kernel(x, p, ffn1_norm, ffn1_w_in, ffn1_w_out, mix_norm, w_in, q_gain, k_gain, conv_w, conv_b, i_bias, f_bias,
           w_up_att, w_up_mlstm, w_out, ffn2_norm, ffn2_w_in, ffn2_w_out, ple_norm, w_ple_gate, w_ple_proj):
    batch, seq, d = x.shape
    depth = p.shape[0]
    assert d == D_MODEL and seq % (ATT_GROUPS[-1][1] * ATT_BLOCK) == 0
    h = x.reshape(batch * seq, d)
    for i in range(depth):
        h = _layer(h, p[i].reshape(batch * seq, PLE_DIM), batch, seq, ffn1_norm[i], ffn1_w_in[i], ffn1_w_out[i],
                   mix_norm[i], w_in[i], q_gain[i], k_gain[i], conv_w[i], conv_b[i], i_bias[i], f_bias[i],
                   w_up_att[i], w_up_mlstm[i], w_out[i], ffn2_norm[i], ffn2_w_in[i], ffn2_w_out[i],
                   ple_norm[i], w_ple_gate[i], w_ple_proj[i])
    return h.reshape(batch, seq, d)
```

```python
import functools

import jax
import jax.numpy as jnp
from jax import lax
from jax.experimental import pallas as pl
from jax.experimental.pallas import tpu as pltpu

D_MODEL = 1024
PLE_DIM = 256
ATT_GROUPS = ((128, 1), (512, 4), (2048, 16))
N_ATT_GROUPS = 3
ATT_SLOTS = 4
ATT_HEADS = N_ATT_GROUPS * ATT_SLOTS
HEAD_DIM = 128
ATT_WIDTH = ATT_HEADS * HEAD_DIM
ATT_OUT_WIDTH = ATT_SLOTS * HEAD_DIM
ROPE_THETA = 500000.0
ROPE_DIM = HEAD_DIM // 4
ROPE_HALF = ROPE_DIM // 2
MLSTM_HEADS = 4
MLSTM_WIDTH = MLSTM_HEADS * HEAD_DIM
MLSTM_CHUNK = 128
CONV_WIDTH = 4
D_FF = 2816
NORM_EPS = 1e-6

FF_CHUNKS = ((0, 512), (512, 512), (1024, 512), (1536, 512), (2048, 512), (2560, 256))
ATT_BLOCK = 128
ATT_BLOCKS_PER_STEP = (8, 2, 1)
ATT_TILES_PER_ITER = 8
ATT_SLABS = ATT_SLOTS + 1
PERM_ROWS = 256
GATE_LANES = 128
CONV_PAD = 8
V7X_VMEM_LIMIT = 56 * 1024 * 1024

BF16 = jnp.bfloat16
F32 = jnp.float32


def _rms_norm(x, g):
    ms = jnp.mean(x * x, axis=-1, keepdims=True)
    return x * lax.rsqrt(ms + NORM_EPS) * g


def _silu(x):
    return x * jax.nn.sigmoid(x)


def _dot(a, b):
    return jnp.dot(a, b, preferred_element_type=F32)


def _dot_nt(a, b):
    return lax.dot_general(a, b, (((1,), (1,)), ((), ())), preferred_element_type=F32)


def _dot_tn(a, b):
    return lax.dot_general(a, b, (((0,), (0,)), ((), ())), preferred_element_type=F32)


def _resident(shape):
    nd = len(shape)
    return pl.BlockSpec(shape, lambda *_: (0,) * nd, pipeline_mode=pl.Buffered(1))


def _params(semantics, flags=None):
    return pltpu.CompilerParams(dimension_semantics=semantics, vmem_limit_bytes=V7X_VMEM_LIMIT, flags=flags)


def _swiglu_update(x_ref, g_ref, w_in_ref, w_out_ref, acc_ref):
    u = _rms_norm(x_ref[...], g_ref[...]).astype(BF16)
    for idx, (off, width) in enumerate(FF_CHUNKS):
        a = _dot(u, w_in_ref[:, off:off + width])
        b = _dot(u, w_in_ref[:, D_FF + off:D_FF + off + width])
        part = _dot((_silu(a) * b).astype(BF16), w_out_ref[off:off + width, :])
        if idx == 0:
            acc_ref[...] = part
        else:
            acc_ref[...] += part
    return x_ref[...] + 0.5 * acc_ref[...]


def _ffn_kernel(x_ref, g_ref, w_in_ref, w_out_ref, o_ref, acc_ref):
    o_ref[...] = _swiglu_update(x_ref, g_ref, w_in_ref, w_out_ref, acc_ref)


def _ffn_ple_kernel(x_ref, p_ref, g_ref, w_in_ref, w_out_ref, pg_ref, w_pg_ref, w_pe_ref, o_ref, acc_ref):
    h = _swiglu_update(x_ref, g_ref, w_in_ref, w_out_ref, acc_ref)
    u = _rms_norm(h, pg_ref[...]).astype(BF16)
    gate = jax.nn.sigmoid(_dot(u, w_pg_ref[...]))
    pe = _dot(p_ref[...].astype(BF16), w_pe_ref[...])
    o_ref[...] = h + pe * gate


def _ffn_call(x2d, norm_g, w_in, w_out, tm, ple=None):
    n_rows = x2d.shape[0]
    row = lambda w: pl.BlockSpec((tm, w), lambda i: (i, 0))
    g2 = norm_g.reshape(1, D_MODEL)
    w_in = w_in.astype(BF16)
    w_out = w_out.astype(BF16)
    if ple is None:
        kern = _ffn_kernel
        args = (x2d, g2, w_in, w_out)
        specs = [row(D_MODEL), _resident(g2.shape), _resident(w_in.shape), _resident(w_out.shape)]
        name = "ffn"
    else:
        p2d, ple_g, w_pg, w_pe = ple
        pg2 = ple_g.reshape(1, D_MODEL)
        kern = _ffn_ple_kernel
        args = (x2d, p2d, g2, w_in, w_out, pg2, w_pg, w_pe)
        specs = [row(D_MODEL), row(PLE_DIM), _resident(g2.shape), _resident(w_in.shape),
                 _resident(w_out.shape), _resident(pg2.shape), _resident(w_pg.shape), _resident(w_pe.shape)]
        name = "ffn_ple"
    return pl.pallas_call(
        kern,
        grid=(n_rows // tm,),
        in_specs=specs,
        out_specs=row(D_MODEL),
        out_shape=jax.ShapeDtypeStruct((n_rows, D_MODEL), F32),
        scratch_shapes=[pltpu.VMEM((tm, D_MODEL), F32)],
        compiler_params=_params(("parallel",)),
        name=name,
    )(*args)


def _log_sigmoid(x):
    return jnp.minimum(x, 0.0) - jnp.log1p(jnp.exp(-jnp.abs(x)))


def _in_proj_kernel(h_ref, hp_ref, g_ref, watt_ref, wm_ref, wgt_ref, qg_ref, kg_ref,
                    cos_ref, sin_ref, perm_ref, cw_ref, cb_ref, gbt_ref,
                    q0_ref, k0_ref, v0_ref, q1_ref, k1_ref, v1_ref, q2_ref, k2_ref, v2_ref,
                    mq_ref, mk_ref, mv_ref, gt_ref, xpad_ref, u_ref):
    tm = h_ref.shape[0]
    gnorm = g_ref[...]
    u = _rms_norm(h_ref[...], gnorm).astype(BF16)
    qgain = qg_ref[...]
    kgain = kg_ref[...]
    att_refs = ((q0_ref, k0_ref, v0_ref), (q1_ref, k1_ref, v1_ref), (q2_ref, k2_ref, v2_ref))

    def qk_head(zh, gain, cos, sin, scale):
        y = _rms_norm(zh, gain)
        y = y * cos + pltpu.roll(y, HEAD_DIM // 2, 1) * sin
        if scale != 1.0:
            y = y * scale
        return y.astype(BF16)

    def deinterleaved(grp, dil):
        if dil == 1:
            return u
        pm = perm_ref[grp - 1]
        return jnp.concatenate(
            [_dot(pm, u[kb * PERM_ROWS:(kb + 1) * PERM_ROWS, :]) for kb in range(tm // PERM_ROWS)],
            axis=0).astype(BF16)

    def store_att(o_ref, val, dil):
        if dil == 1:
            o_ref[0, 0] = val
            return
        rows_per_res = PERM_ROWS // dil
        for kb in range(tm // PERM_ROWS):
            for r in range(dil):
                src = kb * PERM_ROWS + r * rows_per_res
                o_ref[0, r, kb * rows_per_res:(kb + 1) * rows_per_res, :] = val[src:src + rows_per_res, :]

    def att_item(grp, dil, part):
        cols = slice(part * ATT_OUT_WIDTH, (part + 1) * ATT_OUT_WIDTH)
        matmul = lambda: _dot(u_ref[grp], watt_ref[grp, :, cols])

        def epilogue(z):
            if part == 2:
                val = z.astype(BF16)
            else:
                gain, scale = ((qgain, HEAD_DIM ** -0.5), (kgain, 1.0))[part]
                cos = cos_ref[grp]
                sin = sin_ref[grp]
                val = jnp.concatenate([qk_head(z[:, j * HEAD_DIM:(j + 1) * HEAD_DIM], gain, cos, sin, scale)
                                       for j in range(ATT_SLOTS)], axis=1)
            store_att(att_refs[grp][part], val, dil)
        return matmul, epilogue

    def conv_item(part, o_ref, scale):
        cols = slice(part * MLSTM_WIDTH, (part + 1) * MLSTM_WIDTH)

        def matmul():
            up = _rms_norm(hp_ref[...], gnorm).astype(BF16)
            return _dot(u_ref[0], wm_ref[:, cols]), _dot(up, wm_ref[:, cols])

        def epilogue(zz):
            z, zp = zz
            xpad_ref[0:CONV_PAD, cols] = jnp.where(pl.program_id(0) > 0, zp, 0.0)
            xpad_ref[CONV_PAD:CONV_PAD + tm, cols] = z
            acc = cb_ref[:, cols]
            for j in range(CONV_WIDTH):
                off = CONV_PAD - (CONV_WIDTH - 1) + j
                acc = acc + xpad_ref[off:off + tm, cols] * cw_ref[j:j + 1, cols]
            y = _silu(acc)
            if scale != 1.0:
                y = y * scale
            o_ref[...] = y.astype(BF16)
        return matmul, epilogue

    def value_gate_item():
        def matmul():
            return _dot(u_ref[0], wm_ref[:, 2 * MLSTM_WIDTH:]), _dot_nt(wgt_ref[...], u_ref[0])

        def epilogue(zz):
            zv, zt = zz
            mv_ref[...] = zv.astype(BF16)
            zt = zt + gbt_ref[...]
            sub = lax.broadcasted_iota(jnp.int32, zt.shape, 0)
            gt_ref[0] = jnp.where(sub < MLSTM_HEADS, zt, _log_sigmoid(zt))
        return matmul, epilogue

    for grp, (_, dil) in enumerate(ATT_GROUPS):
        u_ref[grp] = deinterleaved(grp, dil)
    items =[att_item(grp, dil, part) for grp, (_, dil) in enumerate(ATT_GROUPS) for part in range(3)]
    items += [conv_item(0, mq_ref, 1.0), conv_item(1, mk_ref, HEAD_DIM ** -0.5), value_gate_item()]
    pending = items[0][0]()
    for idx, (_, epilogue) in enumerate(items):
        current = pending
        if idx + 1 < len(items):
            pending = items[idx + 1][0]()
        epilogue(current)


def _rope_lane_perm():
    return (list(range(0, ROPE_HALF)) + list(range(ROPE_DIM, ROPE_DIM + 48))
            + list(range(ROPE_HALF, ROPE_DIM)) + list(range(ROPE_DIM + 48, HEAD_DIM)))


def _deinterleave_rows(t, dil):
    n, w = t.shape
    return t.reshape(n // PERM_ROWS, PERM_ROWS // dil, dil, w).transpose(0, 2, 1, 3).reshape(n, w)


def _rope_tables(seq):
    pos = jnp.arange(seq, dtype=F32).reshape(seq, 1)
    pos_g = jnp.stack([_deinterleave_rows(pos, d) for _, d in ATT_GROUPS])
    inv_freq = 1.0 / (ROPE_THETA ** (jnp.arange(ROPE_HALF, dtype=F32) / ROPE_HALF))
    rest = jnp.zeros((HEAD_DIM // 2 - ROPE_HALF,), F32)
    freq = jnp.concatenate([inv_freq, rest, inv_freq, rest])
    sign = jnp.concatenate([-jnp.ones((ROPE_HALF,), F32), rest, jnp.ones((ROPE_HALF,), F32), rest])
    ang = pos_g * freq
    return jnp.cos(ang), jnp.sin(ang) * sign


def _in_proj_call(h2d, mix_norm, w_in, q_gain, k_gain, conv_w, conv_b, i_bias, f_bias, batch, seq, tm):
    n_rows = h2d.shape[0]
    perm = jnp.asarray(_rope_lane_perm(), dtype=jnp.int32)
    slot_perm = (jnp.arange(ATT_SLOTS, dtype=jnp.int32)[:, None] * HEAD_DIM + perm[None, :]).reshape(-1)
    wq, wk, wv = (w_in[:, i * ATT_WIDTH:(i + 1) * ATT_WIDTH] for i in range(3))
    w_att = []
    for g in range(N_ATT_GROUPS):
        cols = slice(g * ATT_OUT_WIDTH, (g + 1) * ATT_OUT_WIDTH)
        w_att.append(jnp.concatenate([wq[:, cols][:, slot_perm], wk[:, cols][:, slot_perm], wv[:, cols]], axis=1))
    w_att = jnp.stack(w_att).astype(BF16)
    o = 3 * ATT_WIDTH
    wm = w_in[:, o:o + 3 * MLSTM_WIDTH].astype(BF16)
    o += 4 * MLSTM_WIDTH
    wgt = w_in[:, o:o + 2 * MLSTM_HEADS].T.astype(BF16)
    gbt = jnp.concatenate([i_bias, f_bias]).astype(F32).reshape(2 * MLSTM_HEADS, 1)
    qg = q_gain[perm].reshape(1, HEAD_DIM)
    kg = k_gain[perm].reshape(1, HEAD_DIM)
    cos_g, sin_g = _rope_tables(seq)
    eye = jnp.eye(PERM_ROWS, dtype=F32)
    perms = jnp.stack([_deinterleave_rows(eye, d) for _, d in ATT_GROUPS[1:]]).astype(BF16)
    g2 = mix_norm.reshape(1, D_MODEL)
    cb2 = conv_b.reshape(1, 2 * MLSTM_WIDTH)
    tiles_per_seq = seq // tm

    row = lambda w: pl.BlockSpec((tm, w), lambda t, b: (b * tiles_per_seq + t, 0))
    prev_rows = pl.BlockSpec(
        (CONV_PAD, D_MODEL), lambda t, b: (jnp.maximum((b * tiles_per_seq + t) * (tm // CONV_PAD) - 1, 0), 0))
    pos = pl.BlockSpec((N_ATT_GROUPS, tm, HEAD_DIM), lambda t, b: (0, t, 0))
    in_specs = [row(D_MODEL), prev_rows, _resident(g2.shape), _resident(w_att.shape), _resident(wm.shape),
                _resident(wgt.shape), _resident(qg.shape), _resident(kg.shape), pos, pos,
                _resident(perms.shape), _resident(conv_w.shape), _resident(cb2.shape), _resident(gbt.shape)]
    out_specs, out_shape = [], []
    for _, d in ATT_GROUPS:
        spec = pl.BlockSpec((1, d, tm // d, ATT_OUT_WIDTH), lambda t, b: (b, 0, t, 0))
        out_specs += [spec] * 3
        out_shape += [jax.ShapeDtypeStruct((batch, d, seq // d, ATT_OUT_WIDTH), BF16)] * 3
    out_specs += [row(MLSTM_WIDTH)] * 3 + [pl.BlockSpec((1, 2 * MLSTM_HEADS, tm), lambda t, b: (b, 0, t))]
    out_shape += [jax.ShapeDtypeStruct((n_rows, MLSTM_WIDTH), BF16)] * 3 + [
        jax.ShapeDtypeStruct((batch, 2 * MLSTM_HEADS, seq), F32)]
    outs = pl.pallas_call(
        _in_proj_kernel,
        grid=(tiles_per_seq, batch),
        in_specs=in_specs,
        out_specs=out_specs,
        out_shape=out_shape,
        scratch_shapes=[pltpu.VMEM((tm + CONV_PAD, 2 * MLSTM_WIDTH), F32),
                        pltpu.VMEM((N_ATT_GROUPS, tm, D_MODEL), BF16)],
        compiler_params=_params(("parallel", "parallel")),
        name="in_proj",
    )(h2d, h2d, g2, w_att, wm, wgt, qg, kg, cos_g, sin_g, perms, conv_w, cb2, gbt)
    qkv = [tuple(outs[3 * g:3 * g + 3]) for g in range(N_ATT_GROUPS)]
    return qkv, outs[9:]


def _attention_kernel(dil, nblk, q_ref, kc_ref, vc_ref, kp_ref, vp_ref, o_ref):
    qi = lax.broadcasted_iota(jnp.int32, (ATT_BLOCK, 2 * ATT_BLOCK), 0)
    kj = lax.broadcasted_iota(jnp.int32, (ATT_BLOCK, 2 * ATT_BLOCK), 1)
    band = (kj >= qi) & (kj <= qi + ATT_BLOCK)
    bias = jnp.where(band, 0.0, -jnp.inf)
    first_key = jnp.where(pl.program_id(1) > 0, 0, ATT_BLOCK)
    bias_head = jnp.where(kj >= first_key, bias, -jnp.inf)
    lane = lax.broadcasted_iota(jnp.int32, (ATT_BLOCK, HEAD_DIM), 1)

    heads = [slice(j * HEAD_DIM, (j + 1) * HEAD_DIM) for j in range(ATT_SLOTS)]

    def scores(r, blk):
        q = q_ref[0, r, blk * ATT_BLOCK:(blk + 1) * ATT_BLOCK, :]
        if blk == 0:
            k = jnp.concatenate([kp_ref[0, r], kc_ref[0, r, 0:ATT_BLOCK, :]], axis=0)
            tile_bias = bias_head
        else:
            k = kc_ref[0, r, (blk - 1) * ATT_BLOCK:(blk + 1) * ATT_BLOCK, :]
            tile_bias = bias
        return [_dot_nt(q[:, sl], k[:, sl]) + tile_bias for sl in heads]

    def softmax(s_list):
        out = []
        for s in s_list:
            m = jnp.max(s, axis=-1, keepdims=True)
            e = jnp.exp(s - m)
            den = jnp.sum(e, axis=-1, keepdims=True)
            out.append((e.astype(BF16), den, m + jnp.log(den)))
        return out

    def weighted_values(r, blk, p_list):
        if blk == 0:
            v = jnp.concatenate([vp_ref[0, r], vc_ref[0, r, 0:ATT_BLOCK, :]], axis=0)
        else:
            v = vc_ref[0, r, (blk - 1) * ATT_BLOCK:(blk + 1) * ATT_BLOCK, :]
        return [_dot(p, v[:, sl]) for (p, _, _), sl in zip(p_list, heads)]

    def store(r, blk, p_list, pv_list):
        if dil == 1:
            rows = pl.ds(blk * ATT_BLOCK, ATT_BLOCK)
        else:
            rows = pl.ds(blk * ATT_BLOCK * dil + r, ATT_BLOCK, stride=dil)
        lse_slab = jnp.zeros((ATT_BLOCK, HEAD_DIM), F32)
        for j, ((_, den, lse), pv) in enumerate(zip(p_list, pv_list)):
            o_ref[j, rows, :] = pv / den
            lse_slab = jnp.where(lane == j, lse, lse_slab)
        o_ref[ATT_SLOTS, rows, :] = lse_slab

    def run_tiles(tiles):
        s_next = scores(*tiles[0])
        for idx, (r, blk) in enumerate(tiles):
            s_cur = s_next
            if idx + 1 < len(tiles):
                s_next = scores(*tiles[idx + 1])
            p_list = softmax(s_cur)
            store(r, blk, p_list, weighted_values(r, blk, p_list))

    res_per_iter = ATT_TILES_PER_ITER // nblk
    if dil == res_per_iter:
        run_tiles([(r, blk) for r in range(dil) for blk in range(nblk)])
    else:
        def body(it, carry):
            run_tiles([(it * res_per_iter + rr, blk) for rr in range(res_per_iter) for blk in range(nblk)])
            return carry
        lax.fori_loop(0, dil // res_per_iter, body, 0)


def _attention_call(q, k, v, group, dilation, nblk, batch, seq):
    n = seq // dilation
    steps = n // (nblk * ATT_BLOCK)
    span = nblk * ATT_BLOCK * dilation
    cur = pl.BlockSpec((1, dilation, nblk * ATT_BLOCK, ATT_OUT_WIDTH), lambda b, i: (b, 0, i, 0))
    prev = pl.BlockSpec((1, dilation, ATT_BLOCK, ATT_OUT_WIDTH),
                        lambda b, i: (b, 0, jnp.maximum(i * nblk - 1, 0), 0))
    return pl.pallas_call(
        functools.partial(_attention_kernel, dilation, nblk),
        grid=(batch, steps),
        in_specs=[cur, cur, cur, prev, prev],
        out_specs=pl.BlockSpec((ATT_SLABS, span, HEAD_DIM), lambda b, i: (0, b * steps + i, 0)),
        out_shape=jax.ShapeDtypeStruct((ATT_SLABS, batch * seq, HEAD_DIM), F32),
        compiler_params=_params(("parallel", "arbitrary")),
        name=f"attention_g{group}",
    )(q, k, v, k, v)


def _mlstm_kernel(mq_ref, mk_ref, mv_ref, gt_ref, o_ref, state_ref, m_ref):
    L = MLSTM_CHUNK
    H = MLSTM_HEADS

    @pl.when(pl.program_id(0) == 0)
    def _():
        state_ref[...] = jnp.zeros_like(state_ref)
        m_ref[...] = jnp.zeros_like(m_ref)

    ti = lax.broadcasted_iota(jnp.int32, (L, L), 0)
    si = lax.broadcasted_iota(jnp.int32, (L, L), 1)
    causal = ti >= si
    tri = causal.astype(F32)
    ones = jnp.ones((L, HEAD_DIM), BF16)

    nbatch = mq_ref.shape[0]
    streams = [divmod(bh, H) for bh in range(nbatch * H)]
    lanes = lambda hd: slice(hd * HEAD_DIM, (hd + 1) * HEAD_DIM)

    gts = [gt_ref[bi] for bi in range(nbatch)]
    b_rows = [lax.dot_general(gt, tri, (((1,), (1,)), ((), ())), precision=lax.Precision.HIGHEST,
                              preferred_element_type=F32) for gt in gts]

    qs = [mq_ref[bi, :, lanes(hd)] for bi, hd in streams]
    ks = [mk_ref[bi, :, lanes(hd)] for bi, hd in streams]
    vexts = [jnp.concatenate([mv_ref[bi, :, lanes(hd)], ones], axis=1) for bi, hd in streams]
    states = [state_ref[bh] for bh in range(len(streams))]
    m_prevs = [m_ref[bh, 0:1, 0:1] for bh in range(len(streams))]
    qk = [_dot_nt(q, k) for q, k in zip(qs, ks)]
    q_state = [_dot(q, st.astype(BF16)) for q, st in zip(qs, states)]

    c_rows, upd, decays, m_news = [], [], [], []
    for bh, (bi, hd) in enumerate(streams):
        b_row = b_rows[bi][H + hd:H + hd + 1, :]
        c_row = gts[bi][hd:hd + 1, :] - b_row
        b_last = b_row[:, L - 1:L]
        g_row = b_last + c_row
        m_new = jnp.maximum(b_last + m_prevs[bh], jnp.max(g_row, axis=-1, keepdims=True))
        wk_row = jnp.exp(g_row - m_new)
        c_rows.append(c_row)
        decays.append(jnp.exp(b_last + m_prevs[bh] - m_new))
        m_news.append(m_new)
        k_t = ks[bh].astype(F32).T
        upd.append(_dot((k_t * wk_row).astype(BF16), vexts[bh]))

    s_list, a_list, floor_list = [], [], []
    for bh, (bi, hd) in enumerate(streams):
        cm = jnp.where(causal, c_rows[bh], -jnp.inf)
        n_t = jnp.maximum(m_prevs[bh], jnp.max(cm, axis=-1, keepdims=True))
        b_t = jnp.sum(jnp.where(causal, gts[bi][H + hd:H + hd + 1, :], 0.0), axis=-1, keepdims=True)
        s_list.append((qk[bh] * jnp.exp(cm - n_t)).astype(BF16))
        a_list.append(jnp.exp(m_prevs[bh] - n_t))
        floor_list.append(jnp.exp(-(b_t + n_t)))

    sv = [_dot(s, vext) for s, vext in zip(s_list, vexts)]
    for bh, (bi, hd) in enumerate(streams):
        ext = a_list[bh] * q_state[bh] + sv[bh]
        num = ext[:, :HEAD_DIM]
        den = ext[:, HEAD_DIM:]
        o_ref[bi, :, lanes(hd)] = num / jnp.maximum(jnp.abs(den), floor_list[bh])
        state_ref[bh] = decays[bh] * states[bh] + upd[bh]
        m_ref[bh] = jnp.broadcast_to(m_news[bh], m_ref.shape[1:])


def _mlstm_call(mq, mk, mv, gt, batch, seq):
    nc = seq // MLSTM_CHUNK
    W = MLSTM_WIDTH
    blk = lambda w: pl.BlockSpec((batch, MLSTM_CHUNK, w), lambda c: (0, c, 0))
    out = pl.pallas_call(
        _mlstm_kernel,
        grid=(nc,),
        in_specs=[blk(W), blk(W), blk(W),
                  pl.BlockSpec((batch, 2 * MLSTM_HEADS, MLSTM_CHUNK), lambda c: (0, 0, c))],
        out_specs=blk(W),
        out_shape=jax.ShapeDtypeStruct((batch, seq, W), F32),
        scratch_shapes=[pltpu.VMEM((batch * MLSTM_HEADS, HEAD_DIM, 2 * HEAD_DIM), F32),
                        pltpu.VMEM((batch * MLSTM_HEADS, 8, 128), F32)],
        compiler_params=_params(("arbitrary",)),
        name="mlstm",
    )(mq.reshape(batch, seq, W), mk.reshape(batch, seq, W), mv.reshape(batch, seq, W), gt)
    return out.reshape(batch * seq, W)


def _mix_out_kernel(h_ref, a0_ref, a1_ref, a2_ref, ml_ref, g_ref, wg_ref, wua_ref, wub_ref, wo_ref, out_ref):
    h = h_ref[...]
    u = _rms_norm(h, g_ref[...]).astype(BF16)
    zg = _dot(u, wg_ref[...])
    mo = zg[:, :MLSTM_WIDTH]
    ga = zg[:, MLSTM_WIDTH:MLSTM_WIDTH + D_MODEL]
    gb = zg[:, MLSTM_WIDTH + D_MODEL:]

    a_refs = (a0_ref, a1_ref, a2_ref)
    lses = [a[ATT_SLOTS] for a in a_refs]
    merged_heads = []
    for j in range(ATT_SLOTS):
        ls = [l[:, j:j + 1] for l in lses]
        mx = jnp.maximum(jnp.maximum(ls[0], ls[1]), ls[2])
        ws = [jnp.exp(l - mx) for l in ls]
        inv = 1.0 / (ws[0] + ws[1] + ws[2])
        acc = (ws[0] * inv) * a_refs[0][j]
        acc = acc + (ws[1] * inv) * a_refs[1][j]
        acc = acc + (ws[2] * inv) * a_refs[2][j]
        merged_heads.append(acc.astype(BF16))
    att = jnp.concatenate(merged_heads, axis=1)
    y_att = _dot(att, wua_ref[...])
    y_ml = _dot((jax.nn.sigmoid(mo) * ml_ref[...]).astype(BF16), wub_ref[...])
    merged = jax.nn.sigmoid(ga) * y_att + jax.nn.sigmoid(gb) * y_ml
    out_ref[...] = h + _dot(merged.astype(BF16), wo_ref[...])


def _mix_out_call(h2d, att, ml, mix_norm, w_in, w_up_att, w_up_mlstm, w_out, tm):
    n_rows = h2d.shape[0]
    g2 = mix_norm.reshape(1, D_MODEL)
    o_mo = 3 * ATT_WIDTH + 3 * MLSTM_WIDTH
    o_g = o_mo + MLSTM_WIDTH + 2 * MLSTM_HEADS
    wg = jnp.concatenate([w_in[:, o_mo:o_mo + MLSTM_WIDTH], w_in[:, o_g:o_g + 2 * D_MODEL]], axis=1).astype(BF16)
    wua = w_up_att.astype(BF16)
    wub = w_up_mlstm.astype(BF16)
    wo = w_out.astype(BF16)
    row = lambda w: pl.BlockSpec((tm, w), lambda i: (i, 0))
    slabs = pl.BlockSpec((ATT_SLABS, tm, HEAD_DIM), lambda i: (0, i, 0))
    return pl.pallas_call(
        _mix_out_kernel,
        grid=(n_rows // tm,),
        in_specs=[row(D_MODEL), slabs, slabs, slabs, row(MLSTM_WIDTH),
                  _resident(g2.shape), _resident(wg.shape), _resident(wua.shape), _resident(wub.shape),
                  _resident(wo.shape)],
        out_specs=row(D_MODEL),
        out_shape=jax.ShapeDtypeStruct((n_rows, D_MODEL), F32),
        compiler_params=_params(("parallel",)),
        name="mix_out",
    )(h2d, *att, ml, g2, wg, wua, wub, wo)


def _layer(x2d, p2d, batch, seq, ffn1_norm, ffn1_w_in, ffn1_w_out, mix_norm, w_in, q_gain, k_gain, conv_w, conv_b,
           i_bias, f_bias, w_up_att, w_up_mlstm, w_out, ffn2_norm, ffn2_w_in, ffn2_w_out, ple_norm, w_ple_gate,
           w_ple_proj):
    tm = 512
    h = _ffn_call(x2d, ffn1_norm, ffn1_w_in, ffn1_w_out, tm)
    qkv, (mq, mk, mv, gt) = _in_proj_call(h, mix_norm, w_in, q_gain, k_gain, conv_w, conv_b, i_bias, f_bias,
                                              batch, seq, 256)
    att = []
    for g, (window, dilation) in enumerate(ATT_GROUPS):
        assert window // dilation == ATT_BLOCK
        att.append(_attention_call(*qkv[g], g, dilation, ATT_BLOCKS_PER_STEP[g], batch, seq))
    ml = _mlstm_call(mq, mk, mv, gt, batch, seq)
    h = _mix_out_call(h, att, ml, mix_norm, w_in, w_up_att, w_up_mlstm, w_out, tm)
    ple = (p2d, ple_norm, w_ple_gate.astype(BF16), w_ple_proj.astype(BF16))
    return _ffn_call(h, ffn2_norm, ffn2_w_in, ffn2_w_out, tm, ple=ple)


def kernel(x, p, ffn1_norm, ffn1_w_in, ffn1_w_out, mix_norm, w_in, q_gain, k_gain, conv_w, conv_b, i_bias, f_bias,
           w_up_att, w_up_mlstm, w_out, ffn2_norm, ffn2_w_in, ffn2_w_out, ple_norm, w_ple_gate, w_ple_proj):
    batch, seq, d = x.shape
    depth = p.shape[0]
    assert d == D_MODEL and seq % (ATT_GROUPS[-1][1] * ATT_BLOCK) == 0
    h = x.reshape(batch * seq, d)
    for i in range(depth):
        h = _layer(h, p[i].reshape(batch * seq, PLE_DIM), batch, seq, ffn1_norm[i], ffn1_w_in[i], ffn1_w_out[i],
                   mix_norm[i], w_in[i], q_gain[i], k_gain[i], conv_w[i], conv_b[i], i_bias[i], f_bias[i],
                   w_up_att[i], w_up_mlstm[i], w_out[i], ffn2_norm[i], ffn2_w_in[i], ffn2_w_out[i],
                   ple_norm[i], w_ple_gate[i], w_ple_proj[i])
    return h.reshape(batch, seq, d)
```

```python
import functools

import jax
import jax.numpy as jnp
from jax import lax
from jax.experimental import pallas as pl
from jax.experimental.pallas import tpu as pltpu

D_MODEL = 1024
PLE_DIM = 256
ATT_GROUPS = ((128, 1), (512, 4), (2048, 16))
N_ATT_GROUPS = 3
ATT_SLOTS = 4
ATT_HEADS = N_ATT_GROUPS * ATT_SLOTS
HEAD_DIM = 128
ATT_WIDTH = ATT_HEADS * HEAD_DIM
ATT_OUT_WIDTH = ATT_SLOTS * HEAD_DIM
ROPE_THETA = 500000.0
ROPE_DIM = HEAD_DIM // 4
ROPE_HALF = ROPE_DIM // 2
MLSTM_HEADS = 4
MLSTM_WIDTH = MLSTM_HEADS * HEAD_DIM
MLSTM_CHUNK = 128
CONV_WIDTH = 4
D_FF = 2816
NORM_EPS = 1e-6

FF_CHUNKS = ((0, 512), (512, 512), (1024, 512), (1536, 512), (2048, 512), (2560, 256))
ATT_BLOCK = 128
ATT_BLOCKS_PER_STEP = (8, 2, 1)
ATT_TILES_PER_ITER = 8
ATT_SLABS = ATT_SLOTS + 1
PERM_ROWS = 256
CONV_PAD = 8
V7X_VMEM_LIMIT = 56 * 1024 * 1024

BF16 = jnp.bfloat16
F32 = jnp.float32


def _rms_norm(x, g):
    ms = jnp.mean(x * x, axis=-1, keepdims=True)
    return x * lax.rsqrt(ms + NORM_EPS) * g


def _silu(x):
    return x * jax.nn.sigmoid(x)


def _dot(a, b):
    return jnp.dot(a, b, preferred_element_type=F32)


def _dot_nt(a, b):
    return lax.dot_general(a, b, (((1,), (1,)), ((), ())), preferred_element_type=F32)


def _dot_tn(a, b):
    return lax.dot_general(a, b, (((0,), (0,)), ((), ())), preferred_element_type=F32)


def _resident(shape):
    nd = len(shape)
    return pl.BlockSpec(shape, lambda *_: (0,) * nd, pipeline_mode=pl.Buffered(1))


def _params(semantics, flags=None):
    return pltpu.CompilerParams(dimension_semantics=semantics, vmem_limit_bytes=V7X_VMEM_LIMIT, flags=flags)


def _swiglu_update(x_ref, g_ref, w_in_ref, w_out_ref, acc_ref):
    u = _rms_norm(x_ref[...], g_ref[...]).astype(BF16)
    for idx, (off, width) in enumerate(FF_CHUNKS):
        a = _dot(u, w_in_ref[:, off:off + width])
        b = _dot(u, w_in_ref[:, D_FF + off:D_FF + off + width])
        part = _dot((_silu(a) * b).astype(BF16), w_out_ref[off:off + width, :])
        if idx == 0:
            acc_ref[...] = part
        else:
            acc_ref[...] += part
    return x_ref[...] + 0.5 * acc_ref[...]


def _ffn_kernel(x_ref, g_ref, w_in_ref, w_out_ref, o_ref, acc_ref):
    o_ref[...] = _swiglu_update(x_ref, g_ref, w_in_ref, w_out_ref, acc_ref)


def _ffn_ple_kernel(x_ref, p_ref, g_ref, w_in_ref, w_out_ref, pg_ref, w_pg_ref, w_pe_ref, o_ref, acc_ref):
    h = _swiglu_update(x_ref, g_ref, w_in_ref, w_out_ref, acc_ref)
    u = _rms_norm(h, pg_ref[...]).astype(BF16)
    gate = jax.nn.sigmoid(_dot(u, w_pg_ref[...]))
    pe = _dot(p_ref[...].astype(BF16), w_pe_ref[...])
    o_ref[...] = h + pe * gate


def _ffn_call(x2d, norm_g, w_in, w_out, tm, ple=None):
    n_rows = x2d.shape[0]
    row = lambda w: pl.BlockSpec((tm, w), lambda i: (i, 0))
    g2 = norm_g.reshape(1, D_MODEL)
    w_in = w_in.astype(BF16)
    w_out = w_out.astype(BF16)
    if ple is None:
        kern = _ffn_kernel
        args = (x2d, g2, w_in, w_out)
        specs = [row(D_MODEL), _resident(g2.shape), _resident(w_in.shape), _resident(w_out.shape)]
        name = "ffn"
    else:
        p2d, ple_g, w_pg, w_pe = ple
        pg2 = ple_g.reshape(1, D_MODEL)
        kern = _ffn_ple_kernel
        args = (x2d, p2d, g2, w_in, w_out, pg2, w_pg, w_pe)
        specs = [row(D_MODEL), row(PLE_DIM), _resident(g2.shape), _resident(w_in.shape),
                 _resident(w_out.shape), _resident(pg2.shape), _resident(w_pg.shape), _resident(w_pe.shape)]
        name = "ffn_ple"
    return pl.pallas_call(
        kern,
        grid=(n_rows // tm,),
        in_specs=specs,
        out_specs=row(D_MODEL),
        out_shape=jax.ShapeDtypeStruct((n_rows, D_MODEL), F32),
        scratch_shapes=[pltpu.VMEM((tm, D_MODEL), F32)],
        compiler_params=_params(("parallel",)),
        name=name,
    )(*args)


def _log_sigmoid(x):
    return jnp.minimum(x, 0.0) - jnp.log1p(jnp.exp(-jnp.abs(x)))


def _in_proj_kernel(h_ref, hp_ref, g_ref, watt_ref, wm_ref, wgt_ref, qg_ref, kg_ref,
                    cos_ref, sin_ref, perm_ref, cw_ref, cb_ref, gbt_ref,
                    q0_ref, k0_ref, v0_ref, q1_ref, k1_ref, v1_ref, q2_ref, k2_ref, v2_ref,
                    mq_ref, mk_ref, mv_ref, gt_ref, xpad_ref, u_ref):
    tm = h_ref.shape[0]
    gnorm = g_ref[...]
    u = _rms_norm(h_ref[...], gnorm).astype(BF16)
    qgain = qg_ref[...]
    kgain = kg_ref[...]
    att_refs = ((q0_ref, k0_ref, v0_ref), (q1_ref, k1_ref, v1_ref), (q2_ref, k2_ref, v2_ref))

    def qk_head(zh, gain, cos, sin, scale):
        y = _rms_norm(zh, gain)
        y = y * cos + pltpu.roll(y, HEAD_DIM // 2, 1) * sin
        if scale != 1.0:
            y = y * scale
        return y.astype(BF16)

    def deinterleaved(grp, dil):
        if dil == 1:
            return u
        pm = perm_ref[grp - 1]
        return jnp.concatenate(
            [_dot(pm, u[kb * PERM_ROWS:(kb + 1) * PERM_ROWS, :]) for kb in range(tm // PERM_ROWS)],
            axis=0).astype(BF16)

    def store_att(o_ref, val, dil):
        if dil == 1:
            o_ref[0, 0] = val
            return
        rows_per_res = PERM_ROWS // dil
        for kb in range(tm // PERM_ROWS):
            for r in range(dil):
                src = kb * PERM_ROWS + r * rows_per_res
                o_ref[0, r, kb * rows_per_res:(kb + 1) * rows_per_res, :] = val[src:src + rows_per_res, :]

    def att_item(grp, dil, part):
        cols = slice(part * ATT_OUT_WIDTH, (part + 1) * ATT_OUT_WIDTH)
        matmul = lambda: _dot(u_ref[grp], watt_ref[grp, :, cols])

        def epilogue(z):
            if part == 2:
                val = z.astype(BF16)
            else:
                gain, scale = ((qgain, HEAD_DIM ** -0.5), (kgain, 1.0))[part]
                cos = cos_ref[grp]
                sin = sin_ref[grp]
                val = jnp.concatenate([qk_head(z[:, j * HEAD_DIM:(j + 1) * HEAD_DIM], gain, cos, sin, scale)
                                       for j in range(ATT_SLOTS)], axis=1)
            store_att(att_refs[grp][part], val, dil)
        return matmul, epilogue

    def conv_item(part, o_ref, scale):
        cols = slice(part * MLSTM_WIDTH, (part + 1) * MLSTM_WIDTH)

        def matmul():
            up = _rms_norm(hp_ref[...], gnorm).astype(BF16)
            return _dot(u_ref[0], wm_ref[:, cols]), _dot(up, wm_ref[:, cols])

        def epilogue(zz):
            z, zp = zz
            zp = jnp.where(pl.program_id(0) > 0, zp, 0.0)
            for c in range(MLSTM_WIDTH // HEAD_DIM):
                slab = part * (MLSTM_WIDTH // HEAD_DIM) + c
                lanes = slice(part * MLSTM_WIDTH + c * HEAD_DIM, part * MLSTM_WIDTH + (c + 1) * HEAD_DIM)
                xpad_ref[slab, 0:CONV_PAD, :] = zp[:, c * HEAD_DIM:(c + 1) * HEAD_DIM]
                xpad_ref[slab, CONV_PAD:CONV_PAD + tm, :] = z[:, c * HEAD_DIM:(c + 1) * HEAD_DIM]
                acc = cb_ref[:, lanes]
                for j in range(CONV_WIDTH):
                    off = CONV_PAD - (CONV_WIDTH - 1) + j
                    acc = acc + xpad_ref[slab, off:off + tm, :] * cw_ref[j:j + 1, lanes]
                y = _silu(acc)
                if scale != 1.0:
                    y = y * scale
                o_ref[:, c * HEAD_DIM:(c + 1) * HEAD_DIM] = y.astype(BF16)
        return matmul, epilogue

    def value_gate_item():
        def matmul():
            return _dot(u_ref[0], wm_ref[:, 2 * MLSTM_WIDTH:]), _dot_nt(wgt_ref[...], u_ref[0])

        def epilogue(zz):
            zv, zt = zz
            mv_ref[...] = zv.astype(BF16)
            zt = zt + gbt_ref[...]
            sub = lax.broadcasted_iota(jnp.int32, zt.shape, 0)
            gt_ref[0] = jnp.where(sub < MLSTM_HEADS, zt, _log_sigmoid(zt))
        return matmul, epilogue

    for grp, (_, dil) in enumerate(ATT_GROUPS):
        u_ref[grp] = deinterleaved(grp, dil)
    items =[att_item(grp, dil, part) for grp, (_, dil) in enumerate(ATT_GROUPS) for part in range(3)]
    items += [conv_item(0, mq_ref, 1.0), conv_item(1, mk_ref, HEAD_DIM ** -0.5), value_gate_item()]
    pending = items[0][0]()
    for idx, (_, epilogue) in enumerate(items):
        current = pending
        if idx + 1 < len(items):
            pending = items[idx + 1][0]()
        epilogue(current)


def _rope_lane_perm():
    return (list(range(0, ROPE_HALF)) + list(range(ROPE_DIM, ROPE_DIM + 48))
            + list(range(ROPE_HALF, ROPE_DIM)) + list(range(ROPE_DIM + 48, HEAD_DIM)))


def _deinterleave_rows(t, dil):
    n, w = t.shape
    return t.reshape(n // PERM_ROWS, PERM_ROWS // dil, dil, w).transpose(0, 2, 1, 3).reshape(n, w)


def _rope_tables(seq):
    pos = jnp.arange(seq, dtype=F32).reshape(seq, 1)
    pos_g = jnp.stack([_deinterleave_rows(pos, d) for _, d in ATT_GROUPS])
    inv_freq = 1.0 / (ROPE_THETA ** (jnp.arange(ROPE_HALF, dtype=F32) / ROPE_HALF))
    rest = jnp.zeros((HEAD_DIM // 2 - ROPE_HALF,), F32)
    freq = jnp.concatenate([inv_freq, rest, inv_freq, rest])
    sign = jnp.concatenate([-jnp.ones((ROPE_HALF,), F32), rest, jnp.ones((ROPE_HALF,), F32), rest])
    ang = pos_g * freq
    return jnp.cos(ang), jnp.sin(ang) * sign


def _in_proj_call(h2d, mix_norm, w_in, q_gain, k_gain, conv_w, conv_b, i_bias, f_bias, batch, seq, tm):
    n_rows = h2d.shape[0]
    perm = jnp.asarray(_rope_lane_perm(), dtype=jnp.int32)
    slot_perm = (jnp.arange(ATT_SLOTS, dtype=jnp.int32)[:, None] * HEAD_DIM + perm[None, :]).reshape(-1)
    wq, wk, wv = (w_in[:, i * ATT_WIDTH:(i + 1) * ATT_WIDTH] for i in range(3))
    w_att = []
    for g in range(N_ATT_GROUPS):
        cols = slice(g * ATT_OUT_WIDTH, (g + 1) * ATT_OUT_WIDTH)
        w_att.append(jnp.concatenate([wq[:, cols][:, slot_perm], wk[:, cols][:, slot_perm], wv[:, cols]], axis=1))
    w_att = jnp.stack(w_att).astype(BF16)
    o = 3 * ATT_WIDTH
    wm = w_in[:, o:o + 3 * MLSTM_WIDTH].astype(BF16)
    o += 4 * MLSTM_WIDTH
    wgt = w_in[:, o:o + 2 * MLSTM_HEADS].T.astype(BF16)
    gbt = jnp.concatenate([i_bias, f_bias]).astype(F32).reshape(2 * MLSTM_HEADS, 1)
    qg = q_gain[perm].reshape(1, HEAD_DIM)
    kg = k_gain[perm].reshape(1, HEAD_DIM)
    cos_g, sin_g = _rope_tables(seq)
    eye = jnp.eye(PERM_ROWS, dtype=F32)
    perms = jnp.stack([_deinterleave_rows(eye, d) for _, d in ATT_GROUPS[1:]]).astype(BF16)
    g2 = mix_norm.reshape(1, D_MODEL)
    cb2 = conv_b.reshape(1, 2 * MLSTM_WIDTH)
    tiles_per_seq = seq // tm

    row = lambda w: pl.BlockSpec((tm, w), lambda t, b: (b * tiles_per_seq + t, 0))
    prev_rows = pl.BlockSpec(
        (CONV_PAD, D_MODEL), lambda t, b: (jnp.maximum((b * tiles_per_seq + t) * (tm // CONV_PAD) - 1, 0), 0))
    pos = pl.BlockSpec((N_ATT_GROUPS, tm, HEAD_DIM), lambda t, b: (0, t, 0))
    in_specs = [row(D_MODEL), prev_rows, _resident(g2.shape), _resident(w_att.shape), _resident(wm.shape),
                _resident(wgt.shape), _resident(qg.shape), _resident(kg.shape), pos, pos,
                _resident(perms.shape), _resident(conv_w.shape), _resident(cb2.shape), _resident(gbt.shape)]
    out_specs, out_shape = [], []
    for _, d in ATT_GROUPS:
        spec = pl.BlockSpec((1, d, tm // d, ATT_OUT_WIDTH), lambda t, b: (b, 0, t, 0))
        out_specs += [spec] * 3
        out_shape += [jax.ShapeDtypeStruct((batch, d, seq // d, ATT_OUT_WIDTH), BF16)] * 3
    out_specs += [row(MLSTM_WIDTH)] * 3 + [pl.BlockSpec((1, 2 * MLSTM_HEADS, tm), lambda t, b: (b, 0, t))]
    out_shape += [jax.ShapeDtypeStruct((n_rows, MLSTM_WIDTH), BF16)] * 3 + [
        jax.ShapeDtypeStruct((batch, 2 * MLSTM_HEADS, seq), F32)]
    outs = pl.pallas_call(
        _in_proj_kernel,
        grid=(tiles_per_seq, batch),
        in_specs=in_specs,
        out_specs=out_specs,
        out_shape=out_shape,
        scratch_shapes=[pltpu.VMEM((2 * MLSTM_WIDTH // HEAD_DIM, tm + CONV_PAD, HEAD_DIM), F32),
                        pltpu.VMEM((N_ATT_GROUPS, tm, D_MODEL), BF16)],
        compiler_params=_params(("parallel", "parallel")),
        name="in_proj",
    )(h2d, h2d, g2, w_att, wm, wgt, qg, kg, cos_g, sin_g, perms, conv_w, cb2, gbt)
    qkv = [tuple(outs[3 * g:3 * g + 3]) for g in range(N_ATT_GROUPS)]
    return qkv, outs[9:]


def _attention_kernel(dil, nblk, q_ref, kc_ref, vc_ref, o_ref, kp_ref, vp_ref):
    slot_in = pl.program_id(1) % 2
    slot_out = 1 - slot_in

    @pl.when(pl.program_id(1) == 0)
    def _():
        kp_ref[0] = jnp.zeros(kp_ref.shape[1:], kp_ref.dtype)
        vp_ref[0] = jnp.zeros(vp_ref.shape[1:], vp_ref.dtype)

    qi = lax.broadcasted_iota(jnp.int32, (ATT_BLOCK, 2 * ATT_BLOCK), 0)
    kj = lax.broadcasted_iota(jnp.int32, (ATT_BLOCK, 2 * ATT_BLOCK), 1)
    band = (kj >= qi) & (kj <= qi + ATT_BLOCK)
    bias = jnp.where(band, 0.0, -jnp.inf)
    first_key = jnp.where(pl.program_id(1) > 0, 0, ATT_BLOCK)
    bias_head = jnp.where(kj >= first_key, bias, -jnp.inf)
    lane = lax.broadcasted_iota(jnp.int32, (ATT_BLOCK, HEAD_DIM), 1)

    heads = [slice(j * HEAD_DIM, (j + 1) * HEAD_DIM) for j in range(ATT_SLOTS)]

    def scores(r, blk):
        q = q_ref[0, r, blk * ATT_BLOCK:(blk + 1) * ATT_BLOCK, :]
        if blk == 0:
            k = jnp.concatenate([kp_ref[slot_in, r], kc_ref[0, r, 0:ATT_BLOCK, :]], axis=0)
            tile_bias = bias_head
        else:
            k = kc_ref[0, r, (blk - 1) * ATT_BLOCK:(blk + 1) * ATT_BLOCK, :]
            tile_bias = bias
        return [_dot_nt(q[:, sl], k[:, sl]) + tile_bias for sl in heads]

    def softmax(s_list):
        out = []
        for s in s_list:
            m = jnp.max(s, axis=-1, keepdims=True)
            e = jnp.exp(s - m)
            den = jnp.sum(e, axis=-1, keepdims=True)
            out.append((e.astype(BF16), den, m + jnp.log(den)))
        return out

    def weighted_values(r, blk, p_list):
        if blk == 0:
            v = jnp.concatenate([vp_ref[slot_in, r], vc_ref[0, r, 0:ATT_BLOCK, :]], axis=0)
        else:
            v = vc_ref[0, r, (blk - 1) * ATT_BLOCK:(blk + 1) * ATT_BLOCK, :]
        return [_dot(p, v[:, sl]) for (p, _, _), sl in zip(p_list, heads)]

    def store(r, blk, p_list, pv_list):
        if dil == 1:
            rows = pl.ds(blk * ATT_BLOCK, ATT_BLOCK)
        else:
            rows = pl.ds(blk * ATT_BLOCK * dil + r, ATT_BLOCK, stride=dil)
        lse_slab = jnp.zeros((ATT_BLOCK, HEAD_DIM), F32)
        for j, ((_, den, lse), pv) in enumerate(zip(p_list, pv_list)):
            o_ref[j, rows, :] = pv / den
            lse_slab = jnp.where(lane == j, lse, lse_slab)
        o_ref[ATT_SLOTS, rows, :] = lse_slab

    def run_tiles(tiles):
        s_next = scores(*tiles[0])
        for idx, (r, blk) in enumerate(tiles):
            s_cur = s_next
            if idx + 1 < len(tiles):
                s_next = scores(*tiles[idx + 1])
            p_list = softmax(s_cur)
            store(r, blk, p_list, weighted_values(r, blk, p_list))

    res_per_iter = ATT_TILES_PER_ITER // nblk
    if dil == res_per_iter:
        run_tiles([(r, blk) for r in range(dil) for blk in range(nblk)])
    else:
        def body(it, carry):
            run_tiles([(it * res_per_iter + rr, blk) for rr in range(res_per_iter) for blk in range(nblk)])
            return carry
        lax.fori_loop(0, dil // res_per_iter, body, 0)

    last = slice((nblk - 1) * ATT_BLOCK, nblk * ATT_BLOCK)
    kp_ref[slot_out] = kc_ref[0, :, last, :]
    vp_ref[slot_out] = vc_ref[0, :, last, :]


def _attention_call(q, k, v, group, dilation, nblk, batch, seq):
    n = seq // dilation
    steps = n // (nblk * ATT_BLOCK)
    span = nblk * ATT_BLOCK * dilation
    cur = pl.BlockSpec((1, dilation, nblk * ATT_BLOCK, ATT_OUT_WIDTH), lambda b, i: (b, 0, i, 0))
    carry = pltpu.VMEM((2, dilation, ATT_BLOCK, ATT_OUT_WIDTH), BF16)
    return pl.pallas_call(
        functools.partial(_attention_kernel, dilation, nblk),
        grid=(batch, steps),
        in_specs=[cur, cur, cur],
        out_specs=pl.BlockSpec((ATT_SLABS, span, HEAD_DIM), lambda b, i: (0, b * steps + i, 0)),
        out_shape=jax.ShapeDtypeStruct((ATT_SLABS, batch * seq, HEAD_DIM), F32),
        scratch_shapes=[carry, carry],
        compiler_params=_params(("arbitrary", "arbitrary")),
        name=f"attention_g{group}",
    )(q, k, v)


def _mlstm_kernel(mq_ref, mk_ref, mv_ref, gt_ref, o_ref, state_ref, m_ref):
    L = MLSTM_CHUNK
    H = MLSTM_HEADS

    @pl.when(pl.program_id(0) == 0)
    def _():
        state_ref[...] = jnp.zeros_like(state_ref)
        m_ref[...] = jnp.zeros_like(m_ref)

    ti = lax.broadcasted_iota(jnp.int32, (L, L), 0)
    si = lax.broadcasted_iota(jnp.int32, (L, L), 1)
    causal = ti >= si
    tri = causal.astype(F32)
    ones = jnp.ones((L, HEAD_DIM), BF16)

    nbatch = mq_ref.shape[0]
    streams = [divmod(bh, H) for bh in range(nbatch * H)]
    lanes = lambda hd: slice(hd * HEAD_DIM, (hd + 1) * HEAD_DIM)

    gts = [gt_ref[bi] for bi in range(nbatch)]
    b_rows = [lax.dot_general(gt, tri, (((1,), (1,)), ((), ())), precision=lax.Precision.HIGHEST,
                              preferred_element_type=F32) for gt in gts]

    qs = [mq_ref[bi, :, lanes(hd)] for bi, hd in streams]
    ks = [mk_ref[bi, :, lanes(hd)] for bi, hd in streams]
    vexts = [jnp.concatenate([mv_ref[bi, :, lanes(hd)], ones], axis=1) for bi, hd in streams]
    states = [state_ref[bh] for bh in range(len(streams))]
    m_prevs = [m_ref[bh, 0:1, 0:1] for bh in range(len(streams))]
    qk = [_dot_nt(q, k) for q, k in zip(qs, ks)]
    q_state = [_dot(q, st.astype(BF16)) for q, st in zip(qs, states)]

    c_rows, upd, decays, m_news = [], [], [], []
    for bh, (bi, hd) in enumerate(streams):
        b_row = b_rows[bi][H + hd:H + hd + 1, :]
        c_row = gts[bi][hd:hd + 1, :] - b_row
        b_last = b_row[:, L - 1:L]
        g_row = b_last + c_row
        m_new = jnp.maximum(b_last + m_prevs[bh], jnp.max(g_row, axis=-1, keepdims=True))
        wk_row = jnp.exp(g_row - m_new)
        c_rows.append(c_row)
        decays.append(jnp.exp(b_last + m_prevs[bh] - m_new))
        m_news.append(m_new)
        k_t = ks[bh].astype(F32).T
        upd.append(_dot((k_t * wk_row).astype(BF16), vexts[bh]))

    s_list, a_list, floor_list = [], [], []
    for bh, (bi, hd) in enumerate(streams):
        cm = jnp.where(causal, c_rows[bh], -jnp.inf)
        n_t = jnp.maximum(m_prevs[bh], jnp.max(cm, axis=-1, keepdims=True))
        b_t = jnp.sum(jnp.where(causal, gts[bi][H + hd:H + hd + 1, :], 0.0), axis=-1, keepdims=True)
        s_list.append((qk[bh] * jnp.exp(cm - n_t)).astype(BF16))
        a_list.append(jnp.exp(m_prevs[bh] - n_t))
        floor_list.append(jnp.exp(-(b_t + n_t)))

    sv = [_dot(s, vext) for s, vext in zip(s_list, vexts)]
    for bh, (bi, hd) in enumerate(streams):
        ext = a_list[bh] * q_state[bh] + sv[bh]
        num = ext[:, :HEAD_DIM]
        den = ext[:, HEAD_DIM:]
        o_ref[bi, :, lanes(hd)] = (num / jnp.maximum(jnp.abs(den), floor_list[bh])).astype(o_ref.dtype)
        state_ref[bh] = decays[bh] * states[bh] + upd[bh]
        m_ref[bh] = jnp.broadcast_to(m_news[bh], m_ref.shape[1:])


def _mlstm_call(mq, mk, mv, gt, batch, seq):
    nc = seq // MLSTM_CHUNK
    W = MLSTM_WIDTH
    blk = lambda w: pl.BlockSpec((batch, MLSTM_CHUNK, w), lambda c: (0, c, 0))
    out = pl.pallas_call(
        _mlstm_kernel,
        grid=(nc,),
        in_specs=[blk(W), blk(W), blk(W),
                  pl.BlockSpec((batch, 2 * MLSTM_HEADS, MLSTM_CHUNK), lambda c: (0, 0, c))],
        out_specs=blk(W),
        out_shape=jax.ShapeDtypeStruct((batch, seq, W), BF16),
        scratch_shapes=[pltpu.VMEM((batch * MLSTM_HEADS, HEAD_DIM, 2 * HEAD_DIM), F32),
                        pltpu.VMEM((batch * MLSTM_HEADS, 8, 128), F32)],
        compiler_params=_params(("arbitrary",)),
        name="mlstm",
    )(mq.reshape(batch, seq, W), mk.reshape(batch, seq, W), mv.reshape(batch, seq, W), gt)
    return out.reshape(batch * seq, W)


def _mix_out_kernel(h_ref, a0_ref, a1_ref, a2_ref, ml_ref, g_ref, wg_ref, wua_ref, wub_ref, wo_ref, out_ref):
    h = h_ref[...]
    u = _rms_norm(h, g_ref[...]).astype(BF16)
    zg = _dot(u, wg_ref[...])
    mo = zg[:, :MLSTM_WIDTH]
    ga = zg[:, MLSTM_WIDTH:MLSTM_WIDTH + D_MODEL]
    gb = zg[:, MLSTM_WIDTH + D_MODEL:]

    a_refs = (a0_ref, a1_ref, a2_ref)
    lses = [a[ATT_SLOTS] for a in a_refs]
    merged_heads = []
    for j in range(ATT_SLOTS):
        ls = [l[:, j:j + 1] for l in lses]
        mx = jnp.maximum(jnp.maximum(ls[0], ls[1]), ls[2])
        ws = [jnp.exp(l - mx) for l in ls]
        inv = 1.0 / (ws[0] + ws[1] + ws[2])
        acc = (ws[0] * inv) * a_refs[0][j]
        acc = acc + (ws[1] * inv) * a_refs[1][j]
        acc = acc + (ws[2] * inv) * a_refs[2][j]
        merged_heads.append(acc.astype(BF16))
    att = jnp.concatenate(merged_heads, axis=1)
    y_att = _dot(att, wua_ref[...])
    y_ml = _dot((jax.nn.sigmoid(mo) * ml_ref[...]).astype(BF16), wub_ref[...])
    merged = jax.nn.sigmoid(ga) * y_att + jax.nn.sigmoid(gb) * y_ml
    out_ref[...] = h + _dot(merged.astype(BF16), wo_ref[...])


def _mix_out_call(h2d, att, ml, mix_norm, w_in, w_up_att, w_up_mlstm, w_out, tm):
    n_rows = h2d.shape[0]
    g2 = mix_norm.reshape(1, D_MODEL)
    o_mo = 3 * ATT_WIDTH + 3 * MLSTM_WIDTH
    o_g = o_mo + MLSTM_WIDTH + 2 * MLSTM_HEADS
    wg = jnp.concatenate([w_in[:, o_mo:o_mo + MLSTM_WIDTH], w_in[:, o_g:o_g + 2 * D_MODEL]], axis=1).astype(BF16)
    wua = w_up_att.astype(BF16)
    wub = w_up_mlstm.astype(BF16)
    wo = w_out.astype(BF16)
    row = lambda w: pl.BlockSpec((tm, w), lambda i: (i, 0))
    slabs = pl.BlockSpec((ATT_SLABS, tm, HEAD_DIM), lambda i: (0, i, 0))
    return pl.pallas_call(
        _mix_out_kernel,
        grid=(n_rows // tm,),
        in_specs=[row(D_MODEL), slabs, slabs, slabs, row(MLSTM_WIDTH),
                  _resident(g2.shape), _resident(wg.shape), _resident(wua.shape), _resident(wub.shape),
                  _resident(wo.shape)],
        out_specs=row(D_MODEL),
        out_shape=jax.ShapeDtypeStruct((n_rows, D_MODEL), F32),
        compiler_params=_params(("parallel",)),
        name="mix_out",
    )(h2d, *att, ml, g2, wg, wua, wub, wo)


def _layer(x2d, p2d, batch, seq, ffn1_norm, ffn1_w_in, ffn1_w_out, mix_norm, w_in, q_gain, k_gain, conv_w, conv_b,
           i_bias, f_bias, w_up_att, w_up_mlstm, w_out, ffn2_norm, ffn2_w_in, ffn2_w_out, ple_norm, w_ple_gate,
           w_ple_proj):
    tm = 512
    h = _ffn_call(x2d, ffn1_norm, ffn1_w_in, ffn1_w_out, tm)
    qkv, (mq, mk, mv, gt) = _in_proj_call(h, mix_norm, w_in, q_gain, k_gain, conv_w, conv_b, i_bias, f_bias,
                                              batch, seq, 256)
    att = []
    for g, (window, dilation) in enumerate(ATT_GROUPS):
        assert window // dilation == ATT_BLOCK
        att.append(_attention_call(*qkv[g], g, dilation, ATT_BLOCKS_PER_STEP[g], batch, seq))
    ml = _mlstm_call(mq, mk, mv, gt, batch, seq)
    h = _mix_out_call(h, att, ml, mix_norm, w_in, w_up_att, w_up_mlstm, w_out, tm)
    ple = (p2d, ple_norm, w_ple_gate.astype(BF16), w_ple_proj.astype(BF16))
    return _ffn_call(h, ffn2_norm, ffn2_w_in, ffn2_w_out, tm, ple=ple)


def kernel(x, p, ffn1_norm, ffn1_w_in, ffn1_w_out, mix_norm, w_in, q_gain, k_gain, conv_w, conv_b, i_bias, f_bias,
           w_up_att, w_up_mlstm, w_out, ffn2_norm, ffn2_w_in, ffn2_w_out, ple_norm, w_ple_gate, w_ple_proj):
    batch, seq, d = x.shape
    depth = p.shape[0]
    assert d == D_MODEL and seq % (ATT_GROUPS[-1][1] * ATT_BLOCK) == 0
    h = x.reshape(batch * seq, d)
    for i in range(depth):
        h = _layer(h, p[i].reshape(batch * seq, PLE_DIM), batch, seq, ffn1_norm[i], ffn1_w_in[i], ffn1_w_out[i],
                   mix_norm[i], w_in[i], q_gain[i], k_gain[i], conv_w[i], conv_b[i], i_bias[i], f_bias[i],
                   w_up_att[i], w_up_mlstm[i], w_out[i], ffn2_norm[i], ffn2_w_in[i], ffn2_w_out[i],
                   ple_norm[i], w_ple_gate[i], w_ple_proj[i])
    return h.reshape(batch, seq, d)
```

```python
import functools

import jax
import jax.numpy as jnp
from jax import lax
from jax.experimental import pallas as pl
from jax.experimental.pallas import tpu as pltpu

D_MODEL = 1024
PLE_DIM = 256
ATT_GROUPS = ((128, 1), (512, 4), (2048, 16))
N_ATT_GROUPS = 3
ATT_SLOTS = 4
ATT_HEADS = N_ATT_GROUPS * ATT_SLOTS
HEAD_DIM = 128
ATT_WIDTH = ATT_HEADS * HEAD_DIM
ATT_OUT_WIDTH = ATT_SLOTS * HEAD_DIM
ROPE_THETA = 500000.0
ROPE_DIM = HEAD_DIM // 4
ROPE_HALF = ROPE_DIM // 2
ROPE_SPLIT = 64
MLSTM_HEADS = 4
MLSTM_WIDTH = MLSTM_HEADS * HEAD_DIM
MLSTM_CHUNK = 128
CONV_WIDTH = 4
D_FF = 2816
NORM_EPS = 1e-6

FF_CHUNKS = ((0, 512), (512, 512), (1024, 512), (1536, 512), (2048, 512), (2560, 256))
ATT_BLOCK = 128
ATT_BLOCKS_PER_STEP = (8, 2, 1)
ATT_TILES_PER_ITER = 8
ATT_SLABS = ATT_SLOTS + 1
PERM_ROWS = 256
CONV_PAD = 8
V7X_VMEM_LIMIT = 56 * 1024 * 1024

BF16 = jnp.bfloat16
F32 = jnp.float32


def _rms_norm(x, g):
    ms = jnp.mean(x * x, axis=-1, keepdims=True)
    return x * lax.rsqrt(ms + NORM_EPS) * g


def _silu(x):
    return x * jax.nn.sigmoid(x)


def _dot(a, b):
    return jnp.dot(a, b, preferred_element_type=F32)


def _dot_nt(a, b):
    return lax.dot_general(a, b, (((1,), (1,)), ((), ())), preferred_element_type=F32)


def _dot_tn(a, b):
    return lax.dot_general(a, b, (((0,), (0,)), ((), ())), preferred_element_type=F32)


def _resident(shape):
    nd = len(shape)
    return pl.BlockSpec(shape, lambda *_: (0,) * nd, pipeline_mode=pl.Buffered(1))


def _params(semantics, flags=None):
    return pltpu.CompilerParams(dimension_semantics=semantics, vmem_limit_bytes=V7X_VMEM_LIMIT, flags=flags)


def _swiglu_update(x_ref, g_ref, w_in_ref, w_out_ref, acc_ref):
    u = _rms_norm(x_ref[...], g_ref[...]).astype(BF16)
    for idx, (off, width) in enumerate(FF_CHUNKS):
        a = _dot(u, w_in_ref[:, off:off + width])
        b = _dot(u, w_in_ref[:, D_FF + off:D_FF + off + width])
        part = _dot((_silu(a) * b).astype(BF16), w_out_ref[off:off + width, :])
        if idx == 0:
            acc_ref[...] = part
        else:
            acc_ref[...] += part
    return x_ref[...] + 0.5 * acc_ref[...]


def _ffn_kernel(x_ref, g_ref, w_in_ref, w_out_ref, o_ref, acc_ref):
    o_ref[...] = _swiglu_update(x_ref, g_ref, w_in_ref, w_out_ref, acc_ref)


def _ffn_ple_kernel(x_ref, p_ref, g_ref, w_in_ref, w_out_ref, pg_ref, w_pg_ref, w_pe_ref, o_ref, acc_ref):
    h = _swiglu_update(x_ref, g_ref, w_in_ref, w_out_ref, acc_ref)
    u = _rms_norm(h, pg_ref[...]).astype(BF16)
    gate = jax.nn.sigmoid(_dot(u, w_pg_ref[...]))
    pe = _dot(p_ref[...].astype(BF16), w_pe_ref[...])
    o_ref[...] = h + pe * gate


def _ffn_call(x2d, norm_g, w_in, w_out, tm, ple=None):
    n_rows = x2d.shape[0]
    row = lambda w: pl.BlockSpec((tm, w), lambda i: (i, 0))
    g2 = norm_g.reshape(1, D_MODEL)
    w_in = w_in.astype(BF16)
    w_out = w_out.astype(BF16)
    if ple is None:
        kern = _ffn_kernel
        args = (x2d, g2, w_in, w_out)
        specs = [row(D_MODEL), _resident(g2.shape), _resident(w_in.shape), _resident(w_out.shape)]
        name = "ffn"
    else:
        p2d, ple_g, w_pg, w_pe = ple
        pg2 = ple_g.reshape(1, D_MODEL)
        kern = _ffn_ple_kernel
        args = (x2d, p2d, g2, w_in, w_out, pg2, w_pg, w_pe)
        specs = [row(D_MODEL), row(PLE_DIM), _resident(g2.shape), _resident(w_in.shape),
                 _resident(w_out.shape), _resident(pg2.shape), _resident(w_pg.shape), _resident(w_pe.shape)]
        name = "ffn_ple"
    return pl.pallas_call(
        kern,
        grid=(n_rows // tm,),
        in_specs=specs,
        out_specs=row(D_MODEL),
        out_shape=jax.ShapeDtypeStruct((n_rows, D_MODEL), F32),
        scratch_shapes=[pltpu.VMEM((tm, D_MODEL), F32)],
        compiler_params=_params(("parallel",)),
        name=name,
    )(*args)


def _log_sigmoid(x):
    return jnp.minimum(x, 0.0) - jnp.log1p(jnp.exp(-jnp.abs(x)))


def _in_proj_kernel(h_ref, hp_ref, g_ref, watt_ref, wm_ref, wgt_ref, qg_ref, kg_ref,
                    cos_ref, sin_ref, perm_ref, cw_ref, cb_ref, gbt_ref,
                    q0_ref, k0_ref, v0_ref, q1_ref, k1_ref, v1_ref, q2_ref, k2_ref, v2_ref,
                    mq_ref, mk_ref, mv_ref, gt_ref, xpad_ref, u_ref):
    tm = h_ref.shape[0]
    gnorm = g_ref[...]
    u = _rms_norm(h_ref[...], gnorm).astype(BF16)
    qgain = qg_ref[...]
    kgain = kg_ref[...]
    att_refs = ((q0_ref, k0_ref, v0_ref), (q1_ref, k1_ref, v1_ref), (q2_ref, k2_ref, v2_ref))

    def qk_head(zh, gain, cos, sin, scale):
        y = _rms_norm(zh, gain)
        y = y * cos + pltpu.roll(y, HEAD_DIM // 2, 1) * sin
        if scale != 1.0:
            y = y * scale
        return y.astype(BF16)

    def deinterleaved(grp, dil):
        if dil == 1:
            return u
        pm = perm_ref[grp - 1]
        return jnp.concatenate(
            [_dot(pm, u[kb * PERM_ROWS:(kb + 1) * PERM_ROWS, :]) for kb in range(tm // PERM_ROWS)],
            axis=0).astype(BF16)

    def store_att(o_ref, val, dil):
        if dil == 1:
            o_ref[0, 0] = val
            return
        rows_per_res = PERM_ROWS // dil
        for kb in range(tm // PERM_ROWS):
            for r in range(dil):
                src = kb * PERM_ROWS + r * rows_per_res
                o_ref[0, r, kb * rows_per_res:(kb + 1) * rows_per_res, :] = val[src:src + rows_per_res, :]

    def att_item(grp, dil, part):
        cols = slice(part * ATT_OUT_WIDTH, (part + 1) * ATT_OUT_WIDTH)
        matmul = lambda: _dot(u_ref[grp], watt_ref[grp, :, cols])

        def epilogue(z):
            if part == 2:
                val = z.astype(BF16)
            else:
                gain, scale = ((qgain, HEAD_DIM ** -0.5), (kgain, 1.0))[part]
                cos = cos_ref[grp]
                sin = sin_ref[grp]
                val = jnp.concatenate([qk_head(z[:, j * HEAD_DIM:(j + 1) * HEAD_DIM], gain, cos, sin, scale)
                                       for j in range(ATT_SLOTS)], axis=1)
            store_att(att_refs[grp][part], val, dil)
        return matmul, epilogue

    def conv_item(part, o_ref, scale):
        cols = slice(part * MLSTM_WIDTH, (part + 1) * MLSTM_WIDTH)

        def matmul():
            up = _rms_norm(hp_ref[...], gnorm).astype(BF16)
            return _dot(u_ref[0], wm_ref[:, cols]), _dot(up, wm_ref[:, cols])

        def epilogue(zz):
            z, zp = zz
            zp = jnp.where(pl.program_id(0) > 0, zp, 0.0)
            for c in range(MLSTM_WIDTH // HEAD_DIM):
                slab = part * (MLSTM_WIDTH // HEAD_DIM) + c
                lanes = slice(part * MLSTM_WIDTH + c * HEAD_DIM, part * MLSTM_WIDTH + (c + 1) * HEAD_DIM)
                xpad_ref[slab, 0:CONV_PAD, :] = zp[:, c * HEAD_DIM:(c + 1) * HEAD_DIM]
                xpad_ref[slab, CONV_PAD:CONV_PAD + tm, :] = z[:, c * HEAD_DIM:(c + 1) * HEAD_DIM]
                acc = cb_ref[:, lanes]
                for j in range(CONV_WIDTH):
                    off = CONV_PAD - (CONV_WIDTH - 1) + j
                    acc = acc + xpad_ref[slab, off:off + tm, :] * cw_ref[j:j + 1, lanes]
                y = _silu(acc)
                if scale != 1.0:
                    y = y * scale
                o_ref[:, c * HEAD_DIM:(c + 1) * HEAD_DIM] = y.astype(BF16)
        return matmul, epilogue

    def value_gate_item():
        def matmul():
            return _dot(u_ref[0], wm_ref[:, 2 * MLSTM_WIDTH:]), _dot_nt(wgt_ref[...], u_ref[0])

        def epilogue(zz):
            zv, zt = zz
            mv_ref[...] = zv.astype(BF16)
            zt = zt + gbt_ref[...]
            sub = lax.broadcasted_iota(jnp.int32, zt.shape, 0)
            gt_ref[0] = jnp.where(sub < MLSTM_HEADS, zt, _log_sigmoid(zt))
        return matmul, epilogue

    for grp, (_, dil) in enumerate(ATT_GROUPS):
        u_ref[grp] = deinterleaved(grp, dil)
    light = [conv_item(0, mq_ref, 1.0), conv_item(1, mk_ref, HEAD_DIM ** -0.5), value_gate_item()]
    items = []
    for grp, (_, dil) in enumerate(ATT_GROUPS):
        items += [att_item(grp, dil, 0), att_item(grp, dil, 2), att_item(grp, dil, 1), light[grp]]
    pending = items[0][0]()
    for idx, (_, epilogue) in enumerate(items):
        current = pending
        if idx + 1 < len(items):
            pending = items[idx + 1][0]()
        epilogue(current)


def _rope_lane_perm():
    return (list(range(0, ROPE_HALF)) + list(range(ROPE_DIM, ROPE_DIM + 48))
            + list(range(ROPE_HALF, ROPE_DIM)) + list(range(ROPE_DIM + 48, HEAD_DIM)))


def _deinterleave_rows(t, dil):
    n, w = t.shape
    return t.reshape(n // PERM_ROWS, PERM_ROWS // dil, dil, w).transpose(0, 2, 1, 3).reshape(n, w)


def _rope_tables(seq):
    inv_freq = 1.0 / (ROPE_THETA ** (jnp.arange(ROPE_HALF, dtype=F32) / ROPE_HALF))
    rest = jnp.zeros((HEAD_DIM // 2 - ROPE_HALF,), F32)
    freq = jnp.concatenate([inv_freq, rest, inv_freq, rest])
    sign = jnp.concatenate([-jnp.ones((ROPE_HALF,), F32), rest, jnp.ones((ROPE_HALF,), F32), rest])
    coarse = (jnp.arange(seq // ROPE_SPLIT, dtype=F32) * ROPE_SPLIT).reshape(-1, 1) * freq
    fine = jnp.arange(ROPE_SPLIT, dtype=F32).reshape(-1, 1) * freq
    ca, sa = jnp.cos(coarse)[:, None, :], jnp.sin(coarse)[:, None, :]
    cb, sb = jnp.cos(fine)[None, :, :], jnp.sin(fine)[None, :, :]
    cos_t = (ca * cb - sa * sb).reshape(seq, HEAD_DIM)
    sin_t = ((sa * cb + ca * sb) * sign).reshape(seq, HEAD_DIM)
    cos_g = jnp.stack([_deinterleave_rows(cos_t, d) for _, d in ATT_GROUPS])
    sin_g = jnp.stack([_deinterleave_rows(sin_t, d) for _, d in ATT_GROUPS])
    return cos_g, sin_g


def _in_proj_call(h2d, mix_norm, w_in, q_gain, k_gain, conv_w, conv_b, i_bias, f_bias, batch, seq, tm):
    n_rows = h2d.shape[0]
    perm = jnp.asarray(_rope_lane_perm(), dtype=jnp.int32)
    slot_perm = (jnp.arange(ATT_SLOTS, dtype=jnp.int32)[:, None] * HEAD_DIM + perm[None, :]).reshape(-1)
    wq, wk, wv = (w_in[:, i * ATT_WIDTH:(i + 1) * ATT_WIDTH] for i in range(3))
    w_att = []
    for g in range(N_ATT_GROUPS):
        cols = slice(g * ATT_OUT_WIDTH, (g + 1) * ATT_OUT_WIDTH)
        w_att.append(jnp.concatenate([wq[:, cols][:, slot_perm], wk[:, cols][:, slot_perm], wv[:, cols]], axis=1))
    w_att = jnp.stack(w_att).astype(BF16)
    o = 3 * ATT_WIDTH
    wm = w_in[:, o:o + 3 * MLSTM_WIDTH].astype(BF16)
    o += 4 * MLSTM_WIDTH
    wgt = w_in[:, o:o + 2 * MLSTM_HEADS].T.astype(BF16)
    gbt = jnp.concatenate([i_bias, f_bias]).astype(F32).reshape(2 * MLSTM_HEADS, 1)
    qg = q_gain[perm].reshape(1, HEAD_DIM)
    kg = k_gain[perm].reshape(1, HEAD_DIM)
    cos_g, sin_g = _rope_tables(seq)
    eye = jnp.eye(PERM_ROWS, dtype=F32)
    perms = jnp.stack([_deinterleave_rows(eye, d) for _, d in ATT_GROUPS[1:]]).astype(BF16)
    g2 = mix_norm.reshape(1, D_MODEL)
    cb2 = conv_b.reshape(1, 2 * MLSTM_WIDTH)
    tiles_per_seq = seq // tm

    row = lambda w: pl.BlockSpec((tm, w), lambda t, b: (b * tiles_per_seq + t, 0))
    prev_rows = pl.BlockSpec(
        (CONV_PAD, D_MODEL), lambda t, b: (jnp.maximum((b * tiles_per_seq + t) * (tm // CONV_PAD) - 1, 0), 0))
    pos = pl.BlockSpec((N_ATT_GROUPS, tm, HEAD_DIM), lambda t, b: (0, t, 0))
    in_specs = [row(D_MODEL), prev_rows, _resident(g2.shape), _resident(w_att.shape), _resident(wm.shape),
                _resident(wgt.shape), _resident(qg.shape), _resident(kg.shape), pos, pos,
                _resident(perms.shape), _resident(conv_w.shape), _resident(cb2.shape), _resident(gbt.shape)]
    out_specs, out_shape = [], []
    for _, d in ATT_GROUPS:
        spec = pl.BlockSpec((1, d, tm // d, ATT_OUT_WIDTH), lambda t, b: (b, 0, t, 0))
        out_specs += [spec] * 3
        out_shape += [jax.ShapeDtypeStruct((batch, d, seq // d, ATT_OUT_WIDTH), BF16)] * 3
    out_specs += [row(MLSTM_WIDTH)] * 3 + [pl.BlockSpec((1, 2 * MLSTM_HEADS, tm), lambda t, b: (b, 0, t))]
    out_shape += [jax.ShapeDtypeStruct((n_rows, MLSTM_WIDTH), BF16)] * 3 + [
        jax.ShapeDtypeStruct((batch, 2 * MLSTM_HEADS, seq), F32)]
    outs = pl.pallas_call(
        _in_proj_kernel,
        grid=(tiles_per_seq, batch),
        in_specs=in_specs,
        out_specs=out_specs,
        out_shape=out_shape,
        scratch_shapes=[pltpu.VMEM((2 * MLSTM_WIDTH // HEAD_DIM, tm + CONV_PAD, HEAD_DIM), F32),
                        pltpu.VMEM((N_ATT_GROUPS, tm, D_MODEL), BF16)],
        compiler_params=_params(("parallel", "parallel")),
        name="in_proj",
    )(h2d, h2d, g2, w_att, wm, wgt, qg, kg, cos_g, sin_g, perms, conv_w, cb2, gbt)
    qkv = [tuple(outs[3 * g:3 * g + 3]) for g in range(N_ATT_GROUPS)]
    return qkv, outs[9:]


def _attention_kernel(dil, nblk, q_ref, kc_ref, vc_ref, o_ref, kp_ref, vp_ref):
    slot_in = pl.program_id(1) % 2
    slot_out = 1 - slot_in

    @pl.when(pl.program_id(1) == 0)
    def _():
        kp_ref[0] = jnp.zeros(kp_ref.shape[1:], kp_ref.dtype)
        vp_ref[0] = jnp.zeros(vp_ref.shape[1:], vp_ref.dtype)

    qi = lax.broadcasted_iota(jnp.int32, (ATT_BLOCK, 2 * ATT_BLOCK), 0)
    kj = lax.broadcasted_iota(jnp.int32, (ATT_BLOCK, 2 * ATT_BLOCK), 1)
    band = (kj >= qi) & (kj <= qi + ATT_BLOCK)
    bias = jnp.where(band, 0.0, -jnp.inf)
    first_key = jnp.where(pl.program_id(1) > 0, 0, ATT_BLOCK)
    bias_head = jnp.where(kj >= first_key, bias, -jnp.inf)
    lane = lax.broadcasted_iota(jnp.int32, (ATT_BLOCK, HEAD_DIM), 1)

    heads = [slice(j * HEAD_DIM, (j + 1) * HEAD_DIM) for j in range(ATT_SLOTS)]

    def scores(r, blk):
        q = q_ref[0, r, blk * ATT_BLOCK:(blk + 1) * ATT_BLOCK, :]
        if blk == 0:
            k = jnp.concatenate([kp_ref[slot_in, r], kc_ref[0, r, 0:ATT_BLOCK, :]], axis=0)
            tile_bias = bias_head
        else:
            k = kc_ref[0, r, (blk - 1) * ATT_BLOCK:(blk + 1) * ATT_BLOCK, :]
            tile_bias = bias
        return [_dot_nt(q[:, sl], k[:, sl]) + tile_bias for sl in heads]

    def softmax(s_list):
        out = []
        for s in s_list:
            m = jnp.max(s, axis=-1, keepdims=True)
            e = jnp.exp(s - m)
            den = jnp.sum(e, axis=-1, keepdims=True)
            out.append((e.astype(BF16), den, m + jnp.log(den)))
        return out

    def weighted_values(r, blk, p_list):
        if blk == 0:
            v = jnp.concatenate([vp_ref[slot_in, r], vc_ref[0, r, 0:ATT_BLOCK, :]], axis=0)
        else:
            v = vc_ref[0, r, (blk - 1) * ATT_BLOCK:(blk + 1) * ATT_BLOCK, :]
        return [_dot(p, v[:, sl]) for (p, _, _), sl in zip(p_list, heads)]

    def store(r, blk, p_list, pv_list):
        if dil == 1:
            rows = pl.ds(blk * ATT_BLOCK, ATT_BLOCK)
        else:
            rows = pl.ds(blk * ATT_BLOCK * dil + r, ATT_BLOCK, stride=dil)
        lse_slab = jnp.zeros((ATT_BLOCK, HEAD_DIM), F32)
        for j, ((_, den, lse), pv) in enumerate(zip(p_list, pv_list)):
            o_ref[j, rows, :] = pv / den
            lse_slab = jnp.where(lane == j, lse, lse_slab)
        o_ref[ATT_SLOTS, rows, :] = lse_slab

    def run_tiles(tiles):
        s_next = scores(*tiles[0])
        for idx, (r, blk) in enumerate(tiles):
            s_cur = s_next
            if idx + 1 < len(tiles):
                s_next = scores(*tiles[idx + 1])
            p_list = softmax(s_cur)
            store(r, blk, p_list, weighted_values(r, blk, p_list))

    res_per_iter = ATT_TILES_PER_ITER // nblk
    if dil == res_per_iter:
        run_tiles([(r, blk) for r in range(dil) for blk in range(nblk)])
    else:
        def body(it, carry):
            run_tiles([(it * res_per_iter + rr, blk) for rr in range(res_per_iter) for blk in range(nblk)])
            return carry
        lax.fori_loop(0, dil // res_per_iter, body, 0)

    last = slice((nblk - 1) * ATT_BLOCK, nblk * ATT_BLOCK)
    kp_ref[slot_out] = kc_ref[0, :, last, :]
    vp_ref[slot_out] = vc_ref[0, :, last, :]


def _attention_call(q, k, v, group, dilation, nblk, batch, seq):
    n = seq // dilation
    steps = n // (nblk * ATT_BLOCK)
    span = nblk * ATT_BLOCK * dilation
    cur = pl.BlockSpec((1, dilation, nblk * ATT_BLOCK, ATT_OUT_WIDTH), lambda b, i: (b, 0, i, 0))
    carry = pltpu.VMEM((2, dilation, ATT_BLOCK, ATT_OUT_WIDTH), BF16)
    return pl.pallas_call(
        functools.partial(_attention_kernel, dilation, nblk),
        grid=(batch, steps),
        in_specs=[cur, cur, cur],
        out_specs=pl.BlockSpec((ATT_SLABS, span, HEAD_DIM), lambda b, i: (0, b * steps + i, 0)),
        out_shape=jax.ShapeDtypeStruct((ATT_SLABS, batch * seq, HEAD_DIM), F32),
        scratch_shapes=[carry, carry],
        compiler_params=_params(("arbitrary", "arbitrary")),
        name=f"attention_g{group}",
    )(q, k, v)


def _mlstm_kernel(mq_ref, mk_ref, mv_ref, gt_ref, o_ref, state_ref, m_ref):
    L = MLSTM_CHUNK
    H = MLSTM_HEADS

    @pl.when(pl.program_id(0) == 0)
    def _():
        state_ref[...] = jnp.zeros_like(state_ref)
        m_ref[...] = jnp.zeros_like(m_ref)

    ti = lax.broadcasted_iota(jnp.int32, (L, L), 0)
    si = lax.broadcasted_iota(jnp.int32, (L, L), 1)
    causal = ti >= si
    tri = causal.astype(F32)
    ones = jnp.ones((L, HEAD_DIM), BF16)

    nbatch = mq_ref.shape[0]
    streams = [divmod(bh, H) for bh in range(nbatch * H)]
    lanes = lambda hd: slice(hd * HEAD_DIM, (hd + 1) * HEAD_DIM)

    gts = [gt_ref[bi] for bi in range(nbatch)]
    b_rows = [lax.dot_general(gt, tri, (((1,), (1,)), ((), ())), precision=lax.Precision.HIGHEST,
                              preferred_element_type=F32) for gt in gts]

    qs = [mq_ref[bi, :, lanes(hd)] for bi, hd in streams]
    ks = [mk_ref[bi, :, lanes(hd)] for bi, hd in streams]
    vexts = [jnp.concatenate([mv_ref[bi, :, lanes(hd)], ones], axis=1) for bi, hd in streams]
    states = [state_ref[bh] for bh in range(len(streams))]
    m_prevs = [m_ref[bh, 0:1, 0:1] for bh in range(len(streams))]
    kts = [k.T for k in ks]
    qk = [_dot(q, kt) for q, kt in zip(qs, kts)]
    q_state = [_dot(q, st.astype(BF16)) for q, st in zip(qs, states)]

    c_rows, upd, decays, m_news = [], [], [], []
    for bh, (bi, hd) in enumerate(streams):
        b_row = b_rows[bi][H + hd:H + hd + 1, :]
        c_row = gts[bi][hd:hd + 1, :] - b_row
        b_last = b_row[:, L - 1:L]
        g_row = b_last + c_row
        m_new = jnp.maximum(b_last + m_prevs[bh], jnp.max(g_row, axis=-1, keepdims=True))
        wk_row = jnp.exp(g_row - m_new)
        c_rows.append(c_row)
        decays.append(jnp.exp(b_last + m_prevs[bh] - m_new))
        m_news.append(m_new)
        upd.append(_dot((kts[bh] * wk_row).astype(BF16), vexts[bh]))

    s_list, a_list, floor_list = [], [], []
    for bh, (bi, hd) in enumerate(streams):
        cm = jnp.where(causal, c_rows[bh], -jnp.inf)
        n_t = jnp.maximum(m_prevs[bh], jnp.max(cm, axis=-1, keepdims=True))
        b_t = jnp.sum(jnp.where(causal, gts[bi][H + hd:H + hd + 1, :], 0.0), axis=-1, keepdims=True)
        s_list.append((qk[bh] * jnp.exp(cm - n_t)).astype(BF16))
        a_list.append(jnp.exp(m_prevs[bh] - n_t))
        floor_list.append(jnp.exp(-(b_t + n_t)))

    sv = [_dot(s, vext) for s, vext in zip(s_list, vexts)]
    for bh, (bi, hd) in enumerate(streams):
        ext = a_list[bh] * q_state[bh] + sv[bh]
        num = ext[:, :HEAD_DIM]
        den = ext[:, HEAD_DIM:]
        o_ref[bi, :, lanes(hd)] = (num / jnp.maximum(jnp.abs(den), floor_list[bh])).astype(o_ref.dtype)
        state_ref[bh] = decays[bh] * states[bh] + upd[bh]
        m_ref[bh] = jnp.broadcast_to(m_news[bh], m_ref.shape[1:])


def _mlstm_call(mq, mk, mv, gt, batch, seq):
    nc = seq // MLSTM_CHUNK
    W = MLSTM_WIDTH
    blk = lambda w: pl.BlockSpec((batch, MLSTM_CHUNK, w), lambda c: (0, c, 0))
    out = pl.pallas_call(
        _mlstm_kernel,
        grid=(nc,),
        in_specs=[blk(W), blk(W), blk(W),
                  pl.BlockSpec((batch, 2 * MLSTM_HEADS, MLSTM_CHUNK), lambda c: (0, 0, c))],
        out_specs=blk(W),
        out_shape=jax.ShapeDtypeStruct((batch, seq, W), BF16),
        scratch_shapes=[pltpu.VMEM((batch * MLSTM_HEADS, HEAD_DIM, 2 * HEAD_DIM), F32),
                        pltpu.VMEM((batch * MLSTM_HEADS, 8, 128), F32)],
        compiler_params=_params(("arbitrary",)),
        name="mlstm",
    )(mq.reshape(batch, seq, W), mk.reshape(batch, seq, W), mv.reshape(batch, seq, W), gt)
    return out.reshape(batch * seq, W)


def _mix_out_kernel(h_ref, a0_ref, a1_ref, a2_ref, ml_ref, g_ref, wg_ref, wua_ref, wub_ref, wo_ref, out_ref):
    h = h_ref[...]
    u = _rms_norm(h, g_ref[...]).astype(BF16)
    zg = _dot(u, wg_ref[...])
    mo = zg[:, :MLSTM_WIDTH]
    ga = zg[:, MLSTM_WIDTH:MLSTM_WIDTH + D_MODEL]
    gb = zg[:, MLSTM_WIDTH + D_MODEL:]

    a_refs = (a0_ref, a1_ref, a2_ref)
    lses = [a[ATT_SLOTS] for a in a_refs]
    mx = jnp.maximum(jnp.maximum(lses[0], lses[1]), lses[2])
    ws = [jnp.exp(l - mx) for l in lses]
    inv = 1.0 / (ws[0] + ws[1] + ws[2])
    alphas = [w * inv for w in ws]
    merged_heads = []
    for j in range(ATT_SLOTS):
        acc = alphas[0][:, j:j + 1] * a_refs[0][j]
        acc = acc + alphas[1][:, j:j + 1] * a_refs[1][j]
        acc = acc + alphas[2][:, j:j + 1] * a_refs[2][j]
        merged_heads.append(acc.astype(BF16))
    att = jnp.concatenate(merged_heads, axis=1)
    y_att = _dot(att, wua_ref[...])
    y_ml = _dot((jax.nn.sigmoid(mo) * ml_ref[...]).astype(BF16), wub_ref[...])
    merged = jax.nn.sigmoid(ga) * y_att + jax.nn.sigmoid(gb) * y_ml
    out_ref[...] = h + _dot(merged.astype(BF16), wo_ref[...])


def _mix_out_call(h2d, att, ml, mix_norm, w_in, w_up_att, w_up_mlstm, w_out, tm):
    n_rows = h2d.shape[0]
    g2 = mix_norm.reshape(1, D_MODEL)
    o_mo = 3 * ATT_WIDTH + 3 * MLSTM_WIDTH
    o_g = o_mo + MLSTM_WIDTH + 2 * MLSTM_HEADS
    wg = jnp.concatenate([w_in[:, o_mo:o_mo + MLSTM_WIDTH], w_in[:, o_g:o_g + 2 * D_MODEL]], axis=1).astype(BF16)
    wua = w_up_att.astype(BF16)
    wub = w_up_mlstm.astype(BF16)
    wo = w_out.astype(BF16)
    row = lambda w: pl.BlockSpec((tm, w), lambda i: (i, 0))
    slabs = pl.BlockSpec((ATT_SLABS, tm, HEAD_DIM), lambda i: (0, i, 0))
    return pl.pallas_call(
        _mix_out_kernel,
        grid=(n_rows // tm,),
        in_specs=[row(D_MODEL), slabs, slabs, slabs, row(MLSTM_WIDTH),
                  _resident(g2.shape), _resident(wg.shape), _resident(wua.shape), _resident(wub.shape),
                  _resident(wo.shape)],
        out_specs=row(D_MODEL),
        out_shape=jax.ShapeDtypeStruct((n_rows, D_MODEL), F32),
        compiler_params=_params(("parallel",)),
        name="mix_out",
    )(h2d, *att, ml, g2, wg, wua, wub, wo)


def _layer(x2d, p2d, batch, seq, ffn1_norm, ffn1_w_in, ffn1_w_out, mix_norm, w_in, q_gain, k_gain, conv_w, conv_b,
           i_bias, f_bias, w_up_att, w_up_mlstm, w_out, ffn2_norm, ffn2_w_in, ffn2_w_out, ple_norm, w_ple_gate,
           w_ple_proj):
    tm = 512
    h = _ffn_call(x2d, ffn1_norm, ffn1_w_in, ffn1_w_out, 1024)
    qkv, (mq, mk, mv, gt) = _in_proj_call(h, mix_norm, w_in, q_gain, k_gain, conv_w, conv_b, i_bias, f_bias,
                                              batch, seq, 256)
    att = []
    for g, (window, dilation) in enumerate(ATT_GROUPS):
        assert window // dilation == ATT_BLOCK
        att.append(_attention_call(*qkv[g], g, dilation, ATT_BLOCKS_PER_STEP[g], batch, seq))
    ml = _mlstm_call(mq, mk, mv, gt, batch, seq)
    h = _mix_out_call(h, att, ml, mix_norm, w_in, w_up_att, w_up_mlstm, w_out, tm)
    ple = (p2d, ple_norm, w_ple_gate.astype(BF16), w_ple_proj.astype(BF16))
    return _ffn_call(h, ffn2_norm, ffn2_w_in, ffn2_w_out, 1024, ple=ple)


def kernel(x, p, ffn1_norm, ffn1_w_in, ffn1_w_out, mix_norm, w_in, q_gain, k_gain, conv_w, conv_b, i_bias, f_bias,
           w_up_att, w_up_mlstm, w_out, ffn2_norm, ffn2_w_in, ffn2_w_out, ple_norm, w_ple_gate, w_ple_proj):
    batch, seq, d = x.shape
    depth = p.shape[0]
    assert d == D_MODEL and seq % (ATT_GROUPS[-1][1] * ATT_BLOCK) == 0
    h = x.reshape(batch * seq, d)
    for i in range(depth):
        h = _layer(h, p[i].reshape(batch * seq, PLE_DIM), batch, seq, ffn1_norm[i], ffn1_w_in[i], ffn1_w_out[i],
                   mix_norm[i], w_in[i], q_gain[i], k_gain[i], conv_w[i], conv_b[i], i_bias[i], f_bias[i],
                   w_up_att[i], w_up_mlstm[i], w_out[i], ffn2_norm[i], ffn2_w_in[i], ffn2_w_out[i],
                   ple_norm[i], w_ple_gate[i], w_ple_proj[i])
    return h.reshape(batch, seq, d)
```

```python
import functools

import jax
import jax.numpy as jnp
import numpy as np
from jax import lax
from jax.experimental import pallas as pl
from jax.experimental.pallas import tpu as pltpu

D_MODEL = 1024
PLE_DIM = 256
ATT_GROUPS = ((128, 1), (512, 4), (2048, 16))
N_ATT_GROUPS = 3
ATT_SLOTS = 4
ATT_HEADS = N_ATT_GROUPS * ATT_SLOTS
HEAD_DIM = 128
ATT_WIDTH = ATT_HEADS * HEAD_DIM
ATT_OUT_WIDTH = ATT_SLOTS * HEAD_DIM
ROPE_THETA = 500000.0
ROPE_DIM = HEAD_DIM // 4
ROPE_HALF = ROPE_DIM // 2
ROPE_SPLIT = 64
MLSTM_HEADS = 4
MLSTM_WIDTH = MLSTM_HEADS * HEAD_DIM
MLSTM_CHUNK = 128
CONV_WIDTH = 4
D_FF = 2816
NORM_EPS = 1e-6

FF_CHUNKS = ((0, 512), (512, 512), (1024, 512), (1536, 512), (2048, 512), (2560, 256))
ATT_BLOCK = 128
ATT_BLOCKS_PER_STEP = (8, 2, 1)
ATT_TILES_PER_ITER = 8
ATT_SLABS = ATT_SLOTS + 1
PERM_ROWS = 256
CONV_PAD = 8
V7X_VMEM_LIMIT = 56 * 1024 * 1024

BF16 = jnp.bfloat16
F32 = jnp.float32


def _rms_norm(x, g):
    ms = jnp.mean(x * x, axis=-1, keepdims=True)
    return x * lax.rsqrt(ms + NORM_EPS) * g


def _silu(x):
    return x * jax.nn.sigmoid(x)


def _dot(a, b):
    return jnp.dot(a, b, preferred_element_type=F32)


def _dot_nt(a, b):
    return lax.dot_general(a, b, (((1,), (1,)), ((), ())), preferred_element_type=F32)


def _dot_tn(a, b):
    return lax.dot_general(a, b, (((0,), (0,)), ((), ())), preferred_element_type=F32)


def _resident(shape):
    nd = len(shape)
    return pl.BlockSpec(shape, lambda *_: (0,) * nd, pipeline_mode=pl.Buffered(1))


def _params(semantics):
    return pltpu.CompilerParams(dimension_semantics=semantics, vmem_limit_bytes=V7X_VMEM_LIMIT)


def _swiglu_update(x_ref, g_ref, w_in_ref, w_out_ref, acc_ref):
    u = _rms_norm(x_ref[...], g_ref[...]).astype(BF16)
    for idx, (off, width) in enumerate(FF_CHUNKS):
        a = _dot(u, w_in_ref[:, off:off + width])
        b = _dot(u, w_in_ref[:, D_FF + off:D_FF + off + width])
        part = _dot((_silu(a) * b).astype(BF16), w_out_ref[off:off + width, :])
        if idx == 0:
            acc_ref[...] = part
        else:
            acc_ref[...] += part
    return x_ref[...] + 0.5 * acc_ref[...]


def _ffn_kernel(x_ref, g_ref, w_in_ref, w_out_ref, o_ref, acc_ref):
    o_ref[...] = _swiglu_update(x_ref, g_ref, w_in_ref, w_out_ref, acc_ref)


def _ffn_ple_kernel(x_ref, p_ref, g_ref, w_in_ref, w_out_ref, pg_ref, w_pg_ref, w_pe_ref, o_ref, acc_ref):
    h = _swiglu_update(x_ref, g_ref, w_in_ref, w_out_ref, acc_ref)
    u = _rms_norm(h, pg_ref[...]).astype(BF16)
    gate = jax.nn.sigmoid(_dot(u, w_pg_ref[...]))
    pe = _dot(p_ref[...].astype(BF16), w_pe_ref[...])
    o_ref[...] = h + pe * gate


def _ffn_call(x2d, norm_g, w_in, w_out, tm, ple=None):
    n_rows = x2d.shape[0]
    row = lambda w: pl.BlockSpec((tm, w), lambda i: (i, 0))
    g2 = norm_g.reshape(1, D_MODEL)
    w_in = w_in.astype(BF16)
    w_out = w_out.astype(BF16)
    if ple is None:
        kern = _ffn_kernel
        args = (x2d, g2, w_in, w_out)
        specs = [row(D_MODEL), _resident(g2.shape), _resident(w_in.shape), _resident(w_out.shape)]
        name = "ffn"
    else:
        p2d, ple_g, w_pg, w_pe = ple
        pg2 = ple_g.reshape(1, D_MODEL)
        kern = _ffn_ple_kernel
        args = (x2d, p2d, g2, w_in, w_out, pg2, w_pg, w_pe)
        specs = [row(D_MODEL), row(PLE_DIM), _resident(g2.shape), _resident(w_in.shape),
                 _resident(w_out.shape), _resident(pg2.shape), _resident(w_pg.shape), _resident(w_pe.shape)]
        name = "ffn_ple"
    return pl.pallas_call(
        kern,
        grid=(n_rows // tm,),
        in_specs=specs,
        out_specs=row(D_MODEL),
        out_shape=jax.ShapeDtypeStruct((n_rows, D_MODEL), F32),
        scratch_shapes=[pltpu.VMEM((tm, D_MODEL), F32)],
        compiler_params=_params(("parallel",)),
        name=name,
    )(*args)


def _log_sigmoid(x):
    return jnp.minimum(x, 0.0) - jnp.log1p(jnp.exp(-jnp.abs(x)))


def _in_proj_kernel(h_ref, hp_ref, g_ref, watt_ref, wm_ref, wgt_ref, qg_ref, kg_ref,
                    cos_ref, sin_ref, perm_ref, cw_ref, cb_ref, gbt_ref,
                    q0_ref, k0_ref, v0_ref, q1_ref, k1_ref, v1_ref, q2_ref, k2_ref, v2_ref,
                    mq_ref, mk_ref, mv_ref, gt_ref, xpad_ref, u_ref):
    tm = h_ref.shape[0]
    gnorm = g_ref[...]
    u = _rms_norm(h_ref[...], gnorm).astype(BF16)
    qgain = qg_ref[...]
    kgain = kg_ref[...]
    att_refs = ((q0_ref, k0_ref, v0_ref), (q1_ref, k1_ref, v1_ref), (q2_ref, k2_ref, v2_ref))

    def qk_head(zh, gain, cos, sin, scale):
        y = _rms_norm(zh, gain)
        y = y * cos + pltpu.roll(y, HEAD_DIM // 2, 1) * sin
        if scale != 1.0:
            y = y * scale
        return y.astype(BF16)

    def deinterleaved(grp, dil):
        if dil == 1:
            return u
        pm = perm_ref[grp - 1]
        return jnp.concatenate(
            [_dot(pm, u[kb * PERM_ROWS:(kb + 1) * PERM_ROWS, :]) for kb in range(tm // PERM_ROWS)],
            axis=0).astype(BF16)

    def store_att(o_ref, val, dil):
        if dil == 1:
            o_ref[0, 0] = val
            return
        rows_per_res = PERM_ROWS // dil
        for kb in range(tm // PERM_ROWS):
            for r in range(dil):
                src = kb * PERM_ROWS + r * rows_per_res
                o_ref[0, r, kb * rows_per_res:(kb + 1) * rows_per_res, :] = val[src:src + rows_per_res, :]

    def att_item(grp, dil, part):
        cols = slice(part * ATT_OUT_WIDTH, (part + 1) * ATT_OUT_WIDTH)
        wcols = slice(grp * 3 * ATT_OUT_WIDTH + cols.start, grp * 3 * ATT_OUT_WIDTH + cols.stop)
        matmul = lambda: _dot(u_ref[grp], watt_ref[:, wcols])

        def epilogue(z):
            if part == 2:
                val = z.astype(BF16)
            else:
                gain, scale = ((qgain, HEAD_DIM ** -0.5), (kgain, 1.0))[part]
                cos = cos_ref[grp]
                sin = sin_ref[grp]
                val = jnp.concatenate([qk_head(z[:, j * HEAD_DIM:(j + 1) * HEAD_DIM], gain, cos, sin, scale)
                                       for j in range(ATT_SLOTS)], axis=1)
            store_att(att_refs[grp][part], val, dil)
        return matmul, epilogue

    def conv_item(part, o_ref, scale):
        cols = slice(part * MLSTM_WIDTH, (part + 1) * MLSTM_WIDTH)

        def matmul():
            up = _rms_norm(hp_ref[...], gnorm).astype(BF16)
            return _dot(u_ref[0], wm_ref[:, cols]), _dot(up, wm_ref[:, cols])

        def epilogue(zz):
            z, zp = zz
            zp = jnp.where(pl.program_id(0) > 0, zp, 0.0)
            for c in range(MLSTM_WIDTH // HEAD_DIM):
                slab = part * (MLSTM_WIDTH // HEAD_DIM) + c
                lanes = slice(part * MLSTM_WIDTH + c * HEAD_DIM, part * MLSTM_WIDTH + (c + 1) * HEAD_DIM)
                xpad_ref[slab, 0:CONV_PAD, :] = zp[:, c * HEAD_DIM:(c + 1) * HEAD_DIM]
                xpad_ref[slab, CONV_PAD:CONV_PAD + tm, :] = z[:, c * HEAD_DIM:(c + 1) * HEAD_DIM]
                acc = cb_ref[:, lanes]
                for j in range(CONV_WIDTH):
                    off = CONV_PAD - (CONV_WIDTH - 1) + j
                    acc = acc + xpad_ref[slab, off:off + tm, :] * cw_ref[j:j + 1, lanes]
                y = _silu(acc)
                if scale != 1.0:
                    y = y * scale
                o_ref[:, c * HEAD_DIM:(c + 1) * HEAD_DIM] = y.astype(BF16)
        return matmul, epilogue

    def value_gate_item():
        def matmul():
            return _dot(u_ref[0], wm_ref[:, 2 * MLSTM_WIDTH:]), _dot_nt(wgt_ref[...], u_ref[0])

        def epilogue(zz):
            zv, zt = zz
            mv_ref[...] = zv.astype(BF16)
            zt = zt + gbt_ref[...]
            sub = lax.broadcasted_iota(jnp.int32, zt.shape, 0)
            gt_ref[0] = jnp.where(sub < MLSTM_HEADS, zt, _log_sigmoid(zt))
        return matmul, epilogue

    for grp, (_, dil) in enumerate(ATT_GROUPS):
        u_ref[grp] = deinterleaved(grp, dil)
    light = [conv_item(0, mq_ref, 1.0), conv_item(1, mk_ref, HEAD_DIM ** -0.5), value_gate_item()]
    items = []
    for grp, (_, dil) in enumerate(ATT_GROUPS):
        items += [att_item(grp, dil, 0), light[grp], att_item(grp, dil, 1), att_item(grp, dil, 2)]
    pending = items[0][0]()
    for idx, (_, epilogue) in enumerate(items):
        current = pending
        if idx + 1 < len(items):
            pending = items[idx + 1][0]()
        epilogue(current)


def _rope_lane_perm():
    return (list(range(0, ROPE_HALF)) + list(range(ROPE_DIM, ROPE_DIM + 48))
            + list(range(ROPE_HALF, ROPE_DIM)) + list(range(ROPE_DIM + 48, HEAD_DIM)))


def _deinterleave_rows(t, dil):
    n, w = t.shape
    return t.reshape(n // PERM_ROWS, PERM_ROWS // dil, dil, w).transpose(0, 2, 1, 3).reshape(n, w)


def _rope_tables(seq):
    inv_freq = 1.0 / (ROPE_THETA ** (jnp.arange(ROPE_HALF, dtype=F32) / ROPE_HALF))
    rest = jnp.zeros((HEAD_DIM // 2 - ROPE_HALF,), F32)
    freq = jnp.concatenate([inv_freq, rest, inv_freq, rest])
    sign = jnp.concatenate([-jnp.ones((ROPE_HALF,), F32), rest, jnp.ones((ROPE_HALF,), F32), rest])
    coarse = (jnp.arange(seq // ROPE_SPLIT, dtype=F32) * ROPE_SPLIT).reshape(-1, 1) * freq
    fine = jnp.arange(ROPE_SPLIT, dtype=F32).reshape(-1, 1) * freq
    ca, sa = jnp.cos(coarse)[:, None, :], jnp.sin(coarse)[:, None, :]
    cb, sb = jnp.cos(fine)[None, :, :], jnp.sin(fine)[None, :, :]
    cos_t = (ca * cb - sa * sb).reshape(seq, HEAD_DIM)
    sin_t = ((sa * cb + ca * sb) * sign).reshape(seq, HEAD_DIM)
    cos_g = jnp.stack([_deinterleave_rows(cos_t, d) for _, d in ATT_GROUPS])
    sin_g = jnp.stack([_deinterleave_rows(sin_t, d) for _, d in ATT_GROUPS])
    return cos_g, sin_g


def _in_proj_call(h2d, mix_norm, w_in, q_gain, k_gain, conv_w, conv_b, i_bias, f_bias, batch, seq, tm):
    n_rows = h2d.shape[0]
    perm = np.asarray(_rope_lane_perm(), dtype=np.int32)
    slot_perm = (np.arange(ATT_SLOTS, dtype=np.int32)[:, None] * HEAD_DIM + perm[None, :]).reshape(-1)
    cols = []
    for g in range(N_ATT_GROUPS):
        base = g * ATT_OUT_WIDTH
        cols += [base + slot_perm, ATT_WIDTH + base + slot_perm,
                 2 * ATT_WIDTH + base + np.arange(ATT_OUT_WIDTH, dtype=np.int32)]
    cols.append(3 * ATT_WIDTH + np.arange(3 * MLSTM_WIDTH, dtype=np.int32))
    w_proj = jnp.take(w_in, np.concatenate(cols), axis=1).astype(BF16)
    o = 3 * ATT_WIDTH + 4 * MLSTM_WIDTH
    wgt = w_in[:, o:o + 2 * MLSTM_HEADS].T.astype(BF16)
    gbt = jnp.concatenate([i_bias, f_bias]).astype(F32).reshape(2 * MLSTM_HEADS, 1)
    qg = q_gain[perm].reshape(1, HEAD_DIM)
    kg = k_gain[perm].reshape(1, HEAD_DIM)
    cos_g, sin_g = _rope_tables(seq)
    eye = np.eye(PERM_ROWS, dtype=np.float32)
    perms = jnp.asarray(np.stack([
        eye.reshape(PERM_ROWS // d, d, PERM_ROWS).transpose(1, 0, 2).reshape(PERM_ROWS, PERM_ROWS)
        for _, d in ATT_GROUPS[1:]]), dtype=BF16)
    g2 = mix_norm.reshape(1, D_MODEL)
    cb2 = conv_b.reshape(1, 2 * MLSTM_WIDTH)
    tiles_per_seq = seq // tm

    row = lambda w: pl.BlockSpec((tm, w), lambda t, b: (b * tiles_per_seq + t, 0))
    prev_rows = pl.BlockSpec(
        (CONV_PAD, D_MODEL), lambda t, b: (jnp.maximum((b * tiles_per_seq + t) * (tm // CONV_PAD) - 1, 0), 0))
    pos = pl.BlockSpec((N_ATT_GROUPS, tm, HEAD_DIM), lambda t, b: (0, t, 0))
    w_att_spec = pl.BlockSpec((D_MODEL, 3 * ATT_WIDTH), lambda t, b: (0, 0), pipeline_mode=pl.Buffered(1))
    wm_spec = pl.BlockSpec((D_MODEL, 3 * MLSTM_WIDTH), lambda t, b: (0, ATT_WIDTH // MLSTM_WIDTH),
                           pipeline_mode=pl.Buffered(1))
    in_specs = [row(D_MODEL), prev_rows, _resident(g2.shape), w_att_spec, wm_spec, _resident(wgt.shape), _resident(qg.shape), _resident(kg.shape), pos, pos,
                _resident(perms.shape), _resident(conv_w.shape), _resident(cb2.shape), _resident(gbt.shape)]
    out_specs, out_shape = [], []
    for _, d in ATT_GROUPS:
        spec = pl.BlockSpec((1, d, tm // d, ATT_OUT_WIDTH), lambda t, b: (b, 0, t, 0))
        out_specs += [spec] * 3
        out_shape += [jax.ShapeDtypeStruct((batch, d, seq // d, ATT_OUT_WIDTH), BF16)] * 3
    out_specs += [row(MLSTM_WIDTH)] * 3 + [pl.BlockSpec((1, 2 * MLSTM_HEADS, tm), lambda t, b: (b, 0, t))]
    out_shape += [jax.ShapeDtypeStruct((n_rows, MLSTM_WIDTH), BF16)] * 3 + [
        jax.ShapeDtypeStruct((batch, 2 * MLSTM_HEADS, seq), F32)]
    outs = pl.pallas_call(
        _in_proj_kernel,
        grid=(tiles_per_seq, batch),
        in_specs=in_specs,
        out_specs=out_specs,
        out_shape=out_shape,
        scratch_shapes=[pltpu.VMEM((2 * MLSTM_WIDTH // HEAD_DIM, tm + CONV_PAD, HEAD_DIM), F32),
                        pltpu.VMEM((N_ATT_GROUPS, tm, D_MODEL), BF16)],
        compiler_params=_params(("parallel", "parallel")),
        name="in_proj",
    )(h2d, h2d, g2, w_proj, w_proj, wgt, qg, kg, cos_g, sin_g, perms, conv_w, cb2, gbt)
    qkv = [tuple(outs[3 * g:3 * g + 3]) for g in range(N_ATT_GROUPS)]
    return qkv, outs[9:]


def _attention_kernel(dil, nblk, q_ref, kc_ref, vc_ref, o_ref, kp_ref, vp_ref):
    slot_in = pl.program_id(1) % 2
    slot_out = 1 - slot_in

    @pl.when(pl.program_id(1) == 0)
    def _():
        kp_ref[0] = jnp.zeros(kp_ref.shape[1:], kp_ref.dtype)
        vp_ref[0] = jnp.zeros(vp_ref.shape[1:], vp_ref.dtype)

    qi = lax.broadcasted_iota(jnp.int32, (ATT_BLOCK, 2 * ATT_BLOCK), 0)
    kj = lax.broadcasted_iota(jnp.int32, (ATT_BLOCK, 2 * ATT_BLOCK), 1)
    band = (kj >= qi) & (kj <= qi + ATT_BLOCK)
    bias = jnp.where(band, 0.0, -jnp.inf)
    first_key = jnp.where(pl.program_id(1) > 0, 0, ATT_BLOCK)
    bias_head = jnp.where(kj >= first_key, bias, -jnp.inf)
    lane = lax.broadcasted_iota(jnp.int32, (ATT_BLOCK, HEAD_DIM), 1)

    heads = [slice(j * HEAD_DIM, (j + 1) * HEAD_DIM) for j in range(ATT_SLOTS)]

    def scores(r, blk):
        q = q_ref[0, r, blk * ATT_BLOCK:(blk + 1) * ATT_BLOCK, :]
        if blk == 0:
            k = jnp.concatenate([kp_ref[slot_in, r], kc_ref[0, r, 0:ATT_BLOCK, :]], axis=0)
            tile_bias = bias_head
        else:
            k = kc_ref[0, r, (blk - 1) * ATT_BLOCK:(blk + 1) * ATT_BLOCK, :]
            tile_bias = bias
        return [_dot_nt(q[:, sl], k[:, sl]) + tile_bias for sl in heads]

    def softmax(s_list):
        out = []
        for s in s_list:
            m = jnp.max(s, axis=-1, keepdims=True)
            e = jnp.exp(s - m)
            den = jnp.sum(e, axis=-1, keepdims=True)
            out.append((e.astype(BF16), den, m))
        return out

    def weighted_values(r, blk, p_list):
        if blk == 0:
            v = jnp.concatenate([vp_ref[slot_in, r], vc_ref[0, r, 0:ATT_BLOCK, :]], axis=0)
        else:
            v = vc_ref[0, r, (blk - 1) * ATT_BLOCK:(blk + 1) * ATT_BLOCK, :]
        return [_dot(p, v[:, sl]) for (p, _, _), sl in zip(p_list, heads)]

    def store(r, blk, p_list, pv_list):
        if dil == 1:
            rows = pl.ds(blk * ATT_BLOCK, ATT_BLOCK)
        else:
            rows = pl.ds(blk * ATT_BLOCK * dil + r, ATT_BLOCK, stride=dil)
        stats = jnp.ones((ATT_BLOCK, HEAD_DIM), F32)
        for j, ((_, den, m), pv) in enumerate(zip(p_list, pv_list)):
            o_ref[j, rows, :] = pv
            stats = jnp.where(lane == j, m, jnp.where(lane == ATT_SLOTS + j, den, stats))
        o_ref[ATT_SLOTS, rows, :] = stats

    def run_tiles(tiles):
        s_next = scores(*tiles[0])
        for idx, (r, blk) in enumerate(tiles):
            s_cur = s_next
            if idx + 1 < len(tiles):
                s_next = scores(*tiles[idx + 1])
            p_list = softmax(s_cur)
            store(r, blk, p_list, weighted_values(r, blk, p_list))

    res_per_iter = ATT_TILES_PER_ITER // nblk
    if dil == res_per_iter:
        run_tiles([(r, blk) for r in range(dil) for blk in range(nblk)])
    else:
        def body(it, carry):
            run_tiles([(it * res_per_iter + rr, blk) for rr in range(res_per_iter) for blk in range(nblk)])
            return carry
        lax.fori_loop(0, dil // res_per_iter, body, 0)

    last = slice((nblk - 1) * ATT_BLOCK, nblk * ATT_BLOCK)
    kp_ref[slot_out] = kc_ref[0, :, last, :]
    vp_ref[slot_out] = vc_ref[0, :, last, :]


def _attention_call(q, k, v, group, dilation, nblk, batch, seq):
    n = seq // dilation
    steps = n // (nblk * ATT_BLOCK)
    span = nblk * ATT_BLOCK * dilation
    cur = pl.BlockSpec((1, dilation, nblk * ATT_BLOCK, ATT_OUT_WIDTH), lambda b, i: (b, 0, i, 0))
    carry = pltpu.VMEM((2, dilation, ATT_BLOCK, ATT_OUT_WIDTH), BF16)
    return pl.pallas_call(
        functools.partial(_attention_kernel, dilation, nblk),
        grid=(batch, steps),
        in_specs=[cur, cur, cur],
        out_specs=pl.BlockSpec((ATT_SLABS, span, HEAD_DIM), lambda b, i: (0, b * steps + i, 0)),
        out_shape=jax.ShapeDtypeStruct((ATT_SLABS, batch * seq, HEAD_DIM), F32),
        scratch_shapes=[carry, carry],
        compiler_params=_params(("arbitrary", "arbitrary")),
        name=f"attention_g{group}",
    )(q, k, v)


def _mlstm_kernel(mq_ref, mk_ref, mv_ref, gt_ref, o_ref, state_ref, m_ref):
    L = MLSTM_CHUNK
    H = MLSTM_HEADS

    @pl.when(pl.program_id(0) == 0)
    def _():
        state_ref[...] = jnp.zeros_like(state_ref)
        m_ref[...] = jnp.zeros_like(m_ref)

    ti = lax.broadcasted_iota(jnp.int32, (L, L), 0)
    si = lax.broadcasted_iota(jnp.int32, (L, L), 1)
    causal = ti >= si
    tri = causal.astype(F32)
    ones = jnp.ones((L, HEAD_DIM), BF16)

    nbatch = mq_ref.shape[0]
    streams = [divmod(bh, H) for bh in range(nbatch * H)]
    lanes = lambda hd: slice(hd * HEAD_DIM, (hd + 1) * HEAD_DIM)

    gts = [gt_ref[bi] for bi in range(nbatch)]
    b_rows = [lax.dot_general(gt, tri, (((1,), (1,)), ((), ())), precision=lax.Precision.HIGHEST,
                              preferred_element_type=F32) for gt in gts]

    qs = [mq_ref[bi, :, lanes(hd)] for bi, hd in streams]
    ks = [mk_ref[bi, :, lanes(hd)] for bi, hd in streams]
    vexts = [jnp.concatenate([mv_ref[bi, :, lanes(hd)], ones], axis=1) for bi, hd in streams]
    states = [state_ref[bh] for bh in range(len(streams))]
    m_prevs = [m_ref[bh, 0:1, 0:1] for bh in range(len(streams))]
    kts = [k.T for k in ks]
    qk = [_dot(q, kt) for q, kt in zip(qs, kts)]
    q_state = [_dot(q, st.astype(BF16)) for q, st in zip(qs, states)]

    c_rows, upd, decays, m_news = [], [], [], []
    for bh, (bi, hd) in enumerate(streams):
        b_row = b_rows[bi][H + hd:H + hd + 1, :]
        c_row = gts[bi][hd:hd + 1, :] - b_row
        b_last = b_row[:, L - 1:L]
        g_row = b_last + c_row
        m_new = jnp.maximum(b_last + m_prevs[bh], jnp.max(g_row, axis=-1, keepdims=True))
        wk_row = jnp.exp(g_row - m_new)
        c_rows.append(c_row)
        decays.append(jnp.exp(b_last + m_prevs[bh] - m_new))
        m_news.append(m_new)
        upd.append(_dot((kts[bh] * wk_row).astype(BF16), vexts[bh]))

    s_list, a_list, floor_list = [], [], []
    for bh, (bi, hd) in enumerate(streams):
        cm = jnp.where(causal, c_rows[bh], -jnp.inf)
        n_t = jnp.maximum(m_prevs[bh], jnp.max(cm, axis=-1, keepdims=True))
        b_t = jnp.sum(jnp.where(causal, gts[bi][H + hd:H + hd + 1, :], 0.0), axis=-1, keepdims=True)
        s_list.append((qk[bh] * jnp.exp(cm - n_t)).astype(BF16))
        a_list.append(jnp.exp(m_prevs[bh] - n_t))
        floor_list.append(jnp.exp(-(b_t + n_t)))

    sv = [_dot(s, vext) for s, vext in zip(s_list, vexts)]
    for bh, (bi, hd) in enumerate(streams):
        ext = a_list[bh] * q_state[bh] + sv[bh]
        num = ext[:, :HEAD_DIM]
        den = ext[:, HEAD_DIM:]
        o_ref[bi, :, lanes(hd)] = (num / jnp.maximum(jnp.abs(den), floor_list[bh])).astype(o_ref.dtype)
        state_ref[bh] = decays[bh] * states[bh] + upd[bh]
        m_ref[bh] = jnp.broadcast_to(m_news[bh], m_ref.shape[1:])


def _mlstm_call(mq, mk, mv, gt, batch, seq):
    nc = seq // MLSTM_CHUNK
    W = MLSTM_WIDTH
    blk = lambda w: pl.BlockSpec((batch, MLSTM_CHUNK, w), lambda c: (0, c, 0))
    out = pl.pallas_call(
        _mlstm_kernel,
        grid=(nc,),
        in_specs=[blk(W), blk(W), blk(W),
                  pl.BlockSpec((batch, 2 * MLSTM_HEADS, MLSTM_CHUNK), lambda c: (0, 0, c))],
        out_specs=blk(W),
        out_shape=jax.ShapeDtypeStruct((batch, seq, W), BF16),
        scratch_shapes=[pltpu.VMEM((batch * MLSTM_HEADS, HEAD_DIM, 2 * HEAD_DIM), F32),
                        pltpu.VMEM((batch * MLSTM_HEADS, 8, 128), F32)],
        compiler_params=_params(("arbitrary",)),
        name="mlstm",
    )(mq.reshape(batch, seq, W), mk.reshape(batch, seq, W), mv.reshape(batch, seq, W), gt)
    return out.reshape(batch * seq, W)


def _mix_out_kernel(h_ref, a0_ref, a1_ref, a2_ref, ml_ref, g_ref, wg_ref, wua_ref, wub_ref, wo_ref, out_ref):
    h = h_ref[...]
    u = _rms_norm(h, g_ref[...]).astype(BF16)
    zg = _dot(u, wg_ref[...])
    mo = zg[:, :MLSTM_WIDTH]
    ga = zg[:, MLSTM_WIDTH:MLSTM_WIDTH + D_MODEL]
    gb = zg[:, MLSTM_WIDTH + D_MODEL:]

    a_refs = (a0_ref, a1_ref, a2_ref)
    stats = [a[ATT_SLOTS] for a in a_refs]
    mx = jnp.maximum(jnp.maximum(stats[0], stats[1]), stats[2])
    ws = [jnp.exp(s - mx) for s in stats]
    dens = [pltpu.roll(s, HEAD_DIM - ATT_SLOTS, 1) for s in stats]
    inv = 1.0 / (ws[0] * dens[0] + ws[1] * dens[1] + ws[2] * dens[2])
    alphas = [w * inv for w in ws]
    merged_heads = []
    for j in range(ATT_SLOTS):
        acc = alphas[0][:, j:j + 1] * a_refs[0][j]
        acc = acc + alphas[1][:, j:j + 1] * a_refs[1][j]
        acc = acc + alphas[2][:, j:j + 1] * a_refs[2][j]
        merged_heads.append(acc.astype(BF16))
    att = jnp.concatenate(merged_heads, axis=1)
    y_att = _dot(att, wua_ref[...])
    y_ml = _dot((jax.nn.sigmoid(mo) * ml_ref[...]).astype(BF16), wub_ref[...])
    merged = jax.nn.sigmoid(ga) * y_att + jax.nn.sigmoid(gb) * y_ml
    out_ref[...] = h + _dot(merged.astype(BF16), wo_ref[...])


def _mix_out_call(h2d, att, ml, mix_norm, w_in, w_up_att, w_up_mlstm, w_out, tm):
    n_rows = h2d.shape[0]
    g2 = mix_norm.reshape(1, D_MODEL)
    o_mo = 3 * ATT_WIDTH + 3 * MLSTM_WIDTH
    o_g = o_mo + MLSTM_WIDTH + 2 * MLSTM_HEADS
    gate_cols = np.concatenate([o_mo + np.arange(MLSTM_WIDTH, dtype=np.int32),
                                o_g + np.arange(2 * D_MODEL, dtype=np.int32)])
    wg = jnp.take(w_in, gate_cols, axis=1).astype(BF16)
    wua = w_up_att.astype(BF16)
    wub = w_up_mlstm.astype(BF16)
    wo = w_out.astype(BF16)
    row = lambda w: pl.BlockSpec((tm, w), lambda i: (i, 0))
    slabs = pl.BlockSpec((ATT_SLABS, tm, HEAD_DIM), lambda i: (0, i, 0))
    return pl.pallas_call(
        _mix_out_kernel,
        grid=(n_rows // tm,),
        in_specs=[row(D_MODEL), slabs, slabs, slabs, row(MLSTM_WIDTH),
                  _resident(g2.shape), _resident(wg.shape), _resident(wua.shape), _resident(wub.shape),
                  _resident(wo.shape)],
        out_specs=row(D_MODEL),
        out_shape=jax.ShapeDtypeStruct((n_rows, D_MODEL), F32),
        compiler_params=_params(("parallel",)),
        name="mix_out",
    )(h2d, *att, ml, g2, wg, wua, wub, wo)


def _layer(x2d, p2d, batch, seq, ffn1_norm, ffn1_w_in, ffn1_w_out, mix_norm, w_in, q_gain, k_gain, conv_w, conv_b,
           i_bias, f_bias, w_up_att, w_up_mlstm, w_out, ffn2_norm, ffn2_w_in, ffn2_w_out, ple_norm, w_ple_gate,
           w_ple_proj):
    tm = 512
    h = _ffn_call(x2d, ffn1_norm, ffn1_w_in, ffn1_w_out, tm)
    qkv, (mq, mk, mv, gt) = _in_proj_call(h, mix_norm, w_in, q_gain, k_gain, conv_w, conv_b, i_bias, f_bias,
                                              batch, seq, 256)
    att = []
    for g, (window, dilation) in enumerate(ATT_GROUPS):
        assert window // dilation == ATT_BLOCK
        att.append(_attention_call(*qkv[g], g, dilation, ATT_BLOCKS_PER_STEP[g], batch, seq))
    ml = _mlstm_call(mq, mk, mv, gt, batch, seq)
    h = _mix_out_call(h, att, ml, mix_norm, w_in, w_up_att, w_up_mlstm, w_out, tm)
    ple = (p2d, ple_norm, w_ple_gate.astype(BF16), w_ple_proj.astype(BF16))
    return _ffn_call(h, ffn2_norm, ffn2_w_in, ffn2_w_out, 1024, ple=ple)


def kernel(x, p, ffn1_norm, ffn1_w_in, ffn1_w_out, mix_norm, w_in, q_gain, k_gain, conv_w, conv_b, i_bias, f_bias,
           w_up_att, w_up_mlstm, w_out, ffn2_norm, ffn2_w_in, ffn2_w_out, ple_norm, w_ple_gate, w_ple_proj):
    batch, seq, d = x.shape
    depth = p.shape[0]
    assert d == D_MODEL and seq % (ATT_GROUPS[-1][1] * ATT_BLOCK) == 0
    h = x.reshape(batch * seq, d)
    for i in range(depth):
        h = _layer(h, p[i].reshape(batch * seq, PLE_DIM), batch, seq, ffn1_norm[i], ffn1_w_in[i], ffn1_w_out[i],
                   mix_norm[i], w_in[i], q_gain[i], k_gain[i], conv_w[i], conv_b[i], i_bias[i], f_bias[i],
                   w_up_att[i], w_up_mlstm[i], w_out[i], ffn2_norm[i], ffn2_w_in[i], ffn2_w_out[i],
                   ple_norm[i], w_ple_gate[i], w_ple_proj[i])
    return h.reshape(batch, seq, d)
```

```python
import functools

import jax
import jax.numpy as jnp
import numpy as np
from jax import lax
from jax.experimental import pallas as pl
from jax.experimental.pallas import tpu as pltpu

D_MODEL = 1024
PLE_DIM = 256
ATT_GROUPS = ((128, 1), (512, 4), (2048, 16))
N_ATT_GROUPS = 3
ATT_SLOTS = 4
ATT_HEADS = N_ATT_GROUPS * ATT_SLOTS
HEAD_DIM = 128
ATT_WIDTH = ATT_HEADS * HEAD_DIM
ATT_OUT_WIDTH = ATT_SLOTS * HEAD_DIM
ROPE_THETA = 500000.0
ROPE_DIM = HEAD_DIM // 4
ROPE_HALF = ROPE_DIM // 2
ROPE_SPLIT = 64
MLSTM_HEADS = 4
MLSTM_WIDTH = MLSTM_HEADS * HEAD_DIM
MLSTM_CHUNK = 128
CONV_WIDTH = 4
D_FF = 2816
NORM_EPS = 1e-6

FF_CHUNKS = ((0, 512), (512, 512), (1024, 512), (1536, 512), (2048, 512), (2560, 256))
ATT_BLOCK = 128
ATT_BLOCKS_PER_STEP = (8, 2, 1)
ATT_TILES_PER_ITER = 8
ATT_SLABS = ATT_SLOTS + 1
PERM_ROWS = 256
CONV_PAD = 8
V7X_VMEM_LIMIT = 56 * 1024 * 1024

BF16 = jnp.bfloat16
F32 = jnp.float32


def _rms_norm(x, g):
    ms = jnp.mean(x * x, axis=-1, keepdims=True)
    return x * lax.rsqrt(ms + NORM_EPS) * g


def _silu(x):
    return x * jax.nn.sigmoid(x)


def _dot(a, b):
    return jnp.dot(a, b, preferred_element_type=F32)


def _dot_nt(a, b):
    return lax.dot_general(a, b, (((1,), (1,)), ((), ())), preferred_element_type=F32)


def _dot_tn(a, b):
    return lax.dot_general(a, b, (((0,), (0,)), ((), ())), preferred_element_type=F32)


def _resident(shape):
    nd = len(shape)
    return pl.BlockSpec(shape, lambda *_: (0,) * nd, pipeline_mode=pl.Buffered(1))


def _params(semantics):
    return pltpu.CompilerParams(dimension_semantics=semantics, vmem_limit_bytes=V7X_VMEM_LIMIT)


def _swiglu_update(x_ref, g_ref, w_in_ref, w_out_ref, acc_ref):
    u = _rms_norm(x_ref[...], g_ref[...]).astype(BF16)
    for idx, (off, width) in enumerate(FF_CHUNKS):
        a = _dot(u, w_in_ref[:, off:off + width])
        b = _dot(u, w_in_ref[:, D_FF + off:D_FF + off + width])
        part = _dot((_silu(a) * b).astype(BF16), w_out_ref[off:off + width, :])
        if idx == 0:
            acc_ref[...] = part
        else:
            acc_ref[...] += part
    return x_ref[...] + 0.5 * acc_ref[...]


def _ffn_kernel(x_ref, g_ref, w_in_ref, w_out_ref, o_ref, acc_ref):
    o_ref[...] = _swiglu_update(x_ref, g_ref, w_in_ref, w_out_ref, acc_ref)


def _ffn_ple_kernel(x_ref, p_ref, g_ref, w_in_ref, w_out_ref, pg_ref, w_pg_ref, w_pe_ref, o_ref, acc_ref):
    h = _swiglu_update(x_ref, g_ref, w_in_ref, w_out_ref, acc_ref)
    u = _rms_norm(h, pg_ref[...]).astype(BF16)
    gate = jax.nn.sigmoid(_dot(u, w_pg_ref[...]))
    pe = _dot(p_ref[...].astype(BF16), w_pe_ref[...])
    o_ref[...] = h + pe * gate


def _ffn_call(x2d, norm_g, w_in, w_out, tm, ple=None):
    n_rows = x2d.shape[0]
    row = lambda w: pl.BlockSpec((tm, w), lambda i: (i, 0))
    g2 = norm_g.reshape(1, D_MODEL)
    w_in = w_in.astype(BF16)
    w_out = w_out.astype(BF16)
    if ple is None:
        kern = _ffn_kernel
        args = (x2d, g2, w_in, w_out)
        specs = [row(D_MODEL), _resident(g2.shape), _resident(w_in.shape), _resident(w_out.shape)]
        name = "ffn"
    else:
        p2d, ple_g, w_pg, w_pe = ple
        pg2 = ple_g.reshape(1, D_MODEL)
        kern = _ffn_ple_kernel
        args = (x2d, p2d, g2, w_in, w_out, pg2, w_pg, w_pe)
        specs = [row(D_MODEL), row(PLE_DIM), _resident(g2.shape), _resident(w_in.shape),
                 _resident(w_out.shape), _resident(pg2.shape), _resident(w_pg.shape), _resident(w_pe.shape)]
        name = "ffn_ple"
    return pl.pallas_call(
        kern,
        grid=(n_rows // tm,),
        in_specs=specs,
        out_specs=row(D_MODEL),
        out_shape=jax.ShapeDtypeStruct((n_rows, D_MODEL), F32),
        scratch_shapes=[pltpu.VMEM((tm, D_MODEL), F32)],
        compiler_params=_params(("parallel",)),
        name=name,
    )(*args)


def _log_sigmoid(x):
    return jnp.minimum(x, 0.0) - jnp.log1p(jnp.exp(-jnp.abs(x)))


def _in_proj_kernel(h_ref, hp_ref, g_ref, watt_ref, wm_ref, wgt_ref, qg_ref, kg_ref,
                    cos_ref, sin_ref, perm_ref, cw_ref, cb_ref, gbt_ref,
                    q0_ref, k0_ref, v0_ref, q1_ref, k1_ref, v1_ref, q2_ref, k2_ref, v2_ref,
                    mq_ref, mk_ref, mv_ref, gt_ref, xpad_ref, u_ref):
    tm = h_ref.shape[0]
    gnorm = g_ref[...]
    u = _rms_norm(h_ref[...], gnorm).astype(BF16)
    qgain = qg_ref[...]
    kgain = kg_ref[...]
    att_refs = ((q0_ref, k0_ref, v0_ref), (q1_ref, k1_ref, v1_ref), (q2_ref, k2_ref, v2_ref))

    def qk_head(zh, gain, cos, sin, scale):
        y = _rms_norm(zh, gain)
        y = y * cos + pltpu.roll(y, HEAD_DIM // 2, 1) * sin
        if scale != 1.0:
            y = y * scale
        return y.astype(BF16)

    def deinterleaved(grp, dil):
        if dil == 1:
            return u
        pm = perm_ref[grp - 1]
        return jnp.concatenate(
            [_dot(pm, u[kb * PERM_ROWS:(kb + 1) * PERM_ROWS, :]) for kb in range(tm // PERM_ROWS)],
            axis=0).astype(BF16)

    def store_att(o_ref, val, dil):
        if dil == 1:
            o_ref[0, 0] = val
            return
        rows_per_res = PERM_ROWS // dil
        for kb in range(tm // PERM_ROWS):
            for r in range(dil):
                src = kb * PERM_ROWS + r * rows_per_res
                o_ref[0, r, kb * rows_per_res:(kb + 1) * rows_per_res, :] = val[src:src + rows_per_res, :]

    def att_item(grp, dil, part):
        cols = slice(part * ATT_OUT_WIDTH, (part + 1) * ATT_OUT_WIDTH)
        matmul = lambda: _dot(u_ref[grp], watt_ref[grp, :, cols])

        def epilogue(z):
            if part == 2:
                val = z.astype(BF16)
            else:
                gain, scale = ((qgain, HEAD_DIM ** -0.5), (kgain, 1.0))[part]
                cos = cos_ref[grp]
                sin = sin_ref[grp]
                val = jnp.concatenate([qk_head(z[:, j * HEAD_DIM:(j + 1) * HEAD_DIM], gain, cos, sin, scale)
                                       for j in range(ATT_SLOTS)], axis=1)
            store_att(att_refs[grp][part], val, dil)
        return matmul, epilogue

    def conv_item(part, o_ref, scale):
        cols = slice(part * MLSTM_WIDTH, (part + 1) * MLSTM_WIDTH)

        def matmul():
            up = _rms_norm(hp_ref[...], gnorm).astype(BF16)
            return _dot(u_ref[0], wm_ref[:, cols]), _dot(up, wm_ref[:, cols])

        def epilogue(zz):
            z, zp = zz
            zp = jnp.where(pl.program_id(0) > 0, zp, 0.0)
            for c in range(MLSTM_WIDTH // HEAD_DIM):
                slab = part * (MLSTM_WIDTH // HEAD_DIM) + c
                lanes = slice(part * MLSTM_WIDTH + c * HEAD_DIM, part * MLSTM_WIDTH + (c + 1) * HEAD_DIM)
                xpad_ref[slab, 0:CONV_PAD, :] = zp[:, c * HEAD_DIM:(c + 1) * HEAD_DIM]
                xpad_ref[slab, CONV_PAD:CONV_PAD + tm, :] = z[:, c * HEAD_DIM:(c + 1) * HEAD_DIM]
                acc = cb_ref[:, lanes]
                for j in range(CONV_WIDTH):
                    off = CONV_PAD - (CONV_WIDTH - 1) + j
                    acc = acc + xpad_ref[slab, off:off + tm, :] * cw_ref[j:j + 1, lanes]
                y = _silu(acc)
                if scale != 1.0:
                    y = y * scale
                o_ref[:, c * HEAD_DIM:(c + 1) * HEAD_DIM] = y.astype(BF16)
        return matmul, epilogue

    def value_gate_item():
        def matmul():
            return _dot(u_ref[0], wm_ref[:, 2 * MLSTM_WIDTH:]), _dot_nt(wgt_ref[...], u_ref[0])

        def epilogue(zz):
            zv, zt = zz
            mv_ref[...] = zv.astype(BF16)
            zt = zt + gbt_ref[...]
            sub = lax.broadcasted_iota(jnp.int32, zt.shape, 0)
            gt_ref[0] = jnp.where(sub < MLSTM_HEADS, zt, _log_sigmoid(zt))
        return matmul, epilogue

    for grp, (_, dil) in enumerate(ATT_GROUPS):
        u_ref[grp] = deinterleaved(grp, dil)
    light = [conv_item(0, mq_ref, 1.0), conv_item(1, mk_ref, HEAD_DIM ** -0.5), value_gate_item()]
    items = []
    for grp, (_, dil) in enumerate(ATT_GROUPS):
        items += [att_item(grp, dil, 0), att_item(grp, dil, 2), att_item(grp, dil, 1), light[grp]]
    pending = items[0][0]()
    for idx, (_, epilogue) in enumerate(items):
        current = pending
        if idx + 1 < len(items):
            pending = items[idx + 1][0]()
        epilogue(current)


def _rope_lane_perm():
    return (list(range(0, ROPE_HALF)) + list(range(ROPE_DIM, ROPE_DIM + 48))
            + list(range(ROPE_HALF, ROPE_DIM)) + list(range(ROPE_DIM + 48, HEAD_DIM)))


def _deinterleave_rows(t, dil):
    n, w = t.shape
    return t.reshape(n // PERM_ROWS, PERM_ROWS // dil, dil, w).transpose(0, 2, 1, 3).reshape(n, w)


def _rope_tables(seq):
    inv_freq = 1.0 / (ROPE_THETA ** (jnp.arange(ROPE_HALF, dtype=F32) / ROPE_HALF))
    rest = jnp.zeros((HEAD_DIM // 2 - ROPE_HALF,), F32)
    freq = jnp.concatenate([inv_freq, rest, inv_freq, rest])
    sign = jnp.concatenate([-jnp.ones((ROPE_HALF,), F32), rest, jnp.ones((ROPE_HALF,), F32), rest])
    coarse = (jnp.arange(seq // ROPE_SPLIT, dtype=F32) * ROPE_SPLIT).reshape(-1, 1) * freq
    fine = jnp.arange(ROPE_SPLIT, dtype=F32).reshape(-1, 1) * freq
    ca, sa = jnp.cos(coarse)[:, None, :], jnp.sin(coarse)[:, None, :]
    cb, sb = jnp.cos(fine)[None, :, :], jnp.sin(fine)[None, :, :]
    cos_t = (ca * cb - sa * sb).reshape(seq, HEAD_DIM)
    sin_t = ((sa * cb + ca * sb) * sign).reshape(seq, HEAD_DIM)
    cos_g = jnp.stack([_deinterleave_rows(cos_t, d) for _, d in ATT_GROUPS])
    sin_g = jnp.stack([_deinterleave_rows(sin_t, d) for _, d in ATT_GROUPS])
    return cos_g, sin_g


def _in_proj_call(h2d, mix_norm, w_in, q_gain, k_gain, conv_w, conv_b, i_bias, f_bias, batch, seq, tm):
    n_rows = h2d.shape[0]
    perm = np.asarray(_rope_lane_perm(), dtype=np.int32)
    slot_perm = (np.arange(ATT_SLOTS, dtype=np.int32)[:, None] * HEAD_DIM + perm[None, :]).reshape(-1)
    wq, wk, wv = (w_in[:, i * ATT_WIDTH:(i + 1) * ATT_WIDTH] for i in range(3))
    w_att = []
    for g in range(N_ATT_GROUPS):
        cols = slice(g * ATT_OUT_WIDTH, (g + 1) * ATT_OUT_WIDTH)
        w_att.append(jnp.concatenate([wq[:, cols][:, slot_perm], wk[:, cols][:, slot_perm], wv[:, cols]], axis=1))
    w_att = jnp.stack(w_att).astype(BF16)
    o = 3 * ATT_WIDTH
    wm = w_in[:, o:o + 3 * MLSTM_WIDTH].astype(BF16)
    o += 4 * MLSTM_WIDTH
    wgt = w_in[:, o:o + 2 * MLSTM_HEADS].T.astype(BF16)
    gbt = jnp.concatenate([i_bias, f_bias]).astype(F32).reshape(2 * MLSTM_HEADS, 1)
    qg = q_gain[perm].reshape(1, HEAD_DIM)
    kg = k_gain[perm].reshape(1, HEAD_DIM)
    cos_g, sin_g = _rope_tables(seq)
    eye = np.eye(PERM_ROWS, dtype=np.float32)
    perms = jnp.asarray(np.stack([
        eye.reshape(PERM_ROWS // d, d, PERM_ROWS).transpose(1, 0, 2).reshape(PERM_ROWS, PERM_ROWS)
        for _, d in ATT_GROUPS[1:]]), dtype=BF16)
    g2 = mix_norm.reshape(1, D_MODEL)
    cb2 = conv_b.reshape(1, 2 * MLSTM_WIDTH)
    tiles_per_seq = seq // tm

    row = lambda w: pl.BlockSpec((tm, w), lambda t, b: (b * tiles_per_seq + t, 0))
    prev_rows = pl.BlockSpec(
        (CONV_PAD, D_MODEL), lambda t, b: (jnp.maximum((b * tiles_per_seq + t) * (tm // CONV_PAD) - 1, 0), 0))
    pos = pl.BlockSpec((N_ATT_GROUPS, tm, HEAD_DIM), lambda t, b: (0, t, 0))
    in_specs = [row(D_MODEL), prev_rows, _resident(g2.shape), _resident(w_att.shape), _resident(wm.shape),
                _resident(wgt.shape), _resident(qg.shape), _resident(kg.shape), pos, pos,
                _resident(perms.shape), _resident(conv_w.shape), _resident(cb2.shape), _resident(gbt.shape)]
    out_specs, out_shape = [], []
    for _, d in ATT_GROUPS:
        spec = pl.BlockSpec((1, d, tm // d, ATT_OUT_WIDTH), lambda t, b: (b, 0, t, 0))
        out_specs += [spec] * 3
        out_shape += [jax.ShapeDtypeStruct((batch, d, seq // d, ATT_OUT_WIDTH), BF16)] * 3
    out_specs += [row(MLSTM_WIDTH)] * 3 + [pl.BlockSpec((1, 2 * MLSTM_HEADS, tm), lambda t, b: (b, 0, t))]
    out_shape += [jax.ShapeDtypeStruct((n_rows, MLSTM_WIDTH), BF16)] * 3 + [
        jax.ShapeDtypeStruct((batch, 2 * MLSTM_HEADS, seq), F32)]
    outs = pl.pallas_call(
        _in_proj_kernel,
        grid=(tiles_per_seq, batch),
        in_specs=in_specs,
        out_specs=out_specs,
        out_shape=out_shape,
        scratch_shapes=[pltpu.VMEM((2 * MLSTM_WIDTH // HEAD_DIM, tm + CONV_PAD, HEAD_DIM), F32),
                        pltpu.VMEM((N_ATT_GROUPS, tm, D_MODEL), BF16)],
        compiler_params=_params(("parallel", "parallel")),
        name="in_proj",
    )(h2d, h2d, g2, w_att, wm, wgt, qg, kg, cos_g, sin_g, perms, conv_w, cb2, gbt)
    qkv = [tuple(outs[3 * g:3 * g + 3]) for g in range(N_ATT_GROUPS)]
    return qkv, outs[9:]


def _attention_kernel(dil, nblk, q_ref, kc_ref, vc_ref, o_ref, kp_ref, vp_ref):
    slot_in = pl.program_id(1) % 2
    slot_out = 1 - slot_in

    @pl.when(pl.program_id(1) == 0)
    def _():
        kp_ref[0] = jnp.zeros(kp_ref.shape[1:], kp_ref.dtype)
        vp_ref[0] = jnp.zeros(vp_ref.shape[1:], vp_ref.dtype)

    qi = lax.broadcasted_iota(jnp.int32, (ATT_BLOCK, 2 * ATT_BLOCK), 0)
    kj = lax.broadcasted_iota(jnp.int32, (ATT_BLOCK, 2 * ATT_BLOCK), 1)
    band = (kj >= qi) & (kj <= qi + ATT_BLOCK)
    bias = jnp.where(band, 0.0, -jnp.inf)
    first_key = jnp.where(pl.program_id(1) > 0, 0, ATT_BLOCK)
    bias_head = jnp.where(kj >= first_key, bias, -jnp.inf)
    lane = lax.broadcasted_iota(jnp.int32, (ATT_BLOCK, HEAD_DIM), 1)

    heads = [slice(j * HEAD_DIM, (j + 1) * HEAD_DIM) for j in range(ATT_SLOTS)]

    def scores(r, blk):
        q = q_ref[0, r, blk * ATT_BLOCK:(blk + 1) * ATT_BLOCK, :]
        if blk == 0:
            k = jnp.concatenate([kp_ref[slot_in, r], kc_ref[0, r, 0:ATT_BLOCK, :]], axis=0)
            tile_bias = bias_head
        else:
            k = kc_ref[0, r, (blk - 1) * ATT_BLOCK:(blk + 1) * ATT_BLOCK, :]
            tile_bias = bias
        return [_dot_nt(q[:, sl], k[:, sl]) + tile_bias for sl in heads]

    def softmax(s_list):
        out = []
        for s in s_list:
            m = jnp.max(s, axis=-1, keepdims=True)
            e = jnp.exp(s - m)
            den = jnp.sum(e, axis=-1, keepdims=True)
            out.append((e.astype(BF16), den, m))
        return out

    def weighted_values(r, blk, p_list):
        if blk == 0:
            v = jnp.concatenate([vp_ref[slot_in, r], vc_ref[0, r, 0:ATT_BLOCK, :]], axis=0)
        else:
            v = vc_ref[0, r, (blk - 1) * ATT_BLOCK:(blk + 1) * ATT_BLOCK, :]
        return [_dot(p, v[:, sl]) for (p, _, _), sl in zip(p_list, heads)]

    def store(r, blk, p_list, pv_list):
        if dil == 1:
            rows = pl.ds(blk * ATT_BLOCK, ATT_BLOCK)
        else:
            rows = pl.ds(blk * ATT_BLOCK * dil + r, ATT_BLOCK, stride=dil)
        stats = jnp.ones((ATT_BLOCK, HEAD_DIM), F32)
        for j, ((_, den, m), pv) in enumerate(zip(p_list, pv_list)):
            o_ref[j, rows, :] = pv
            stats = jnp.where(lane == j, m, jnp.where(lane == ATT_SLOTS + j, den, stats))
        o_ref[ATT_SLOTS, rows, :] = stats

    def run_tiles(tiles):
        s_next = scores(*tiles[0])
        for idx, (r, blk) in enumerate(tiles):
            s_cur = s_next
            if idx + 1 < len(tiles):
                s_next = scores(*tiles[idx + 1])
            p_list = softmax(s_cur)
            store(r, blk, p_list, weighted_values(r, blk, p_list))

    res_per_iter = ATT_TILES_PER_ITER // nblk
    if dil == res_per_iter:
        run_tiles([(r, blk) for r in range(dil) for blk in range(nblk)])
    else:
        def body(it, carry):
            run_tiles([(it * res_per_iter + rr, blk) for rr in range(res_per_iter) for blk in range(nblk)])
            return carry
        lax.fori_loop(0, dil // res_per_iter, body, 0)

    last = slice((nblk - 1) * ATT_BLOCK, nblk * ATT_BLOCK)
    kp_ref[slot_out] = kc_ref[0, :, last, :]
    vp_ref[slot_out] = vc_ref[0, :, last, :]


def _attention_call(q, k, v, group, dilation, nblk, batch, seq):
    n = seq // dilation
    steps = n // (nblk * ATT_BLOCK)
    span = nblk * ATT_BLOCK * dilation
    cur = pl.BlockSpec((1, dilation, nblk * ATT_BLOCK, ATT_OUT_WIDTH), lambda b, i: (b, 0, i, 0))
    carry = pltpu.VMEM((2, dilation, ATT_BLOCK, ATT_OUT_WIDTH), BF16)
    return pl.pallas_call(
        functools.partial(_attention_kernel, dilation, nblk),
        grid=(batch, steps),
        in_specs=[cur, cur, cur],
        out_specs=pl.BlockSpec((ATT_SLABS, span, HEAD_DIM), lambda b, i: (0, b * steps + i, 0)),
        out_shape=jax.ShapeDtypeStruct((ATT_SLABS, batch * seq, HEAD_DIM), F32),
        scratch_shapes=[carry, carry],
        compiler_params=_params(("arbitrary", "arbitrary")),
        name=f"attention_g{group}",
    )(q, k, v)


def _mlstm_kernel(mq_ref, mk_ref, mv_ref, gt_ref, o_ref, state_ref, m_ref):
    L = MLSTM_CHUNK
    H = MLSTM_HEADS

    @pl.when(pl.program_id(0) == 0)
    def _():
        state_ref[...] = jnp.zeros_like(state_ref)
        m_ref[...] = jnp.zeros_like(m_ref)

    ti = lax.broadcasted_iota(jnp.int32, (L, L), 0)
    si = lax.broadcasted_iota(jnp.int32, (L, L), 1)
    causal = ti >= si
    tri = causal.astype(F32)
    ones = jnp.ones((L, HEAD_DIM), BF16)

    nbatch = mq_ref.shape[0]
    streams = [divmod(bh, H) for bh in range(nbatch * H)]
    lanes = lambda hd: slice(hd * HEAD_DIM, (hd + 1) * HEAD_DIM)

    gts = [gt_ref[bi] for bi in range(nbatch)]
    b_rows = [lax.dot_general(gt, tri, (((1,), (1,)), ((), ())), precision=lax.Precision.HIGHEST,
                              preferred_element_type=F32) for gt in gts]

    qs = [mq_ref[bi, :, lanes(hd)] for bi, hd in streams]
    ks = [mk_ref[bi, :, lanes(hd)] for bi, hd in streams]
    vexts = [jnp.concatenate([mv_ref[bi, :, lanes(hd)], ones], axis=1) for bi, hd in streams]
    states = [state_ref[bh] for bh in range(len(streams))]
    m_prevs = [m_ref[bh, 0:1, 0:1] for bh in range(len(streams))]
    kts = [k.T for k in ks]
    qk = [_dot(q, kt) for q, kt in zip(qs, kts)]
    q_state = [_dot(q, st.astype(BF16)) for q, st in zip(qs, states)]

    c_rows, upd, decays, m_news = [], [], [], []
    for bh, (bi, hd) in enumerate(streams):
        b_row = b_rows[bi][H + hd:H + hd + 1, :]
        c_row = gts[bi][hd:hd + 1, :] - b_row
        b_last = b_row[:, L - 1:L]
        g_row = b_last + c_row
        m_new = jnp.maximum(b_last + m_prevs[bh], jnp.max(g_row, axis=-1, keepdims=True))
        wk_row = jnp.exp(g_row - m_new)
        c_rows.append(c_row)
        decays.append(jnp.exp(b_last + m_prevs[bh] - m_new))
        m_news.append(m_new)
        upd.append(_dot((kts[bh] * wk_row).astype(BF16), vexts[bh]))

    s_list, a_list, floor_list = [], [], []
    for bh, (bi, hd) in enumerate(streams):
        cm = jnp.where(causal, c_rows[bh], -jnp.inf)
        n_t = jnp.maximum(m_prevs[bh], jnp.max(cm, axis=-1, keepdims=True))
        b_t = jnp.sum(jnp.where(causal, gts[bi][H + hd:H + hd + 1, :], 0.0), axis=-1, keepdims=True)
        s_list.append((qk[bh] * jnp.exp(cm - n_t)).astype(BF16))
        a_list.append(jnp.exp(m_prevs[bh] - n_t))
        floor_list.append(jnp.exp(-(b_t + n_t)))

    sv = [_dot(s, vext) for s, vext in zip(s_list, vexts)]
    for bh, (bi, hd) in enumerate(streams):
        ext = a_list[bh] * q_state[bh] + sv[bh]
        num = ext[:, :HEAD_DIM]
        den = ext[:, HEAD_DIM:]
        o_ref[bi, :, lanes(hd)] = (num / jnp.maximum(jnp.abs(den), floor_list[bh])).astype(o_ref.dtype)
        state_ref[bh] = decays[bh] * states[bh] + upd[bh]
        m_ref[bh] = jnp.broadcast_to(m_news[bh], m_ref.shape[1:])


def _mlstm_call(mq, mk, mv, gt, batch, seq):
    nc = seq // MLSTM_CHUNK
    W = MLSTM_WIDTH
    blk = lambda w: pl.BlockSpec((batch, MLSTM_CHUNK, w), lambda c: (0, c, 0))
    out = pl.pallas_call(
        _mlstm_kernel,
        grid=(nc,),
        in_specs=[blk(W), blk(W), blk(W),
                  pl.BlockSpec((batch, 2 * MLSTM_HEADS, MLSTM_CHUNK), lambda c: (0, 0, c))],
        out_specs=blk(W),
        out_shape=jax.ShapeDtypeStruct((batch, seq, W), BF16),
        scratch_shapes=[pltpu.VMEM((batch * MLSTM_HEADS, HEAD_DIM, 2 * HEAD_DIM), F32),
                        pltpu.VMEM((batch * MLSTM_HEADS, 8, 128), F32)],
        compiler_params=_params(("arbitrary",)),
        name="mlstm",
    )(mq.reshape(batch, seq, W), mk.reshape(batch, seq, W), mv.reshape(batch, seq, W), gt)
    return out.reshape(batch * seq, W)


def _mix_out_kernel(h_ref, a0_ref, a1_ref, a2_ref, ml_ref, g_ref, wg_ref, wua_ref, wub_ref, wo_ref, out_ref):
    h = h_ref[...]
    u = _rms_norm(h, g_ref[...]).astype(BF16)
    zg = _dot(u, wg_ref[...])
    mo = zg[:, :MLSTM_WIDTH]
    ga = zg[:, MLSTM_WIDTH:MLSTM_WIDTH + D_MODEL]
    gb = zg[:, MLSTM_WIDTH + D_MODEL:]

    a_refs = (a0_ref, a1_ref, a2_ref)
    stats = [a[ATT_SLOTS] for a in a_refs]
    mx = jnp.maximum(jnp.maximum(stats[0], stats[1]), stats[2])
    ws = [jnp.exp(s - mx) for s in stats]
    dens = [pltpu.roll(s, HEAD_DIM - ATT_SLOTS, 1) for s in stats]
    inv = 1.0 / (ws[0] * dens[0] + ws[1] * dens[1] + ws[2] * dens[2])
    alphas = [w * inv for w in ws]
    merged_heads = []
    for j in range(ATT_SLOTS):
        acc = alphas[0][:, j:j + 1] * a_refs[0][j]
        acc = acc + alphas[1][:, j:j + 1] * a_refs[1][j]
        acc = acc + alphas[2][:, j:j + 1] * a_refs[2][j]
        merged_heads.append(acc.astype(BF16))
    att = jnp.concatenate(merged_heads, axis=1)
    y_att = _dot(att, wua_ref[...])
    y_ml = _dot((jax.nn.sigmoid(mo) * ml_ref[...]).astype(BF16), wub_ref[...])
    merged = jax.nn.sigmoid(ga) * y_att + jax.nn.sigmoid(gb) * y_ml
    out_ref[...] = h + _dot(merged.astype(BF16), wo_ref[...])


def _mix_out_call(h2d, att, ml, mix_norm, w_in, w_up_att, w_up_mlstm, w_out, tm):
    n_rows = h2d.shape[0]
    g2 = mix_norm.reshape(1, D_MODEL)
    o_mo = 3 * ATT_WIDTH + 3 * MLSTM_WIDTH
    o_g = o_mo + MLSTM_WIDTH + 2 * MLSTM_HEADS
    wg = jnp.concatenate([w_in[:, o_mo:o_mo + MLSTM_WIDTH], w_in[:, o_g:o_g + 2 * D_MODEL]], axis=1).astype(BF16)
    wua = w_up_att.astype(BF16)
    wub = w_up_mlstm.astype(BF16)
    wo = w_out.astype(BF16)
    row = lambda w: pl.BlockSpec((tm, w), lambda i: (i, 0))
    slabs = pl.BlockSpec((ATT_SLABS, tm, HEAD_DIM), lambda i: (0, i, 0))
    return pl.pallas_call(
        _mix_out_kernel,
        grid=(n_rows // tm,),
        in_specs=[row(D_MODEL), slabs, slabs, slabs, row(MLSTM_WIDTH),
                  _resident(g2.shape), _resident(wg.shape), _resident(wua.shape), _resident(wub.shape),
                  _resident(wo.shape)],
        out_specs=row(D_MODEL),
        out_shape=jax.ShapeDtypeStruct((n_rows, D_MODEL), F32),
        compiler_params=_params(("parallel",)),
        name="mix_out",
    )(h2d, *att, ml, g2, wg, wua, wub, wo)


def _layer(x2d, p2d, batch, seq, ffn1_norm, ffn1_w_in, ffn1_w_out, mix_norm, w_in, q_gain, k_gain, conv_w, conv_b,
           i_bias, f_bias, w_up_att, w_up_mlstm, w_out, ffn2_norm, ffn2_w_in, ffn2_w_out, ple_norm, w_ple_gate,
           w_ple_proj):
    tm = 512
    h = _ffn_call(x2d, ffn1_norm, ffn1_w_in, ffn1_w_out, tm)
    qkv, (mq, mk, mv, gt) = _in_proj_call(h, mix_norm, w_in, q_gain, k_gain, conv_w, conv_b, i_bias, f_bias,
                                              batch, seq, 256)
    att = []
    for g, (window, dilation) in enumerate(ATT_GROUPS):
        assert window // dilation == ATT_BLOCK
        att.append(_attention_call(*qkv[g], g, dilation, ATT_BLOCKS_PER_STEP[g], batch, seq))
    ml = _mlstm_call(mq, mk, mv, gt, batch, seq)
    h = _mix_out_call(h, att, ml, mix_norm, w_in, w_up_att, w_up_mlstm, w_out, tm)
    ple = (p2d, ple_norm, w_ple_gate.astype(BF16), w_ple_proj.astype(BF16))
    return _ffn_call(h, ffn2_norm, ffn2_w_in, ffn2_w_out, 1024, ple=ple)


def kernel(x, p, ffn1_norm, ffn1_w_in, ffn1_w_out, mix_norm, w_in, q_gain, k_gain, conv_w, conv_b, i_bias, f_bias,
           w_up_att, w_up_mlstm, w_out, ffn2_norm, ffn2_w_in, ffn2_w_out, ple_norm, w_ple_gate, w_ple_proj):
    batch, seq, d = x.shape
    depth = p.shape[0]
    assert d == D_MODEL and seq % (ATT_GROUPS[-1][1] * ATT_BLOCK) == 0
    h = x.reshape(batch * seq, d)
    for i in range(depth):
        h = _layer(h, p[i].reshape(batch * seq, PLE_DIM), batch, seq, ffn1_norm[i], ffn1_w_in[i], ffn1_w_out[i],
                   mix_norm[i], w_in[i], q_gain[i], k_gain[i], conv_w[i], conv_b[i], i_bias[i], f_bias[i],
                   w_up_att[i], w_up_mlstm[i], w_out[i], ffn2_norm[i], ffn2_w_in[i], ffn2_w_out[i],
                   ple_norm[i], w_ple_gate[i], w_ple_proj[i])
    return h.reshape(batch, seq, d)
```

```python
import functools

import jax
import jax.numpy as jnp
import numpy as np
from jax import lax
from jax.experimental import pallas as pl
from jax.experimental.pallas import tpu as pltpu

D_MODEL = 1024
PLE_DIM = 256
ATT_GROUPS = ((128, 1), (512, 4), (2048, 16))
N_ATT_GROUPS = 3
ATT_SLOTS = 4
ATT_HEADS = N_ATT_GROUPS * ATT_SLOTS
HEAD_DIM = 128
ATT_WIDTH = ATT_HEADS * HEAD_DIM
ATT_OUT_WIDTH = ATT_SLOTS * HEAD_DIM
ROPE_THETA = 500000.0
ROPE_DIM = HEAD_DIM // 4
ROPE_HALF = ROPE_DIM // 2
ROPE_SPLIT = 64
MLSTM_HEADS = 4
MLSTM_WIDTH = MLSTM_HEADS * HEAD_DIM
MLSTM_CHUNK = 128
CONV_WIDTH = 4
D_FF = 2816
NORM_EPS = 1e-6

FF_CHUNKS = ((0, 512), (512, 512), (1024, 512), (1536, 512), (2048, 512), (2560, 256))
ATT_BLOCK = 128
ATT_BLOCKS_PER_STEP = (8, 2, 1)
ATT_TILES_PER_ITER = 8
ATT_SLABS = ATT_SLOTS + 1
PERM_ROWS = 256
CONV_PAD = 8
V7X_VMEM_LIMIT = 56 * 1024 * 1024

BF16 = jnp.bfloat16
F32 = jnp.float32


def _rms_norm(x, g):
    ms = jnp.mean(x * x, axis=-1, keepdims=True)
    return x * lax.rsqrt(ms + NORM_EPS) * g


def _silu(x):
    return x * jax.nn.sigmoid(x)


def _dot(a, b):
    return jnp.dot(a, b, preferred_element_type=F32)


def _dot_nt(a, b):
    return lax.dot_general(a, b, (((1,), (1,)), ((), ())), preferred_element_type=F32)


def _dot_tn(a, b):
    return lax.dot_general(a, b, (((0,), (0,)), ((), ())), preferred_element_type=F32)


def _resident(shape):
    nd = len(shape)
    return pl.BlockSpec(shape, lambda *_: (0,) * nd, pipeline_mode=pl.Buffered(1))


def _params(semantics):
    return pltpu.CompilerParams(dimension_semantics=semantics, vmem_limit_bytes=V7X_VMEM_LIMIT)


def _swiglu_update(x_ref, g_ref, w_in_ref, w_out_ref, acc_ref):
    u = _rms_norm(x_ref[...], g_ref[...]).astype(BF16)
    for idx, (off, width) in enumerate(FF_CHUNKS):
        a = _dot(u, w_in_ref[:, off:off + width])
        b = _dot(u, w_in_ref[:, D_FF + off:D_FF + off + width])
        part = _dot((_silu(a) * b).astype(BF16), w_out_ref[off:off + width, :])
        if idx == 0:
            acc_ref[...] = part
        else:
            acc_ref[...] += part
    return x_ref[...] + 0.5 * acc_ref[...]


def _ffn_kernel(x_ref, g_ref, w_in_ref, w_out_ref, o_ref, acc_ref):
    o_ref[...] = _swiglu_update(x_ref, g_ref, w_in_ref, w_out_ref, acc_ref)


def _ffn_ple_kernel(x_ref, p_ref, g_ref, w_in_ref, w_out_ref, pg_ref, w_pg_ref, w_pe_ref, o_ref, acc_ref):
    h = _swiglu_update(x_ref, g_ref, w_in_ref, w_out_ref, acc_ref)
    u = _rms_norm(h, pg_ref[...]).astype(BF16)
    gate = jax.nn.sigmoid(_dot(u, w_pg_ref[...]))
    pe = _dot(p_ref[...].astype(BF16), w_pe_ref[...])
    o_ref[...] = h + pe * gate


def _ffn_call(x2d, norm_g, w_in, w_out, tm, ple=None):
    n_rows = x2d.shape[0]
    row = lambda w: pl.BlockSpec((tm, w), lambda i: (i, 0))
    g2 = norm_g.reshape(1, D_MODEL)
    w_in = w_in.astype(BF16)
    w_out = w_out.astype(BF16)
    if ple is None:
        kern = _ffn_kernel
        args = (x2d, g2, w_in, w_out)
        specs = [row(D_MODEL), _resident(g2.shape), _resident(w_in.shape), _resident(w_out.shape)]
        name = "ffn"
    else:
        p2d, ple_g, w_pg, w_pe = ple
        pg2 = ple_g.reshape(1, D_MODEL)
        kern = _ffn_ple_kernel
        args = (x2d, p2d, g2, w_in, w_out, pg2, w_pg, w_pe)
        specs = [row(D_MODEL), row(PLE_DIM), _resident(g2.shape), _resident(w_in.shape),
                 _resident(w_out.shape), _resident(pg2.shape), _resident(w_pg.shape), _resident(w_pe.shape)]
        name = "ffn_ple"
    return pl.pallas_call(
        kern,
        grid=(n_rows // tm,),
        in_specs=specs,
        out_specs=row(D_MODEL),
        out_shape=jax.ShapeDtypeStruct((n_rows, D_MODEL), F32),
        scratch_shapes=[pltpu.VMEM((tm, D_MODEL), F32)],
        compiler_params=_params(("parallel",)),
        name=name,
    )(*args)


def _log_sigmoid(x):
    return jnp.minimum(x, 0.0) - jnp.log1p(jnp.exp(-jnp.abs(x)))


def _in_proj_kernel(h_ref, hp_ref, g_ref, watt_ref, wm_ref, wgt_ref, qg_ref, kg_ref,
                    cos_ref, sin_ref, perm_ref, cw_ref, cb_ref, gbt_ref,
                    q0_ref, k0_ref, v0_ref, q1_ref, k1_ref, v1_ref, q2_ref, k2_ref, v2_ref,
                    mq_ref, mk_ref, mv_ref, gt_ref, xpad_ref, u_ref):
    n_sub = h_ref.shape[0] // PERM_ROWS
    gnorm = g_ref[...]
    qgain = qg_ref[...]
    kgain = kg_ref[...]
    att_refs = ((q0_ref, k0_ref, v0_ref), (q1_ref, k1_ref, v1_ref), (q2_ref, k2_ref, v2_ref))
    sub_rows = lambda sub: slice(sub * PERM_ROWS, (sub + 1) * PERM_ROWS)

    def prepare(sub):
        rows = sub_rows(sub)
        u = _rms_norm(h_ref[rows, :], gnorm).astype(BF16)
        u_ref[0, rows, :] = u
        for grp in range(1, N_ATT_GROUPS):
            u_ref[grp, rows, :] = _dot(perm_ref[grp - 1], u).astype(BF16)

    def qk_head(zh, gain, cos, sin, scale):
        y = _rms_norm(zh, gain)
        y = y * cos + pltpu.roll(y, HEAD_DIM // 2, 1) * sin
        if scale != 1.0:
            y = y * scale
        return y.astype(BF16)

    def store_att(o_ref, val, dil, sub):
        if dil == 1:
            o_ref[0, 0, sub_rows(sub), :] = val
            return
        rows_per_res = PERM_ROWS // dil
        for r in range(dil):
            o_ref[0, r, sub * rows_per_res:(sub + 1) * rows_per_res, :] = val[r * rows_per_res:(r + 1) * rows_per_res, :]

    def att_item(sub, grp, dil, part):
        rows = sub_rows(sub)
        cols = slice(part * ATT_OUT_WIDTH, (part + 1) * ATT_OUT_WIDTH)
        matmul = lambda: _dot(u_ref[grp, rows, :], watt_ref[grp, :, cols])

        def epilogue(z):
            if part == 2:
                val = z.astype(BF16)
            else:
                gain, scale = ((qgain, HEAD_DIM ** -0.5), (kgain, 1.0))[part]
                cos = cos_ref[grp, rows, :]
                sin = sin_ref[grp, rows, :]
                val = jnp.concatenate([qk_head(z[:, j * HEAD_DIM:(j + 1) * HEAD_DIM], gain, cos, sin, scale)
                                       for j in range(ATT_SLOTS)], axis=1)
            store_att(att_refs[grp][part], val, dil, sub)
        return matmul, epilogue

    def conv_item(sub, part, o_ref, scale):
        rows = sub_rows(sub)
        cols = slice(part * MLSTM_WIDTH, (part + 1) * MLSTM_WIDTH)

        def matmul():
            z = _dot(u_ref[0, rows, :], wm_ref[:, cols])
            if sub > 0:
                return z, None
            up = _rms_norm(hp_ref[...], gnorm).astype(BF16)
            return z, _dot(up, wm_ref[:, cols])

        def epilogue(zz):
            z, zp = zz
            if sub == 0:
                zp = jnp.where(pl.program_id(0) > 0, zp, 0.0)
            for c in range(MLSTM_WIDTH // HEAD_DIM):
                slab = part * (MLSTM_WIDTH // HEAD_DIM) + c
                lanes = slice(part * MLSTM_WIDTH + c * HEAD_DIM, part * MLSTM_WIDTH + (c + 1) * HEAD_DIM)
                if sub == 0:
                    xpad_ref[sub, slab, 0:CONV_PAD, :] = zp[:, c * HEAD_DIM:(c + 1) * HEAD_DIM]
                else:
                    xpad_ref[sub, slab, 0:CONV_PAD, :] = xpad_ref[sub - 1, slab, PERM_ROWS:PERM_ROWS + CONV_PAD, :]
                xpad_ref[sub, slab, CONV_PAD:CONV_PAD + PERM_ROWS, :] = z[:, c * HEAD_DIM:(c + 1) * HEAD_DIM]
                acc = cb_ref[:, lanes]
                for j in range(CONV_WIDTH):
                    off = CONV_PAD - (CONV_WIDTH - 1) + j
                    acc = acc + xpad_ref[sub, slab, off:off + PERM_ROWS, :] * cw_ref[j:j + 1, lanes]
                y = _silu(acc)
                if scale != 1.0:
                    y = y * scale
                o_ref[rows, c * HEAD_DIM:(c + 1) * HEAD_DIM] = y.astype(BF16)
        return matmul, epilogue

    def value_gate_item(sub):
        rows = sub_rows(sub)

        def matmul():
            uu = u_ref[0, rows, :]
            return _dot(uu, wm_ref[:, 2 * MLSTM_WIDTH:]), _dot_nt(wgt_ref[...], uu)

        def epilogue(zz):
            zv, zt = zz
            mv_ref[rows, :] = zv.astype(BF16)
            zt = zt + gbt_ref[...]
            head = lax.broadcasted_iota(jnp.int32, zt.shape, 0)
            gt_ref[0, :, rows] = jnp.where(head < MLSTM_HEADS, zt, _log_sigmoid(zt))
        return matmul, epilogue

    def with_prepare(sub, item):
        matmul, epilogue = item

        def prepared_matmul():
            prepare(sub)
            return matmul()
        return prepared_matmul, epilogue

    per_sub = []
    for sub in range(n_sub):
        light = [conv_item(sub, 0, mq_ref, 1.0), conv_item(sub, 1, mk_ref, HEAD_DIM ** -0.5), value_gate_item(sub)]
        seq_items = []
        for grp, (_, dil) in enumerate(ATT_GROUPS):
            seq_items += [att_item(sub, grp, dil, 0), att_item(sub, grp, dil, 2), att_item(sub, grp, dil, 1),
                          light[grp]]
        seq_items[0] = with_prepare(sub, seq_items[0])
        per_sub.append(seq_items)
    items = [per_sub[sub][k] for k in range(len(per_sub[0])) for sub in range(n_sub)]
    pending = items[0][0]()
    for idx, (_, epilogue) in enumerate(items):
        current = pending
        if idx + 1 < len(items):
            pending = items[idx + 1][0]()
        epilogue(current)


def _rope_lane_perm():
    return (list(range(0, ROPE_HALF)) + list(range(ROPE_DIM, ROPE_DIM + 48))
            + list(range(ROPE_HALF, ROPE_DIM)) + list(range(ROPE_DIM + 48, HEAD_DIM)))


def _deinterleave_rows(t, dil):
    n, w = t.shape
    return t.reshape(n // PERM_ROWS, PERM_ROWS // dil, dil, w).transpose(0, 2, 1, 3).reshape(n, w)


def _rope_tables(seq):
    inv_freq = 1.0 / (ROPE_THETA ** (jnp.arange(ROPE_HALF, dtype=F32) / ROPE_HALF))
    rest = jnp.zeros((HEAD_DIM // 2 - ROPE_HALF,), F32)
    freq = jnp.concatenate([inv_freq, rest, inv_freq, rest])
    sign = jnp.concatenate([-jnp.ones((ROPE_HALF,), F32), rest, jnp.ones((ROPE_HALF,), F32), rest])
    coarse = (jnp.arange(seq // ROPE_SPLIT, dtype=F32) * ROPE_SPLIT).reshape(-1, 1) * freq
    fine = jnp.arange(ROPE_SPLIT, dtype=F32).reshape(-1, 1) * freq
    ca, sa = jnp.cos(coarse)[:, None, :], jnp.sin(coarse)[:, None, :]
    cb, sb = jnp.cos(fine)[None, :, :], jnp.sin(fine)[None, :, :]
    cos_t = (ca * cb - sa * sb).reshape(seq, HEAD_DIM)
    sin_t = ((sa * cb + ca * sb) * sign).reshape(seq, HEAD_DIM)
    cos_g = jnp.stack([_deinterleave_rows(cos_t, d) for _, d in ATT_GROUPS])
    sin_g = jnp.stack([_deinterleave_rows(sin_t, d) for _, d in ATT_GROUPS])
    return cos_g, sin_g


def _in_proj_call(h2d, mix_norm, w_in, q_gain, k_gain, conv_w, conv_b, i_bias, f_bias, batch, seq, tm):
    n_rows = h2d.shape[0]
    perm = np.asarray(_rope_lane_perm(), dtype=np.int32)
    slot_perm = (np.arange(ATT_SLOTS, dtype=np.int32)[:, None] * HEAD_DIM + perm[None, :]).reshape(-1)
    wq, wk, wv = (w_in[:, i * ATT_WIDTH:(i + 1) * ATT_WIDTH] for i in range(3))
    w_att = []
    for g in range(N_ATT_GROUPS):
        cols = slice(g * ATT_OUT_WIDTH, (g + 1) * ATT_OUT_WIDTH)
        w_att.append(jnp.concatenate([wq[:, cols][:, slot_perm], wk[:, cols][:, slot_perm], wv[:, cols]], axis=1))
    w_att = jnp.stack(w_att).astype(BF16)
    o = 3 * ATT_WIDTH
    wm = w_in[:, o:o + 3 * MLSTM_WIDTH].astype(BF16)
    o += 4 * MLSTM_WIDTH
    wgt = w_in[:, o:o + 2 * MLSTM_HEADS].T.astype(BF16)
    gbt = jnp.concatenate([i_bias, f_bias]).astype(F32).reshape(2 * MLSTM_HEADS, 1)
    qg = q_gain[perm].reshape(1, HEAD_DIM)
    kg = k_gain[perm].reshape(1, HEAD_DIM)
    cos_g, sin_g = _rope_tables(seq)
    eye = np.eye(PERM_ROWS, dtype=np.float32)
    perms = jnp.asarray(np.stack([
        eye.reshape(PERM_ROWS // d, d, PERM_ROWS).transpose(1, 0, 2).reshape(PERM_ROWS, PERM_ROWS)
        for _, d in ATT_GROUPS[1:]]), dtype=BF16)
    g2 = mix_norm.reshape(1, D_MODEL)
    cb2 = conv_b.reshape(1, 2 * MLSTM_WIDTH)
    tiles_per_seq = seq // tm

    row = lambda w: pl.BlockSpec((tm, w), lambda t, b: (b * tiles_per_seq + t, 0))
    prev_rows = pl.BlockSpec(
        (CONV_PAD, D_MODEL), lambda t, b: (jnp.maximum((b * tiles_per_seq + t) * (tm // CONV_PAD) - 1, 0), 0))
    pos = pl.BlockSpec((N_ATT_GROUPS, tm, HEAD_DIM), lambda t, b: (0, t, 0))
    in_specs = [row(D_MODEL), prev_rows, _resident(g2.shape), _resident(w_att.shape), _resident(wm.shape),
                _resident(wgt.shape), _resident(qg.shape), _resident(kg.shape), pos, pos,
                _resident(perms.shape), _resident(conv_w.shape), _resident(cb2.shape), _resident(gbt.shape)]
    out_specs, out_shape = [], []
    for _, d in ATT_GROUPS:
        spec = pl.BlockSpec((1, d, tm // d, ATT_OUT_WIDTH), lambda t, b: (b, 0, t, 0))
        out_specs += [spec] * 3
        out_shape += [jax.ShapeDtypeStruct((batch, d, seq // d, ATT_OUT_WIDTH), BF16)] * 3
    out_specs += [row(MLSTM_WIDTH)] * 3 + [pl.BlockSpec((1, 2 * MLSTM_HEADS, tm), lambda t, b: (b, 0, t))]
    out_shape += [jax.ShapeDtypeStruct((n_rows, MLSTM_WIDTH), BF16)] * 3 + [
        jax.ShapeDtypeStruct((batch, 2 * MLSTM_HEADS, seq), F32)]
    outs = pl.pallas_call(
        _in_proj_kernel,
        grid=(tiles_per_seq, batch),
        in_specs=in_specs,
        out_specs=out_specs,
        out_shape=out_shape,
        scratch_shapes=[pltpu.VMEM((tm // PERM_ROWS, 2 * MLSTM_WIDTH // HEAD_DIM, PERM_ROWS + CONV_PAD, HEAD_DIM), F32),
                        pltpu.VMEM((N_ATT_GROUPS, tm, D_MODEL), BF16)],
        compiler_params=_params(("parallel", "parallel")),
        name="in_proj",
    )(h2d, h2d, g2, w_att, wm, wgt, qg, kg, cos_g, sin_g, perms, conv_w, cb2, gbt)
    qkv = [tuple(outs[3 * g:3 * g + 3]) for g in range(N_ATT_GROUPS)]
    return qkv, outs[9:]


def _attention_kernel(dil, nblk, q_ref, kc_ref, vc_ref, o_ref, kp_ref, vp_ref):
    slot_in = pl.program_id(1) % 2
    slot_out = 1 - slot_in

    @pl.when(pl.program_id(1) == 0)
    def _():
        kp_ref[0] = jnp.zeros(kp_ref.shape[1:], kp_ref.dtype)
        vp_ref[0] = jnp.zeros(vp_ref.shape[1:], vp_ref.dtype)

    qi = lax.broadcasted_iota(jnp.int32, (ATT_BLOCK, 2 * ATT_BLOCK), 0)
    kj = lax.broadcasted_iota(jnp.int32, (ATT_BLOCK, 2 * ATT_BLOCK), 1)
    band = (kj >= qi) & (kj <= qi + ATT_BLOCK)
    bias = jnp.where(band, 0.0, -jnp.inf)
    first_key = jnp.where(pl.program_id(1) > 0, 0, ATT_BLOCK)
    bias_head = jnp.where(kj >= first_key, bias, -jnp.inf)
    lane = lax.broadcasted_iota(jnp.int32, (ATT_BLOCK, HEAD_DIM), 1)

    heads = [slice(j * HEAD_DIM, (j + 1) * HEAD_DIM) for j in range(ATT_SLOTS)]

    def scores(r, blk):
        q = q_ref[0, r, blk * ATT_BLOCK:(blk + 1) * ATT_BLOCK, :]
        if blk == 0:
            k = jnp.concatenate([kp_ref[slot_in, r], kc_ref[0, r, 0:ATT_BLOCK, :]], axis=0)
            tile_bias = bias_head
        else:
            k = kc_ref[0, r, (blk - 1) * ATT_BLOCK:(blk + 1) * ATT_BLOCK, :]
            tile_bias = bias
        return [_dot_nt(q[:, sl], k[:, sl]) + tile_bias for sl in heads]

    def softmax(s_list):
        out = []
        for s in s_list:
            m = jnp.max(s, axis=-1, keepdims=True)
            e = jnp.exp(s - m)
            den = jnp.sum(e, axis=-1, keepdims=True)
            out.append((e.astype(BF16), den, m))
        return out

    def weighted_values(r, blk, p_list):
        if blk == 0:
            v = jnp.concatenate([vp_ref[slot_in, r], vc_ref[0, r, 0:ATT_BLOCK, :]], axis=0)
        else:
            v = vc_ref[0, r, (blk - 1) * ATT_BLOCK:(blk + 1) * ATT_BLOCK, :]
        return [_dot(p, v[:, sl]) for (p, _, _), sl in zip(p_list, heads)]

    def store(r, blk, p_list, pv_list):
        if dil == 1:
            rows = pl.ds(blk * ATT_BLOCK, ATT_BLOCK)
        else:
            rows = pl.ds(blk * ATT_BLOCK * dil + r, ATT_BLOCK, stride=dil)
        stats = jnp.ones((ATT_BLOCK, HEAD_DIM), F32)
        for j, ((_, den, m), pv) in enumerate(zip(p_list, pv_list)):
            o_ref[j, rows, :] = pv
            stats = jnp.where(lane == j, m, jnp.where(lane == ATT_SLOTS + j, den, stats))
        o_ref[ATT_SLOTS, rows, :] = stats

    def run_tiles(tiles):
        s_next = scores(*tiles[0])
        for idx, (r, blk) in enumerate(tiles):
            s_cur = s_next
            if idx + 1 < len(tiles):
                s_next = scores(*tiles[idx + 1])
            p_list = softmax(s_cur)
            store(r, blk, p_list, weighted_values(r, blk, p_list))

    res_per_iter = ATT_TILES_PER_ITER // nblk
    if dil == res_per_iter:
        run_tiles([(r, blk) for r in range(dil) for blk in range(nblk)])
    else:
        def body(it, carry):
            run_tiles([(it * res_per_iter + rr, blk) for rr in range(res_per_iter) for blk in range(nblk)])
            return carry
        lax.fori_loop(0, dil // res_per_iter, body, 0)

    last = slice((nblk - 1) * ATT_BLOCK, nblk * ATT_BLOCK)
    kp_ref[slot_out] = kc_ref[0, :, last, :]
    vp_ref[slot_out] = vc_ref[0, :, last, :]


def _attention_call(q, k, v, group, dilation, nblk, batch, seq):
    n = seq // dilation
    steps = n // (nblk * ATT_BLOCK)
    span = nblk * ATT_BLOCK * dilation
    cur = pl.BlockSpec((1, dilation, nblk * ATT_BLOCK, ATT_OUT_WIDTH), lambda b, i: (b, 0, i, 0))
    carry = pltpu.VMEM((2, dilation, ATT_BLOCK, ATT_OUT_WIDTH), BF16)
    return pl.pallas_call(
        functools.partial(_attention_kernel, dilation, nblk),
        grid=(batch, steps),
        in_specs=[cur, cur, cur],
        out_specs=pl.BlockSpec((ATT_SLABS, span, HEAD_DIM), lambda b, i: (0, b * steps + i, 0)),
        out_shape=jax.ShapeDtypeStruct((ATT_SLABS, batch * seq, HEAD_DIM), F32),
        scratch_shapes=[carry, carry],
        compiler_params=_params(("arbitrary", "arbitrary")),
        name=f"attention_g{group}",
    )(q, k, v)


def _mlstm_kernel(mq_ref, mk_ref, mv_ref, gt_ref, o_ref, state_ref, m_ref):
    L = MLSTM_CHUNK
    H = MLSTM_HEADS

    @pl.when(pl.program_id(0) == 0)
    def _():
        state_ref[...] = jnp.zeros_like(state_ref)
        m_ref[...] = jnp.zeros_like(m_ref)

    ti = lax.broadcasted_iota(jnp.int32, (L, L), 0)
    si = lax.broadcasted_iota(jnp.int32, (L, L), 1)
    causal = ti >= si
    tri = causal.astype(F32)
    ones = jnp.ones((L, HEAD_DIM), BF16)

    nbatch = mq_ref.shape[0]
    streams = [divmod(bh, H) for bh in range(nbatch * H)]
    lanes = lambda hd: slice(hd * HEAD_DIM, (hd + 1) * HEAD_DIM)

    gts = [gt_ref[bi] for bi in range(nbatch)]
    b_rows = [lax.dot_general(gt, tri, (((1,), (1,)), ((), ())), precision=lax.Precision.HIGHEST,
                              preferred_element_type=F32) for gt in gts]

    qs = [mq_ref[bi, :, lanes(hd)] for bi, hd in streams]
    ks = [mk_ref[bi, :, lanes(hd)] for bi, hd in streams]
    vexts = [jnp.concatenate([mv_ref[bi, :, lanes(hd)], ones], axis=1) for bi, hd in streams]
    states = [state_ref[bh] for bh in range(len(streams))]
    m_prevs = [m_ref[bh, 0:1, 0:1] for bh in range(len(streams))]
    kts = [k.T for k in ks]
    qk = [_dot(q, kt) for q, kt in zip(qs, kts)]
    q_state = [_dot(q, st.astype(BF16)) for q, st in zip(qs, states)]

    c_rows, upd, decays, m_news = [], [], [], []
    for bh, (bi, hd) in enumerate(streams):
        b_row = b_rows[bi][H + hd:H + hd + 1, :]
        c_row = gts[bi][hd:hd + 1, :] - b_row
        b_last = b_row[:, L - 1:L]
        g_row = b_last + c_row
        m_new = jnp.maximum(b_last + m_prevs[bh], jnp.max(g_row, axis=-1, keepdims=True))
        wk_row = jnp.exp(g_row - m_new)
        c_rows.append(c_row)
        decays.append(jnp.exp(b_last + m_prevs[bh] - m_new))
        m_news.append(m_new)
        upd.append(_dot((kts[bh] * wk_row).astype(BF16), vexts[bh]))

    s_list, a_list, floor_list = [], [], []
    for bh, (bi, hd) in enumerate(streams):
        cm = jnp.where(causal, c_rows[bh], -jnp.inf)
        n_t = jnp.maximum(m_prevs[bh], jnp.max(cm, axis=-1, keepdims=True))
        b_t = jnp.sum(jnp.where(causal, gts[bi][H + hd:H + hd + 1, :], 0.0), axis=-1, keepdims=True)
        s_list.append((qk[bh] * jnp.exp(cm - n_t)).astype(BF16))
        a_list.append(jnp.exp(m_prevs[bh] - n_t))
        floor_list.append(jnp.exp(-(b_t + n_t)))

    sv = [_dot(s, vext) for s, vext in zip(s_list, vexts)]
    for bh, (bi, hd) in enumerate(streams):
        ext = a_list[bh] * q_state[bh] + sv[bh]
        num = ext[:, :HEAD_DIM]
        den = ext[:, HEAD_DIM:]
        o_ref[bi, :, lanes(hd)] = (num / jnp.maximum(jnp.abs(den), floor_list[bh])).astype(o_ref.dtype)
        state_ref[bh] = decays[bh] * states[bh] + upd[bh]
        m_ref[bh] = jnp.broadcast_to(m_news[bh], m_ref.shape[1:])


def _mlstm_call(mq, mk, mv, gt, batch, seq):
    nc = seq // MLSTM_CHUNK
    W = MLSTM_WIDTH
    blk = lambda w: pl.BlockSpec((batch, MLSTM_CHUNK, w), lambda c: (0, c, 0))
    out = pl.pallas_call(
        _mlstm_kernel,
        grid=(nc,),
        in_specs=[blk(W), blk(W), blk(W),
                  pl.BlockSpec((batch, 2 * MLSTM_HEADS, MLSTM_CHUNK), lambda c: (0, 0, c))],
        out_specs=blk(W),
        out_shape=jax.ShapeDtypeStruct((batch, seq, W), BF16),
        scratch_shapes=[pltpu.VMEM((batch * MLSTM_HEADS, HEAD_DIM, 2 * HEAD_DIM), F32),
                        pltpu.VMEM((batch * MLSTM_HEADS, 8, 128), F32)],
        compiler_params=_params(("arbitrary",)),
        name="mlstm",
    )(mq.reshape(batch, seq, W), mk.reshape(batch, seq, W), mv.reshape(batch, seq, W), gt)
    return out.reshape(batch * seq, W)


def _mix_out_kernel(h_ref, a0_ref, a1_ref, a2_ref, ml_ref, g_ref, wg_ref, wua_ref, wub_ref, wo_ref, out_ref):
    h = h_ref[...]
    u = _rms_norm(h, g_ref[...]).astype(BF16)
    zg = _dot(u, wg_ref[...])
    mo = zg[:, :MLSTM_WIDTH]
    ga = zg[:, MLSTM_WIDTH:MLSTM_WIDTH + D_MODEL]
    gb = zg[:, MLSTM_WIDTH + D_MODEL:]

    a_refs = (a0_ref, a1_ref, a2_ref)
    stats = [a[ATT_SLOTS] for a in a_refs]
    mx = jnp.maximum(jnp.maximum(stats[0], stats[1]), stats[2])
    ws = [jnp.exp(s - mx) for s in stats]
    dens = [pltpu.roll(s, HEAD_DIM - ATT_SLOTS, 1) for s in stats]
    inv = 1.0 / (ws[0] * dens[0] + ws[1] * dens[1] + ws[2] * dens[2])
    alphas = [w * inv for w in ws]
    merged_heads = []
    for j in range(ATT_SLOTS):
        acc = alphas[0][:, j:j + 1] * a_refs[0][j]
        acc = acc + alphas[1][:, j:j + 1] * a_refs[1][j]
        acc = acc + alphas[2][:, j:j + 1] * a_refs[2][j]
        merged_heads.append(acc.astype(BF16))
    att = jnp.concatenate(merged_heads, axis=1)
    y_att = _dot(att, wua_ref[...])
    y_ml = _dot((jax.nn.sigmoid(mo) * ml_ref[...]).astype(BF16), wub_ref[...])
    merged = jax.nn.sigmoid(ga) * y_att + jax.nn.sigmoid(gb) * y_ml
    out_ref[...] = h + _dot(merged.astype(BF16), wo_ref[...])


def _mix_out_call(h2d, att, ml, mix_norm, w_in, w_up_att, w_up_mlstm, w_out, tm):
    n_rows = h2d.shape[0]
    g2 = mix_norm.reshape(1, D_MODEL)
    o_mo = 3 * ATT_WIDTH + 3 * MLSTM_WIDTH
    o_g = o_mo + MLSTM_WIDTH + 2 * MLSTM_HEADS
    wg = jnp.concatenate([w_in[:, o_mo:o_mo + MLSTM_WIDTH], w_in[:, o_g:o_g + 2 * D_MODEL]], axis=1).astype(BF16)
    wua = w_up_att.astype(BF16)
    wub = w_up_mlstm.astype(BF16)
    wo = w_out.astype(BF16)
    row = lambda w: pl.BlockSpec((tm, w), lambda i: (i, 0))
    slabs = pl.BlockSpec((ATT_SLABS, tm, HEAD_DIM), lambda i: (0, i, 0))
    return pl.pallas_call(
        _mix_out_kernel,
        grid=(n_rows // tm,),
        in_specs=[row(D_MODEL), slabs, slabs, slabs, row(MLSTM_WIDTH),
                  _resident(g2.shape), _resident(wg.shape), _resident(wua.shape), _resident(wub.shape),
                  _resident(wo.shape)],
        out_specs=row(D_MODEL),
        out_shape=jax.ShapeDtypeStruct((n_rows, D_MODEL), F32),
        compiler_params=_params(("parallel",)),
        name="mix_out",
    )(h2d, *att, ml, g2, wg, wua, wub, wo)


def _layer(x2d, p2d, batch, seq, ffn1_norm, ffn1_w_in, ffn1_w_out, mix_norm, w_in, q_gain, k_gain, conv_w, conv_b,
           i_bias, f_bias, w_up_att, w_up_mlstm, w_out, ffn2_norm, ffn2_w_in, ffn2_w_out, ple_norm, w_ple_gate,
           w_ple_proj):
    tm = 512
    h = _ffn_call(x2d, ffn1_norm, ffn1_w_in, ffn1_w_out, tm)
    qkv, (mq, mk, mv, gt) = _in_proj_call(h, mix_norm, w_in, q_gain, k_gain, conv_w, conv_b, i_bias, f_bias,
                                              batch, seq, 512)
    att = []
    for g, (window, dilation) in enumerate(ATT_GROUPS):
        assert window // dilation == ATT_BLOCK
        att.append(_attention_call(*qkv[g], g, dilation, ATT_BLOCKS_PER_STEP[g], batch, seq))
    ml = _mlstm_call(mq, mk, mv, gt, batch, seq)
    h = _mix_out_call(h, att, ml, mix_norm, w_in, w_up_att, w_up_mlstm, w_out, tm)
    ple = (p2d, ple_norm, w_ple_gate.astype(BF16), w_ple_proj.astype(BF16))
    return _ffn_call(h, ffn2_norm, ffn2_w_in, ffn2_w_out, 1024, ple=ple)


def kernel(x, p, ffn1_norm, ffn1_w_in, ffn1_w_out, mix_norm, w_in, q_gain, k_gain, conv_w, conv_b, i_bias, f_bias,
           w_up_att, w_up_mlstm, w_out, ffn2_norm, ffn2_w_in, ffn2_w_out, ple_norm, w_ple_gate, w_ple_proj):
    batch, seq, d = x.shape
    depth = p.shape[0]
    assert d == D_MODEL and seq % (ATT_GROUPS[-1][1] * ATT_BLOCK) == 0
    h = x.reshape(batch * seq, d)
    for i in range(depth):
        h = _layer(h, p[i].reshape(batch * seq, PLE_DIM), batch, seq, ffn1_norm[i], ffn1_w_in[i], ffn1_w_out[i],
                   mix_norm[i], w_in[i], q_gain[i], k_gain[i], conv_w[i], conv_b[i], i_bias[i], f_bias[i],
                   w_up_att[i], w_up_mlstm[i], w_out[i], ffn2_norm[i], ffn2_w_in[i], ffn2_w_out[i],
                   ple_norm[i], w_ple_gate[i], w_ple_proj[i])
    return h.reshape(batch, seq, d)
```

```python
import functools

import jax
import jax.numpy as jnp
import numpy as np
from jax import lax
from jax.experimental import pallas as pl
from jax.experimental.pallas import tpu as pltpu

D_MODEL = 1024
PLE_DIM = 256
ATT_GROUPS = ((128, 1), (512, 4), (2048, 16))
N_ATT_GROUPS = 3
ATT_SLOTS = 4
ATT_HEADS = N_ATT_GROUPS * ATT_SLOTS
HEAD_DIM = 128
ATT_WIDTH = ATT_HEADS * HEAD_DIM
ATT_OUT_WIDTH = ATT_SLOTS * HEAD_DIM
ROPE_THETA = 500000.0
ROPE_DIM = HEAD_DIM // 4
ROPE_HALF = ROPE_DIM // 2
ROPE_SPLIT = 64
MLSTM_HEADS = 4
MLSTM_WIDTH = MLSTM_HEADS * HEAD_DIM
MLSTM_CHUNK = 128
CONV_WIDTH = 4
D_FF = 2816
NORM_EPS = 1e-6

FF_CHUNKS = ((0, 512), (512, 512), (1024, 512), (1536, 512), (2048, 512), (2560, 256))
ATT_BLOCK = 128
ATT_BLOCKS_PER_STEP = (8, 2, 1)
ATT_TILES_PER_ITER = 8
ATT_SLABS = ATT_SLOTS + 1
PERM_ROWS = 256
MIX_SUB = 256
CONV_PAD = 8
V7X_VMEM_LIMIT = 56 * 1024 * 1024

BF16 = jnp.bfloat16
F32 = jnp.float32


def _rms_norm(x, g):
    ms = jnp.mean(x * x, axis=-1, keepdims=True)
    return x * lax.rsqrt(ms + NORM_EPS) * g


def _silu(x):
    return x * jax.nn.sigmoid(x)


def _dot(a, b):
    return jnp.dot(a, b, preferred_element_type=F32)


def _dot_nt(a, b):
    return lax.dot_general(a, b, (((1,), (1,)), ((), ())), preferred_element_type=F32)


def _dot_tn(a, b):
    return lax.dot_general(a, b, (((0,), (0,)), ((), ())), preferred_element_type=F32)


def _resident(shape):
    nd = len(shape)
    return pl.BlockSpec(shape, lambda *_: (0,) * nd, pipeline_mode=pl.Buffered(1))


def _params(semantics):
    return pltpu.CompilerParams(dimension_semantics=semantics, vmem_limit_bytes=V7X_VMEM_LIMIT)


def _swiglu_update(x_ref, g_ref, w_in_ref, w_out_ref, acc_ref):
    u = _rms_norm(x_ref[...], g_ref[...]).astype(BF16)
    for idx, (off, width) in enumerate(FF_CHUNKS):
        a = _dot(u, w_in_ref[:, off:off + width])
        b = _dot(u, w_in_ref[:, D_FF + off:D_FF + off + width])
        part = _dot((_silu(a) * b).astype(BF16), w_out_ref[off:off + width, :])
        if idx == 0:
            acc_ref[...] = part
        else:
            acc_ref[...] += part
    return x_ref[...] + 0.5 * acc_ref[...]


def _ffn_kernel(x_ref, g_ref, w_in_ref, w_out_ref, o_ref, acc_ref):
    o_ref[...] = _swiglu_update(x_ref, g_ref, w_in_ref, w_out_ref, acc_ref)


def _ffn_ple_kernel(x_ref, p_ref, g_ref, w_in_ref, w_out_ref, pg_ref, w_pg_ref, w_pe_ref, o_ref, acc_ref):
    h = _swiglu_update(x_ref, g_ref, w_in_ref, w_out_ref, acc_ref)
    u = _rms_norm(h, pg_ref[...]).astype(BF16)
    gate = jax.nn.sigmoid(_dot(u, w_pg_ref[...]))
    pe = _dot(p_ref[...].astype(BF16), w_pe_ref[...])
    o_ref[...] = h + pe * gate


def _ffn_call(x2d, norm_g, w_in, w_out, tm, ple=None):
    n_rows = x2d.shape[0]
    row = lambda w: pl.BlockSpec((tm, w), lambda i: (i, 0))
    g2 = norm_g.reshape(1, D_MODEL)
    w_in = w_in.astype(BF16)
    w_out = w_out.astype(BF16)
    if ple is None:
        kern = _ffn_kernel
        args = (x2d, g2, w_in, w_out)
        specs = [row(D_MODEL), _resident(g2.shape), _resident(w_in.shape), _resident(w_out.shape)]
        name = "ffn"
    else:
        p2d, ple_g, w_pg, w_pe = ple
        pg2 = ple_g.reshape(1, D_MODEL)
        kern = _ffn_ple_kernel
        args = (x2d, p2d, g2, w_in, w_out, pg2, w_pg, w_pe)
        specs = [row(D_MODEL), row(PLE_DIM), _resident(g2.shape), _resident(w_in.shape),
                 _resident(w_out.shape), _resident(pg2.shape), _resident(w_pg.shape), _resident(w_pe.shape)]
        name = "ffn_ple"
    return pl.pallas_call(
        kern,
        grid=(n_rows // tm,),
        in_specs=specs,
        out_specs=row(D_MODEL),
        out_shape=jax.ShapeDtypeStruct((n_rows, D_MODEL), F32),
        scratch_shapes=[pltpu.VMEM((tm, D_MODEL), F32)],
        compiler_params=_params(("parallel",)),
        name=name,
    )(*args)


def _log_sigmoid(x):
    return jnp.minimum(x, 0.0) - jnp.log1p(jnp.exp(-jnp.abs(x)))


def _in_proj_kernel(h_ref, hp_ref, g_ref, watt_ref, wm_ref, wgt_ref, qg_ref, kg_ref,
                    cos_ref, sin_ref, perm_ref, cw_ref, cb_ref, gbt_ref,
                    q0_ref, k0_ref, v0_ref, q1_ref, k1_ref, v1_ref, q2_ref, k2_ref, v2_ref,
                    mq_ref, mk_ref, mv_ref, gt_ref, xpad_ref, u_ref):
    n_sub = h_ref.shape[0] // PERM_ROWS
    gnorm = g_ref[...]
    qgain = qg_ref[...]
    kgain = kg_ref[...]
    att_refs = ((q0_ref, k0_ref, v0_ref), (q1_ref, k1_ref, v1_ref), (q2_ref, k2_ref, v2_ref))
    sub_rows = lambda sub: slice(sub * PERM_ROWS, (sub + 1) * PERM_ROWS)

    def prepare(sub):
        rows = sub_rows(sub)
        u = _rms_norm(h_ref[rows, :], gnorm).astype(BF16)
        u_ref[0, rows, :] = u
        for grp in range(1, N_ATT_GROUPS):
            u_ref[grp, rows, :] = _dot(perm_ref[grp - 1], u).astype(BF16)

    def qk_head(zh, gain, cos, sin, scale):
        y = _rms_norm(zh, gain)
        y = y * cos + pltpu.roll(y, HEAD_DIM // 2, 1) * sin
        if scale != 1.0:
            y = y * scale
        return y.astype(BF16)

    def store_att(o_ref, val, dil, sub):
        if dil == 1:
            o_ref[0, 0, sub_rows(sub), :] = val
            return
        rows_per_res = PERM_ROWS // dil
        for r in range(dil):
            o_ref[0, r, sub * rows_per_res:(sub + 1) * rows_per_res, :] = val[r * rows_per_res:(r + 1) * rows_per_res, :]

    def att_item(sub, grp, dil, part):
        rows = sub_rows(sub)
        cols = slice(part * ATT_OUT_WIDTH, (part + 1) * ATT_OUT_WIDTH)
        matmul = lambda: _dot(u_ref[grp, rows, :], watt_ref[grp, :, cols])

        def epilogue(z):
            if part == 2:
                val = z.astype(BF16)
            else:
                gain, scale = ((qgain, HEAD_DIM ** -0.5), (kgain, 1.0))[part]
                cos = cos_ref[grp, rows, :]
                sin = sin_ref[grp, rows, :]
                val = jnp.concatenate([qk_head(z[:, j * HEAD_DIM:(j + 1) * HEAD_DIM], gain, cos, sin, scale)
                                       for j in range(ATT_SLOTS)], axis=1)
            store_att(att_refs[grp][part], val, dil, sub)
        return matmul, epilogue

    def conv_item(sub, part, o_ref, scale):
        rows = sub_rows(sub)
        cols = slice(part * MLSTM_WIDTH, (part + 1) * MLSTM_WIDTH)

        def matmul():
            z = _dot(u_ref[0, rows, :], wm_ref[:, cols])
            if sub > 0:
                return z, None
            up = _rms_norm(hp_ref[...], gnorm).astype(BF16)
            return z, _dot(up, wm_ref[:, cols])

        def epilogue(zz):
            z, zp = zz
            if sub == 0:
                zp = jnp.where(pl.program_id(0) > 0, zp, 0.0)
            for c in range(MLSTM_WIDTH // HEAD_DIM):
                slab = part * (MLSTM_WIDTH // HEAD_DIM) + c
                lanes = slice(part * MLSTM_WIDTH + c * HEAD_DIM, part * MLSTM_WIDTH + (c + 1) * HEAD_DIM)
                if sub == 0:
                    xpad_ref[sub, slab, 0:CONV_PAD, :] = zp[:, c * HEAD_DIM:(c + 1) * HEAD_DIM]
                else:
                    xpad_ref[sub, slab, 0:CONV_PAD, :] = xpad_ref[sub - 1, slab, PERM_ROWS:PERM_ROWS + CONV_PAD, :]
                xpad_ref[sub, slab, CONV_PAD:CONV_PAD + PERM_ROWS, :] = z[:, c * HEAD_DIM:(c + 1) * HEAD_DIM]
                acc = cb_ref[:, lanes]
                for j in range(CONV_WIDTH):
                    off = CONV_PAD - (CONV_WIDTH - 1) + j
                    acc = acc + xpad_ref[sub, slab, off:off + PERM_ROWS, :] * cw_ref[j:j + 1, lanes]
                y = _silu(acc)
                if scale != 1.0:
                    y = y * scale
                o_ref[rows, c * HEAD_DIM:(c + 1) * HEAD_DIM] = y.astype(BF16)
        return matmul, epilogue

    def value_gate_item(sub):
        rows = sub_rows(sub)

        def matmul():
            uu = u_ref[0, rows, :]
            return _dot(uu, wm_ref[:, 2 * MLSTM_WIDTH:]), _dot_nt(wgt_ref[...], uu)

        def epilogue(zz):
            zv, zt = zz
            mv_ref[rows, :] = zv.astype(BF16)
            zt = zt + gbt_ref[...]
            head = lax.broadcasted_iota(jnp.int32, zt.shape, 0)
            gt_ref[0, :, rows] = jnp.where(head < MLSTM_HEADS, zt, _log_sigmoid(zt))
        return matmul, epilogue

    def with_prepare(sub, item):
        matmul, epilogue = item

        def prepared_matmul():
            prepare(sub)
            return matmul()
        return prepared_matmul, epilogue

    per_sub = []
    for sub in range(n_sub):
        light = [conv_item(sub, 0, mq_ref, 1.0), conv_item(sub, 1, mk_ref, HEAD_DIM ** -0.5), value_gate_item(sub)]
        seq_items = []
        for grp, (_, dil) in enumerate(ATT_GROUPS):
            seq_items += [att_item(sub, grp, dil, 0), att_item(sub, grp, dil, 2), att_item(sub, grp, dil, 1),
                          light[grp]]
        seq_items[0] = with_prepare(sub, seq_items[0])
        per_sub.append(seq_items)
    items = [per_sub[sub][k] for k in range(len(per_sub[0])) for sub in range(n_sub)]
    pending = items[0][0]()
    for idx, (_, epilogue) in enumerate(items):
        current = pending
        if idx + 1 < len(items):
            pending = items[idx + 1][0]()
        epilogue(current)


def _rope_lane_perm():
    return (list(range(0, ROPE_HALF)) + list(range(ROPE_DIM, ROPE_DIM + 48))
            + list(range(ROPE_HALF, ROPE_DIM)) + list(range(ROPE_DIM + 48, HEAD_DIM)))


def _deinterleave_rows(t, dil):
    n, w = t.shape
    return t.reshape(n // PERM_ROWS, PERM_ROWS // dil, dil, w).transpose(0, 2, 1, 3).reshape(n, w)


def _rope_tables(seq):
    inv_freq = 1.0 / (ROPE_THETA ** (jnp.arange(ROPE_HALF, dtype=F32) / ROPE_HALF))
    rest = jnp.zeros((HEAD_DIM // 2 - ROPE_HALF,), F32)
    freq = jnp.concatenate([inv_freq, rest, inv_freq, rest])
    sign = jnp.concatenate([-jnp.ones((ROPE_HALF,), F32), rest, jnp.ones((ROPE_HALF,), F32), rest])
    coarse = (jnp.arange(seq // ROPE_SPLIT, dtype=F32) * ROPE_SPLIT).reshape(-1, 1) * freq
    fine = jnp.arange(ROPE_SPLIT, dtype=F32).reshape(-1, 1) * freq
    ca, sa = jnp.cos(coarse)[:, None, :], jnp.sin(coarse)[:, None, :]
    cb, sb = jnp.cos(fine)[None, :, :], jnp.sin(fine)[None, :, :]
    cos_t = (ca * cb - sa * sb).reshape(seq, HEAD_DIM)
    sin_t = ((sa * cb + ca * sb) * sign).reshape(seq, HEAD_DIM)
    cos_g = jnp.stack([_deinterleave_rows(cos_t, d) for _, d in ATT_GROUPS])
    sin_g = jnp.stack([_deinterleave_rows(sin_t, d) for _, d in ATT_GROUPS])
    return cos_g, sin_g


def _in_proj_call(h2d, mix_norm, w_in, q_gain, k_gain, conv_w, conv_b, i_bias, f_bias, batch, seq, tm):
    n_rows = h2d.shape[0]
    perm = np.asarray(_rope_lane_perm(), dtype=np.int32)
    slot_perm = (np.arange(ATT_SLOTS, dtype=np.int32)[:, None] * HEAD_DIM + perm[None, :]).reshape(-1)
    wq, wk, wv = (w_in[:, i * ATT_WIDTH:(i + 1) * ATT_WIDTH] for i in range(3))
    w_att = []
    for g in range(N_ATT_GROUPS):
        cols = slice(g * ATT_OUT_WIDTH, (g + 1) * ATT_OUT_WIDTH)
        w_att.append(jnp.concatenate([wq[:, cols][:, slot_perm], wk[:, cols][:, slot_perm], wv[:, cols]], axis=1))
    w_att = jnp.stack(w_att).astype(BF16)
    o = 3 * ATT_WIDTH
    wm = w_in[:, o:o + 3 * MLSTM_WIDTH].astype(BF16)
    o += 4 * MLSTM_WIDTH
    wgt = w_in[:, o:o + 2 * MLSTM_HEADS].T.astype(BF16)
    gbt = jnp.concatenate([i_bias, f_bias]).astype(F32).reshape(2 * MLSTM_HEADS, 1)
    qg = q_gain[perm].reshape(1, HEAD_DIM)
    kg = k_gain[perm].reshape(1, HEAD_DIM)
    cos_g, sin_g = _rope_tables(seq)
    eye = np.eye(PERM_ROWS, dtype=np.float32)
    perms = jnp.asarray(np.stack([
        eye.reshape(PERM_ROWS // d, d, PERM_ROWS).transpose(1, 0, 2).reshape(PERM_ROWS, PERM_ROWS)
        for _, d in ATT_GROUPS[1:]]), dtype=BF16)
    g2 = mix_norm.reshape(1, D_MODEL)
    cb2 = conv_b.reshape(1, 2 * MLSTM_WIDTH)
    tiles_per_seq = seq // tm

    row = lambda w: pl.BlockSpec((tm, w), lambda t, b: (b * tiles_per_seq + t, 0))
    prev_rows = pl.BlockSpec(
        (CONV_PAD, D_MODEL), lambda t, b: (jnp.maximum((b * tiles_per_seq + t) * (tm // CONV_PAD) - 1, 0), 0))
    pos = pl.BlockSpec((N_ATT_GROUPS, tm, HEAD_DIM), lambda t, b: (0, t, 0))
    in_specs = [row(D_MODEL), prev_rows, _resident(g2.shape), _resident(w_att.shape), _resident(wm.shape),
                _resident(wgt.shape), _resident(qg.shape), _resident(kg.shape), pos, pos,
                _resident(perms.shape), _resident(conv_w.shape), _resident(cb2.shape), _resident(gbt.shape)]
    out_specs, out_shape = [], []
    for _, d in ATT_GROUPS:
        spec = pl.BlockSpec((1, d, tm // d, ATT_OUT_WIDTH), lambda t, b: (b, 0, t, 0))
        out_specs += [spec] * 3
        out_shape += [jax.ShapeDtypeStruct((batch, d, seq // d, ATT_OUT_WIDTH), BF16)] * 3
    out_specs += [row(MLSTM_WIDTH)] * 3 + [pl.BlockSpec((1, 2 * MLSTM_HEADS, tm), lambda t, b: (b, 0, t))]
    out_shape += [jax.ShapeDtypeStruct((n_rows, MLSTM_WIDTH), BF16)] * 3 + [
        jax.ShapeDtypeStruct((batch, 2 * MLSTM_HEADS, seq), F32)]
    outs = pl.pallas_call(
        _in_proj_kernel,
        grid=(tiles_per_seq, batch),
        in_specs=in_specs,
        out_specs=out_specs,
        out_shape=out_shape,
        scratch_shapes=[pltpu.VMEM((tm // PERM_ROWS, 2 * MLSTM_WIDTH // HEAD_DIM, PERM_ROWS + CONV_PAD, HEAD_DIM), F32),
                        pltpu.VMEM((N_ATT_GROUPS, tm, D_MODEL), BF16)],
        compiler_params=_params(("parallel", "parallel")),
        name="in_proj",
    )(h2d, h2d, g2, w_att, wm, wgt, qg, kg, cos_g, sin_g, perms, conv_w, cb2, gbt)
    qkv = [tuple(outs[3 * g:3 * g + 3]) for g in range(N_ATT_GROUPS)]
    return qkv, outs[9:]


def _attention_kernel(dil, nblk, q_ref, kc_ref, vc_ref, o_ref, kp_ref, vp_ref):
    slot_in = pl.program_id(1) % 2
    slot_out = 1 - slot_in

    @pl.when(pl.program_id(1) == 0)
    def _():
        kp_ref[0] = jnp.zeros(kp_ref.shape[1:], kp_ref.dtype)
        vp_ref[0] = jnp.zeros(vp_ref.shape[1:], vp_ref.dtype)

    qi = lax.broadcasted_iota(jnp.int32, (ATT_BLOCK, 2 * ATT_BLOCK), 0)
    kj = lax.broadcasted_iota(jnp.int32, (ATT_BLOCK, 2 * ATT_BLOCK), 1)
    band = (kj >= qi) & (kj <= qi + ATT_BLOCK)
    bias = jnp.where(band, 0.0, -jnp.inf)
    first_key = jnp.where(pl.program_id(1) > 0, 0, ATT_BLOCK)
    bias_head = jnp.where(kj >= first_key, bias, -jnp.inf)
    lane = lax.broadcasted_iota(jnp.int32, (ATT_BLOCK, HEAD_DIM), 1)

    heads = [slice(j * HEAD_DIM, (j + 1) * HEAD_DIM) for j in range(ATT_SLOTS)]

    def scores(r, blk):
        q = q_ref[0, r, blk * ATT_BLOCK:(blk + 1) * ATT_BLOCK, :]
        if blk == 0:
            k = jnp.concatenate([kp_ref[slot_in, r], kc_ref[0, r, 0:ATT_BLOCK, :]], axis=0)
            tile_bias = bias_head
        else:
            k = kc_ref[0, r, (blk - 1) * ATT_BLOCK:(blk + 1) * ATT_BLOCK, :]
            tile_bias = bias
        return [_dot_nt(q[:, sl], k[:, sl]) + tile_bias for sl in heads]

    def softmax(s_list):
        out = []
        for s in s_list:
            m = jnp.max(s, axis=-1, keepdims=True)
            e = jnp.exp(s - m)
            den = jnp.sum(e, axis=-1, keepdims=True)
            out.append((e.astype(BF16), den, m))
        return out

    def weighted_values(r, blk, p_list):
        if blk == 0:
            v = jnp.concatenate([vp_ref[slot_in, r], vc_ref[0, r, 0:ATT_BLOCK, :]], axis=0)
        else:
            v = vc_ref[0, r, (blk - 1) * ATT_BLOCK:(blk + 1) * ATT_BLOCK, :]
        return [_dot(p, v[:, sl]) for (p, _, _), sl in zip(p_list, heads)]

    def store(r, blk, p_list, pv_list):
        if dil == 1:
            rows = pl.ds(blk * ATT_BLOCK, ATT_BLOCK)
        else:
            rows = pl.ds(blk * ATT_BLOCK * dil + r, ATT_BLOCK, stride=dil)
        stats = jnp.ones((ATT_BLOCK, HEAD_DIM), F32)
        for j, ((_, den, m), pv) in enumerate(zip(p_list, pv_list)):
            o_ref[j, rows, :] = pv
            stats = jnp.where(lane == j, m, jnp.where(lane == ATT_SLOTS + j, den, stats))
        o_ref[ATT_SLOTS, rows, :] = stats

    def run_tiles(tiles):
        s_next = scores(*tiles[0])
        for idx, (r, blk) in enumerate(tiles):
            s_cur = s_next
            if idx + 1 < len(tiles):
                s_next = scores(*tiles[idx + 1])
            p_list = softmax(s_cur)
            store(r, blk, p_list, weighted_values(r, blk, p_list))

    res_per_iter = ATT_TILES_PER_ITER // nblk
    if dil == res_per_iter:
        run_tiles([(r, blk) for r in range(dil) for blk in range(nblk)])
    else:
        def body(it, carry):
            run_tiles([(it * res_per_iter + rr, blk) for rr in range(res_per_iter) for blk in range(nblk)])
            return carry
        lax.fori_loop(0, dil // res_per_iter, body, 0)

    last = slice((nblk - 1) * ATT_BLOCK, nblk * ATT_BLOCK)
    kp_ref[slot_out] = kc_ref[0, :, last, :]
    vp_ref[slot_out] = vc_ref[0, :, last, :]


def _attention_call(q, k, v, group, dilation, nblk, batch, seq):
    n = seq // dilation
    steps = n // (nblk * ATT_BLOCK)
    span = nblk * ATT_BLOCK * dilation
    cur = pl.BlockSpec((1, dilation, nblk * ATT_BLOCK, ATT_OUT_WIDTH), lambda b, i: (b, 0, i, 0))
    carry = pltpu.VMEM((2, dilation, ATT_BLOCK, ATT_OUT_WIDTH), BF16)
    return pl.pallas_call(
        functools.partial(_attention_kernel, dilation, nblk),
        grid=(batch, steps),
        in_specs=[cur, cur, cur],
        out_specs=pl.BlockSpec((ATT_SLABS, span, HEAD_DIM), lambda b, i: (0, b * steps + i, 0)),
        out_shape=jax.ShapeDtypeStruct((ATT_SLABS, batch * seq, HEAD_DIM), F32),
        scratch_shapes=[carry, carry],
        compiler_params=_params(("arbitrary", "arbitrary")),
        name=f"attention_g{group}",
    )(q, k, v)


def _mlstm_kernel(mq_ref, mk_ref, mv_ref, gt_ref, o_ref, state_ref, m_ref):
    L = MLSTM_CHUNK
    H = MLSTM_HEADS

    @pl.when(pl.program_id(0) == 0)
    def _():
        state_ref[...] = jnp.zeros_like(state_ref)
        m_ref[...] = jnp.zeros_like(m_ref)

    ti = lax.broadcasted_iota(jnp.int32, (L, L), 0)
    si = lax.broadcasted_iota(jnp.int32, (L, L), 1)
    causal = ti >= si
    tri = causal.astype(F32)
    ones = jnp.ones((L, HEAD_DIM), BF16)

    nbatch = mq_ref.shape[0]
    streams = [divmod(bh, H) for bh in range(nbatch * H)]
    lanes = lambda hd: slice(hd * HEAD_DIM, (hd + 1) * HEAD_DIM)

    gts = [gt_ref[bi] for bi in range(nbatch)]
    b_rows = [lax.dot_general(gt, tri, (((1,), (1,)), ((), ())), precision=lax.Precision.HIGHEST,
                              preferred_element_type=F32) for gt in gts]

    qs = [mq_ref[bi, :, lanes(hd)] for bi, hd in streams]
    ks = [mk_ref[bi, :, lanes(hd)] for bi, hd in streams]
    vexts = [jnp.concatenate([mv_ref[bi, :, lanes(hd)], ones], axis=1) for bi, hd in streams]
    states = [state_ref[bh] for bh in range(len(streams))]
    m_prevs = [m_ref[bh, 0:1, 0:1] for bh in range(len(streams))]
    kts = [k.T for k in ks]
    qk = [_dot(q, kt) for q, kt in zip(qs, kts)]
    q_state = [_dot(q, st.astype(BF16)) for q, st in zip(qs, states)]

    c_rows, upd, decays, m_news = [], [], [], []
    for bh, (bi, hd) in enumerate(streams):
        b_row = b_rows[bi][H + hd:H + hd + 1, :]
        c_row = gts[bi][hd:hd + 1, :] - b_row
        b_last = b_row[:, L - 1:L]
        g_row = b_last + c_row
        m_new = jnp.maximum(b_last + m_prevs[bh], jnp.max(g_row, axis=-1, keepdims=True))
        wk_row = jnp.exp(g_row - m_new)
        c_rows.append(c_row)
        decays.append(jnp.exp(b_last + m_prevs[bh] - m_new))
        m_news.append(m_new)
        upd.append(_dot((kts[bh] * wk_row).astype(BF16), vexts[bh]))

    s_list, a_list, floor_list = [], [], []
    for bh, (bi, hd) in enumerate(streams):
        cm = jnp.where(causal, c_rows[bh], -jnp.inf)
        n_t = jnp.maximum(m_prevs[bh], jnp.max(cm, axis=-1, keepdims=True))
        b_t = jnp.sum(jnp.where(causal, gts[bi][H + hd:H + hd + 1, :], 0.0), axis=-1, keepdims=True)
        s_list.append((qk[bh] * jnp.exp(cm - n_t)).astype(BF16))
        a_list.append(jnp.exp(m_prevs[bh] - n_t))
        floor_list.append(jnp.exp(-(b_t + n_t)))

    sv = [_dot(s, vext) for s, vext in zip(s_list, vexts)]
    for bh, (bi, hd) in enumerate(streams):
        ext = a_list[bh] * q_state[bh] + sv[bh]
        num = ext[:, :HEAD_DIM]
        den = ext[:, HEAD_DIM:]
        o_ref[bi, :, lanes(hd)] = (num / jnp.maximum(jnp.abs(den), floor_list[bh])).astype(o_ref.dtype)
        state_ref[bh] = decays[bh] * states[bh] + upd[bh]
        m_ref[bh] = jnp.broadcast_to(m_news[bh], m_ref.shape[1:])


def _mlstm_call(mq, mk, mv, gt, batch, seq):
    nc = seq // MLSTM_CHUNK
    W = MLSTM_WIDTH
    blk = lambda w: pl.BlockSpec((batch, MLSTM_CHUNK, w), lambda c: (0, c, 0))
    out = pl.pallas_call(
        _mlstm_kernel,
        grid=(nc,),
        in_specs=[blk(W), blk(W), blk(W),
                  pl.BlockSpec((batch, 2 * MLSTM_HEADS, MLSTM_CHUNK), lambda c: (0, 0, c))],
        out_specs=blk(W),
        out_shape=jax.ShapeDtypeStruct((batch, seq, W), BF16),
        scratch_shapes=[pltpu.VMEM((batch * MLSTM_HEADS, HEAD_DIM, 2 * HEAD_DIM), F32),
                        pltpu.VMEM((batch * MLSTM_HEADS, 8, 128), F32)],
        compiler_params=_params(("arbitrary",)),
        name="mlstm",
    )(mq.reshape(batch, seq, W), mk.reshape(batch, seq, W), mv.reshape(batch, seq, W), gt)
    return out.reshape(batch * seq, W)


def _mix_out_kernel(h_ref, a0_ref, a1_ref, a2_ref, ml_ref, g_ref, wg_ref, wua_ref, wub_ref, wo_ref, out_ref):
    a_refs = (a0_ref, a1_ref, a2_ref)
    subs = [slice(s * MIX_SUB, (s + 1) * MIX_SUB) for s in range(h_ref.shape[0] // MIX_SUB)]

    def gates(rows):
        u = _rms_norm(h_ref[rows, :], g_ref[...]).astype(BF16)
        return _dot(u, wg_ref[...])

    def attention_up(rows):
        stats = [a[ATT_SLOTS, rows, :] for a in a_refs]
        mx = jnp.maximum(jnp.maximum(stats[0], stats[1]), stats[2])
        ws = [jnp.exp(s - mx) for s in stats]
        dens = [pltpu.roll(s, HEAD_DIM - ATT_SLOTS, 1) for s in stats]
        inv = 1.0 / (ws[0] * dens[0] + ws[1] * dens[1] + ws[2] * dens[2])
        alphas = [w * inv for w in ws]
        merged_heads = []
        for j in range(ATT_SLOTS):
            acc = alphas[0][:, j:j + 1] * a_refs[0][j, rows, :]
            acc = acc + alphas[1][:, j:j + 1] * a_refs[1][j, rows, :]
            acc = acc + alphas[2][:, j:j + 1] * a_refs[2][j, rows, :]
            merged_heads.append(acc.astype(BF16))
        return _dot(jnp.concatenate(merged_heads, axis=1), wua_ref[...])

    zgs = [gates(rows) for rows in subs]
    y_atts = [attention_up(rows) for rows in subs]
    y_mls = [_dot((jax.nn.sigmoid(zg[:, :MLSTM_WIDTH]) * ml_ref[rows, :]).astype(BF16), wub_ref[...])
             for zg, rows in zip(zgs, subs)]
    for zg, y_att, y_ml, rows in zip(zgs, y_atts, y_mls, subs):
        ga = zg[:, MLSTM_WIDTH:MLSTM_WIDTH + D_MODEL]
        gb = zg[:, MLSTM_WIDTH + D_MODEL:]
        merged = jax.nn.sigmoid(ga) * y_att + jax.nn.sigmoid(gb) * y_ml
        out_ref[rows, :] = h_ref[rows, :] + _dot(merged.astype(BF16), wo_ref[...])


def _mix_out_call(h2d, att, ml, mix_norm, w_in, w_up_att, w_up_mlstm, w_out, tm):
    n_rows = h2d.shape[0]
    g2 = mix_norm.reshape(1, D_MODEL)
    o_mo = 3 * ATT_WIDTH + 3 * MLSTM_WIDTH
    o_g = o_mo + MLSTM_WIDTH + 2 * MLSTM_HEADS
    wg = jnp.concatenate([w_in[:, o_mo:o_mo + MLSTM_WIDTH], w_in[:, o_g:o_g + 2 * D_MODEL]], axis=1).astype(BF16)
    wua = w_up_att.astype(BF16)
    wub = w_up_mlstm.astype(BF16)
    wo = w_out.astype(BF16)
    row = lambda w: pl.BlockSpec((tm, w), lambda i: (i, 0))
    slabs = pl.BlockSpec((ATT_SLABS, tm, HEAD_DIM), lambda i: (0, i, 0))
    return pl.pallas_call(
        _mix_out_kernel,
        grid=(n_rows // tm,),
        in_specs=[row(D_MODEL), slabs, slabs, slabs, row(MLSTM_WIDTH),
                  _resident(g2.shape), _resident(wg.shape), _resident(wua.shape), _resident(wub.shape),
                  _resident(wo.shape)],
        out_specs=row(D_MODEL),
        out_shape=jax.ShapeDtypeStruct((n_rows, D_MODEL), F32),
        compiler_params=_params(("parallel",)),
        name="mix_out",
    )(h2d, *att, ml, g2, wg, wua, wub, wo)


def _layer(x2d, p2d, batch, seq, ffn1_norm, ffn1_w_in, ffn1_w_out, mix_norm, w_in, q_gain, k_gain, conv_w, conv_b,
           i_bias, f_bias, w_up_att, w_up_mlstm, w_out, ffn2_norm, ffn2_w_in, ffn2_w_out, ple_norm, w_ple_gate,
           w_ple_proj):
    tm = 512
    h = _ffn_call(x2d, ffn1_norm, ffn1_w_in, ffn1_w_out, tm)
    qkv, (mq, mk, mv, gt) = _in_proj_call(h, mix_norm, w_in, q_gain, k_gain, conv_w, conv_b, i_bias, f_bias,
                                              batch, seq, 512)
    att = []
    for g, (window, dilation) in enumerate(ATT_GROUPS):
        assert window // dilation == ATT_BLOCK
        att.append(_attention_call(*qkv[g], g, dilation, ATT_BLOCKS_PER_STEP[g], batch, seq))
    ml = _mlstm_call(mq, mk, mv, gt, batch, seq)
    h = _mix_out_call(h, att, ml, mix_norm, w_in, w_up_att, w_up_mlstm, w_out, tm)
    ple = (p2d, ple_norm, w_ple_gate.astype(BF16), w_ple_proj.astype(BF16))
    return _ffn_call(h, ffn2_norm, ffn2_w_in, ffn2_w_out, 1024, ple=ple)


def kernel(x, p, ffn1_norm, ffn1_w_in, ffn1_w_out, mix_norm, w_in, q_gain, k_gain, conv_w, conv_b, i_bias, f_bias,
           w_up_att, w_up_mlstm, w_out, ffn2_norm, ffn2_w_in, ffn2_w_out, ple_norm, w_ple_gate, w_ple_proj):
    batch, seq, d = x.shape
    depth = p.shape[0]
    assert d == D_MODEL and seq % (ATT_GROUPS[-1][1] * ATT_BLOCK) == 0
    h = x.reshape(batch * seq, d)
    for i in range(depth):
        h = _layer(h, p[i].reshape(batch * seq, PLE_DIM), batch, seq, ffn1_norm[i], ffn1_w_in[i], ffn1_w_out[i],
                   mix_norm[i], w_in[i], q_gain[i], k_gain[i], conv_w[i], conv_b[i], i_bias[i], f_bias[i],
                   w_up_att[i], w_up_mlstm[i], w_out[i], ffn2_norm[i], ffn2_w_in[i], ffn2_w_out[i],
                   ple_norm[i], w_ple_gate[i], w_ple_proj[i])
    return h.reshape(batch, seq, d)
```

```python
import functools

import jax
import jax.numpy as jnp
import numpy as np
from jax import lax
from jax.experimental import pallas as pl
from jax.experimental.pallas import tpu as pltpu

D_MODEL = 1024
PLE_DIM = 256
ATT_GROUPS = ((128, 1), (512, 4), (2048, 16))
N_ATT_GROUPS = 3
ATT_SLOTS = 4
ATT_HEADS = N_ATT_GROUPS * ATT_SLOTS
HEAD_DIM = 128
ATT_WIDTH = ATT_HEADS * HEAD_DIM
ATT_OUT_WIDTH = ATT_SLOTS * HEAD_DIM
ROPE_THETA = 500000.0
ROPE_DIM = HEAD_DIM // 4
ROPE_HALF = ROPE_DIM // 2
ROPE_SPLIT = 64
MLSTM_HEADS = 4
MLSTM_WIDTH = MLSTM_HEADS * HEAD_DIM
MLSTM_CHUNK = 128
CONV_WIDTH = 4
D_FF = 2816
NORM_EPS = 1e-6

FF_CHUNKS = ((0, 512), (512, 512), (1024, 512), (1536, 512), (2048, 512), (2560, 256))
ATT_BLOCK = 128
ATT_BLOCKS_PER_STEP = (8, 2, 1)
ATT_TILES_PER_ITER = 8
ATT_SLABS = ATT_SLOTS + 1
PERM_ROWS = 256
MIX_SUB = 256
FFN_SUB = 512
CONV_PAD = 8
V7X_VMEM_LIMIT = 56 * 1024 * 1024

BF16 = jnp.bfloat16
F32 = jnp.float32


def _rms_norm(x, g):
    ms = jnp.mean(x * x, axis=-1, keepdims=True)
    return x * lax.rsqrt(ms + NORM_EPS) * g


def _silu(x):
    return x * jax.nn.sigmoid(x)


def _dot(a, b):
    return jnp.dot(a, b, preferred_element_type=F32)


def _dot_nt(a, b):
    return lax.dot_general(a, b, (((1,), (1,)), ((), ())), preferred_element_type=F32)


def _dot_tn(a, b):
    return lax.dot_general(a, b, (((0,), (0,)), ((), ())), preferred_element_type=F32)


def _resident(shape):
    nd = len(shape)
    return pl.BlockSpec(shape, lambda *_: (0,) * nd, pipeline_mode=pl.Buffered(1))


def _params(semantics):
    return pltpu.CompilerParams(dimension_semantics=semantics, vmem_limit_bytes=V7X_VMEM_LIMIT)


def _ffn_subs(x_ref):
    return [slice(s * FFN_SUB, (s + 1) * FFN_SUB) for s in range(x_ref.shape[0] // FFN_SUB)]


def _swiglu_update(x_ref, g_ref, w_in_ref, w_out_ref, acc_ref, subs):
    us = [_rms_norm(x_ref[rows, :], g_ref[...]).astype(BF16) for rows in subs]
    for idx, (off, width) in enumerate(FF_CHUNKS):
        ups = [(_dot(u, w_in_ref[:, off:off + width]), _dot(u, w_in_ref[:, D_FF + off:D_FF + off + width]))
               for u in us]
        for (a, b), rows in zip(ups, subs):
            part = _dot((_silu(a) * b).astype(BF16), w_out_ref[off:off + width, :])
            if idx == 0:
                acc_ref[rows, :] = part
            else:
                acc_ref[rows, :] += part
    return [x_ref[rows, :] + 0.5 * acc_ref[rows, :] for rows in subs]


def _ffn_kernel(x_ref, g_ref, w_in_ref, w_out_ref, o_ref, acc_ref):
    subs = _ffn_subs(x_ref)
    for h, rows in zip(_swiglu_update(x_ref, g_ref, w_in_ref, w_out_ref, acc_ref, subs), subs):
        o_ref[rows, :] = h


def _ffn_ple_kernel(x_ref, p_ref, g_ref, w_in_ref, w_out_ref, pg_ref, w_pg_ref, w_pe_ref, o_ref, acc_ref):
    subs = _ffn_subs(x_ref)
    hs = _swiglu_update(x_ref, g_ref, w_in_ref, w_out_ref, acc_ref, subs)
    us = [_rms_norm(h, pg_ref[...]).astype(BF16) for h in hs]
    gates = [jax.nn.sigmoid(_dot(u, w_pg_ref[...])) for u in us]
    for h, gate, rows in zip(hs, gates, subs):
        pe = _dot(p_ref[rows, :].astype(BF16), w_pe_ref[...])
        o_ref[rows, :] = h + pe * gate


def _ffn_call(x2d, norm_g, w_in, w_out, tm, ple=None):
    n_rows = x2d.shape[0]
    row = lambda w: pl.BlockSpec((tm, w), lambda i: (i, 0))
    g2 = norm_g.reshape(1, D_MODEL)
    w_in = w_in.astype(BF16)
    w_out = w_out.astype(BF16)
    if ple is None:
        kern = _ffn_kernel
        args = (x2d, g2, w_in, w_out)
        specs = [row(D_MODEL), _resident(g2.shape), _resident(w_in.shape), _resident(w_out.shape)]
        name = "ffn"
    else:
        p2d, ple_g, w_pg, w_pe = ple
        pg2 = ple_g.reshape(1, D_MODEL)
        kern = _ffn_ple_kernel
        args = (x2d, p2d, g2, w_in, w_out, pg2, w_pg, w_pe)
        specs = [row(D_MODEL), row(PLE_DIM), _resident(g2.shape), _resident(w_in.shape),
                 _resident(w_out.shape), _resident(pg2.shape), _resident(w_pg.shape), _resident(w_pe.shape)]
        name = "ffn_ple"
    return pl.pallas_call(
        kern,
        grid=(n_rows // tm,),
        in_specs=specs,
        out_specs=row(D_MODEL),
        out_shape=jax.ShapeDtypeStruct((n_rows, D_MODEL), F32),
        scratch_shapes=[pltpu.VMEM((tm, D_MODEL), F32)],
        compiler_params=_params(("parallel",)),
        name=name,
    )(*args)


def _log_sigmoid(x):
    return jnp.minimum(x, 0.0) - jnp.log1p(jnp.exp(-jnp.abs(x)))


def _in_proj_kernel(h_ref, hp_ref, g_ref, watt_ref, wm_ref, wgt_ref, qg_ref, kg_ref,
                    cos_ref, sin_ref, perm_ref, cw_ref, cb_ref, gbt_ref,
                    q0_ref, k0_ref, v0_ref, q1_ref, k1_ref, v1_ref, q2_ref, k2_ref, v2_ref,
                    mq_ref, mk_ref, mv_ref, gt_ref, xpad_ref, u_ref):
    n_sub = h_ref.shape[0] // PERM_ROWS
    gnorm = g_ref[...]
    qgain = qg_ref[...]
    kgain = kg_ref[...]
    att_refs = ((q0_ref, k0_ref, v0_ref), (q1_ref, k1_ref, v1_ref), (q2_ref, k2_ref, v2_ref))
    sub_rows = lambda sub: slice(sub * PERM_ROWS, (sub + 1) * PERM_ROWS)

    def prepare(sub):
        rows = sub_rows(sub)
        u = _rms_norm(h_ref[rows, :], gnorm).astype(BF16)
        u_ref[0, rows, :] = u
        for grp in range(1, N_ATT_GROUPS):
            u_ref[grp, rows, :] = _dot(perm_ref[grp - 1], u).astype(BF16)

    def qk_head(zh, gain, cos, sin, scale):
        y = _rms_norm(zh, gain)
        y = y * cos + pltpu.roll(y, HEAD_DIM // 2, 1) * sin
        if scale != 1.0:
            y = y * scale
        return y.astype(BF16)

    def store_att(o_ref, val, dil, sub):
        if dil == 1:
            o_ref[0, 0, sub_rows(sub), :] = val
            return
        rows_per_res = PERM_ROWS // dil
        for r in range(dil):
            o_ref[0, r, sub * rows_per_res:(sub + 1) * rows_per_res, :] = val[r * rows_per_res:(r + 1) * rows_per_res, :]

    def att_item(sub, grp, dil, part):
        rows = sub_rows(sub)
        cols = slice(part * ATT_OUT_WIDTH, (part + 1) * ATT_OUT_WIDTH)
        matmul = lambda: _dot(u_ref[grp, rows, :], watt_ref[grp, :, cols])

        def epilogue(z):
            if part == 2:
                val = z.astype(BF16)
            else:
                gain, scale = ((qgain, HEAD_DIM ** -0.5), (kgain, 1.0))[part]
                cos = cos_ref[grp, rows, :]
                sin = sin_ref[grp, rows, :]
                val = jnp.concatenate([qk_head(z[:, j * HEAD_DIM:(j + 1) * HEAD_DIM], gain, cos, sin, scale)
                                       for j in range(ATT_SLOTS)], axis=1)
            store_att(att_refs[grp][part], val, dil, sub)
        return matmul, epilogue

    def conv_item(sub, part, o_ref, scale):
        rows = sub_rows(sub)
        cols = slice(part * MLSTM_WIDTH, (part + 1) * MLSTM_WIDTH)

        def matmul():
            z = _dot(u_ref[0, rows, :], wm_ref[:, cols])
            if sub > 0:
                return z, None
            up = _rms_norm(hp_ref[...], gnorm).astype(BF16)
            return z, _dot(up, wm_ref[:, cols])

        def epilogue(zz):
            z, zp = zz
            if sub == 0:
                zp = jnp.where(pl.program_id(0) > 0, zp, 0.0)
            for c in range(MLSTM_WIDTH // HEAD_DIM):
                slab = part * (MLSTM_WIDTH // HEAD_DIM) + c
                lanes = slice(part * MLSTM_WIDTH + c * HEAD_DIM, part * MLSTM_WIDTH + (c + 1) * HEAD_DIM)
                if sub == 0:
                    xpad_ref[sub, slab, 0:CONV_PAD, :] = zp[:, c * HEAD_DIM:(c + 1) * HEAD_DIM]
                else:
                    xpad_ref[sub, slab, 0:CONV_PAD, :] = xpad_ref[sub - 1, slab, PERM_ROWS:PERM_ROWS + CONV_PAD, :]
                xpad_ref[sub, slab, CONV_PAD:CONV_PAD + PERM_ROWS, :] = z[:, c * HEAD_DIM:(c + 1) * HEAD_DIM]
                acc = cb_ref[:, lanes]
                for j in range(CONV_WIDTH):
                    off = CONV_PAD - (CONV_WIDTH - 1) + j
                    acc = acc + xpad_ref[sub, slab, off:off + PERM_ROWS, :] * cw_ref[j:j + 1, lanes]
                y = _silu(acc)
                if scale != 1.0:
                    y = y * scale
                o_ref[rows, c * HEAD_DIM:(c + 1) * HEAD_DIM] = y.astype(BF16)
        return matmul, epilogue

    def value_gate_item(sub):
        rows = sub_rows(sub)

        def matmul():
            uu = u_ref[0, rows, :]
            return _dot(uu, wm_ref[:, 2 * MLSTM_WIDTH:]), _dot_nt(wgt_ref[...], uu)

        def epilogue(zz):
            zv, zt = zz
            mv_ref[rows, :] = zv.astype(BF16)
            zt = zt + gbt_ref[...]
            head = lax.broadcasted_iota(jnp.int32, zt.shape, 0)
            gt_ref[0, :, rows] = jnp.where(head < MLSTM_HEADS, zt, _log_sigmoid(zt))
        return matmul, epilogue

    def with_prepare(sub, item):
        matmul, epilogue = item

        def prepared_matmul():
            prepare(sub)
            return matmul()
        return prepared_matmul, epilogue

    per_sub = []
    for sub in range(n_sub):
        light = [conv_item(sub, 0, mq_ref, 1.0), conv_item(sub, 1, mk_ref, HEAD_DIM ** -0.5), value_gate_item(sub)]
        seq_items = []
        for grp, (_, dil) in enumerate(ATT_GROUPS):
            seq_items += [att_item(sub, grp, dil, 0), att_item(sub, grp, dil, 2), att_item(sub, grp, dil, 1),
                          light[grp]]
        seq_items[0] = with_prepare(sub, seq_items[0])
        per_sub.append(seq_items)
    items = [per_sub[sub][k] for k in range(len(per_sub[0])) for sub in range(n_sub)]
    pending = items[0][0]()
    for idx, (_, epilogue) in enumerate(items):
        current = pending
        if idx + 1 < len(items):
            pending = items[idx + 1][0]()
        epilogue(current)


def _rope_lane_perm():
    return (list(range(0, ROPE_HALF)) + list(range(ROPE_DIM, ROPE_DIM + 48))
            + list(range(ROPE_HALF, ROPE_DIM)) + list(range(ROPE_DIM + 48, HEAD_DIM)))


def _deinterleave_rows(t, dil):
    n, w = t.shape
    return t.reshape(n // PERM_ROWS, PERM_ROWS // dil, dil, w).transpose(0, 2, 1, 3).reshape(n, w)


def _rope_tables(seq):
    inv_freq = 1.0 / (ROPE_THETA ** (jnp.arange(ROPE_HALF, dtype=F32) / ROPE_HALF))
    rest = jnp.zeros((HEAD_DIM // 2 - ROPE_HALF,), F32)
    freq = jnp.concatenate([inv_freq, rest, inv_freq, rest])
    sign = jnp.concatenate([-jnp.ones((ROPE_HALF,), F32), rest, jnp.ones((ROPE_HALF,), F32), rest])
    coarse = (jnp.arange(seq // ROPE_SPLIT, dtype=F32) * ROPE_SPLIT).reshape(-1, 1) * freq
    fine = jnp.arange(ROPE_SPLIT, dtype=F32).reshape(-1, 1) * freq
    ca, sa = jnp.cos(coarse)[:, None, :], jnp.sin(coarse)[:, None, :]
    cb, sb = jnp.cos(fine)[None, :, :], jnp.sin(fine)[None, :, :]
    cos_t = (ca * cb - sa * sb).reshape(seq, HEAD_DIM)
    sin_t = ((sa * cb + ca * sb) * sign).reshape(seq, HEAD_DIM)
    cos_g = jnp.stack([_deinterleave_rows(cos_t, d) for _, d in ATT_GROUPS])
    sin_g = jnp.stack([_deinterleave_rows(sin_t, d) for _, d in ATT_GROUPS])
    return cos_g, sin_g


def _in_proj_call(h2d, mix_norm, w_in, q_gain, k_gain, conv_w, conv_b, i_bias, f_bias, batch, seq, tm):
    n_rows = h2d.shape[0]
    perm = np.asarray(_rope_lane_perm(), dtype=np.int32)
    slot_perm = (np.arange(ATT_SLOTS, dtype=np.int32)[:, None] * HEAD_DIM + perm[None, :]).reshape(-1)
    wq, wk, wv = (w_in[:, i * ATT_WIDTH:(i + 1) * ATT_WIDTH] for i in range(3))
    w_att = []
    for g in range(N_ATT_GROUPS):
        cols = slice(g * ATT_OUT_WIDTH, (g + 1) * ATT_OUT_WIDTH)
        w_att.append(jnp.concatenate([wq[:, cols][:, slot_perm], wk[:, cols][:, slot_perm], wv[:, cols]], axis=1))
    w_att = jnp.stack(w_att).astype(BF16)
    o = 3 * ATT_WIDTH
    wm = w_in[:, o:o + 3 * MLSTM_WIDTH].astype(BF16)
    o += 4 * MLSTM_WIDTH
    wgt = w_in[:, o:o + 2 * MLSTM_HEADS].T.astype(BF16)
    gbt = jnp.concatenate([i_bias, f_bias]).astype(F32).reshape(2 * MLSTM_HEADS, 1)
    qg = q_gain[perm].reshape(1, HEAD_DIM)
    kg = k_gain[perm].reshape(1, HEAD_DIM)
    cos_g, sin_g = _rope_tables(seq)
    eye = np.eye(PERM_ROWS, dtype=np.float32)
    perms = jnp.asarray(np.stack([
        eye.reshape(PERM_ROWS // d, d, PERM_ROWS).transpose(1, 0, 2).reshape(PERM_ROWS, PERM_ROWS)
        for _, d in ATT_GROUPS[1:]]), dtype=BF16)
    g2 = mix_norm.reshape(1, D_MODEL)
    cb2 = conv_b.reshape(1, 2 * MLSTM_WIDTH)
    tiles_per_seq = seq // tm

    row = lambda w: pl.BlockSpec((tm, w), lambda t, b: (b * tiles_per_seq + t, 0))
    prev_rows = pl.BlockSpec(
        (CONV_PAD, D_MODEL), lambda t, b: (jnp.maximum((b * tiles_per_seq + t) * (tm // CONV_PAD) - 1, 0), 0))
    pos = pl.BlockSpec((N_ATT_GROUPS, tm, HEAD_DIM), lambda t, b: (0, t, 0))
    in_specs = [row(D_MODEL), prev_rows, _resident(g2.shape), _resident(w_att.shape), _resident(wm.shape),
                _resident(wgt.shape), _resident(qg.shape), _resident(kg.shape), pos, pos,
                _resident(perms.shape), _resident(conv_w.shape), _resident(cb2.shape), _resident(gbt.shape)]
    out_specs, out_shape = [], []
    for _, d in ATT_GROUPS:
        spec = pl.BlockSpec((1, d, tm // d, ATT_OUT_WIDTH), lambda t, b: (b, 0, t, 0))
        out_specs += [spec] * 3
        out_shape += [jax.ShapeDtypeStruct((batch, d, seq // d, ATT_OUT_WIDTH), BF16)] * 3
    out_specs += [row(MLSTM_WIDTH)] * 3 + [pl.BlockSpec((1, 2 * MLSTM_HEADS, tm), lambda t, b: (b, 0, t))]
    out_shape += [jax.ShapeDtypeStruct((n_rows, MLSTM_WIDTH), BF16)] * 3 + [
        jax.ShapeDtypeStruct((batch, 2 * MLSTM_HEADS, seq), F32)]
    outs = pl.pallas_call(
        _in_proj_kernel,
        grid=(tiles_per_seq, batch),
        in_specs=in_specs,
        out_specs=out_specs,
        out_shape=out_shape,
        scratch_shapes=[pltpu.VMEM((tm // PERM_ROWS, 2 * MLSTM_WIDTH // HEAD_DIM, PERM_ROWS + CONV_PAD, HEAD_DIM), F32),
                        pltpu.VMEM((N_ATT_GROUPS, tm, D_MODEL), BF16)],
        compiler_params=_params(("parallel", "parallel")),
        name="in_proj",
    )(h2d, h2d, g2, w_att, wm, wgt, qg, kg, cos_g, sin_g, perms, conv_w, cb2, gbt)
    qkv = [tuple(outs[3 * g:3 * g + 3]) for g in range(N_ATT_GROUPS)]
    return qkv, outs[9:]


def _attention_kernel(dil, nblk, q_ref, kc_ref, vc_ref, o_ref, kp_ref, vp_ref):
    slot_in = pl.program_id(1) % 2
    slot_out = 1 - slot_in

    @pl.when(pl.program_id(1) == 0)
    def _():
        kp_ref[0] = jnp.zeros(kp_ref.shape[1:], kp_ref.dtype)
        vp_ref[0] = jnp.zeros(vp_ref.shape[1:], vp_ref.dtype)

    qi = lax.broadcasted_iota(jnp.int32, (ATT_BLOCK, 2 * ATT_BLOCK), 0)
    kj = lax.broadcasted_iota(jnp.int32, (ATT_BLOCK, 2 * ATT_BLOCK), 1)
    band = (kj >= qi) & (kj <= qi + ATT_BLOCK)
    bias = jnp.where(band, 0.0, -jnp.inf)
    first_key = jnp.where(pl.program_id(1) > 0, 0, ATT_BLOCK)
    bias_head = jnp.where(kj >= first_key, bias, -jnp.inf)
    lane = lax.broadcasted_iota(jnp.int32, (ATT_BLOCK, HEAD_DIM), 1)

    heads = [slice(j * HEAD_DIM, (j + 1) * HEAD_DIM) for j in range(ATT_SLOTS)]

    def scores(r, blk):
        q = q_ref[0, r, blk * ATT_BLOCK:(blk + 1) * ATT_BLOCK, :]
        if blk == 0:
            k = jnp.concatenate([kp_ref[slot_in, r], kc_ref[0, r, 0:ATT_BLOCK, :]], axis=0)
            tile_bias = bias_head
        else:
            k = kc_ref[0, r, (blk - 1) * ATT_BLOCK:(blk + 1) * ATT_BLOCK, :]
            tile_bias = bias
        return [_dot_nt(q[:, sl], k[:, sl]) + tile_bias for sl in heads]

    def softmax(s_list):
        out = []
        for s in s_list:
            m = jnp.max(s, axis=-1, keepdims=True)
            e = jnp.exp(s - m)
            den = jnp.sum(e, axis=-1, keepdims=True)
            out.append((e.astype(BF16), den, m))
        return out

    def weighted_values(r, blk, p_list):
        if blk == 0:
            v = jnp.concatenate([vp_ref[slot_in, r], vc_ref[0, r, 0:ATT_BLOCK, :]], axis=0)
        else:
            v = vc_ref[0, r, (blk - 1) * ATT_BLOCK:(blk + 1) * ATT_BLOCK, :]
        return [_dot(p, v[:, sl]) for (p, _, _), sl in zip(p_list, heads)]

    def store(r, blk, p_list, pv_list):
        if dil == 1:
            rows = pl.ds(blk * ATT_BLOCK, ATT_BLOCK)
        else:
            rows = pl.ds(blk * ATT_BLOCK * dil + r, ATT_BLOCK, stride=dil)
        stats = jnp.ones((ATT_BLOCK, HEAD_DIM), F32)
        for j, ((_, den, m), pv) in enumerate(zip(p_list, pv_list)):
            o_ref[j, rows, :] = pv
            stats = jnp.where(lane == j, m, jnp.where(lane == ATT_SLOTS + j, den, stats))
        o_ref[ATT_SLOTS, rows, :] = stats

    def run_tiles(tiles):
        s_next = scores(*tiles[0])
        for idx, (r, blk) in enumerate(tiles):
            s_cur = s_next
            if idx + 1 < len(tiles):
                s_next = scores(*tiles[idx + 1])
            p_list = softmax(s_cur)
            store(r, blk, p_list, weighted_values(r, blk, p_list))

    res_per_iter = ATT_TILES_PER_ITER // nblk
    if dil == res_per_iter:
        run_tiles([(r, blk) for r in range(dil) for blk in range(nblk)])
    else:
        def body(it, carry):
            run_tiles([(it * res_per_iter + rr, blk) for rr in range(res_per_iter) for blk in range(nblk)])
            return carry
        lax.fori_loop(0, dil // res_per_iter, body, 0)

    last = slice((nblk - 1) * ATT_BLOCK, nblk * ATT_BLOCK)
    kp_ref[slot_out] = kc_ref[0, :, last, :]
    vp_ref[slot_out] = vc_ref[0, :, last, :]


def _attention_call(q, k, v, group, dilation, nblk, batch, seq):
    n = seq // dilation
    steps = n // (nblk * ATT_BLOCK)
    span = nblk * ATT_BLOCK * dilation
    cur = pl.BlockSpec((1, dilation, nblk * ATT_BLOCK, ATT_OUT_WIDTH), lambda b, i: (b, 0, i, 0))
    carry = pltpu.VMEM((2, dilation, ATT_BLOCK, ATT_OUT_WIDTH), BF16)
    return pl.pallas_call(
        functools.partial(_attention_kernel, dilation, nblk),
        grid=(batch, steps),
        in_specs=[cur, cur, cur],
        out_specs=pl.BlockSpec((ATT_SLABS, span, HEAD_DIM), lambda b, i: (0, b * steps + i, 0)),
        out_shape=jax.ShapeDtypeStruct((ATT_SLABS, batch * seq, HEAD_DIM), F32),
        scratch_shapes=[carry, carry],
        compiler_params=_params(("arbitrary", "arbitrary")),
        name=f"attention_g{group}",
    )(q, k, v)


def _mlstm_kernel(mq_ref, mk_ref, mv_ref, gt_ref, o_ref, state_ref, m_ref):
    L = MLSTM_CHUNK
    H = MLSTM_HEADS

    @pl.when(pl.program_id(0) == 0)
    def _():
        state_ref[...] = jnp.zeros_like(state_ref)
        m_ref[...] = jnp.zeros_like(m_ref)

    ti = lax.broadcasted_iota(jnp.int32, (L, L), 0)
    si = lax.broadcasted_iota(jnp.int32, (L, L), 1)
    causal = ti >= si
    tri = causal.astype(F32)
    ones = jnp.ones((L, HEAD_DIM), BF16)

    nbatch = mq_ref.shape[0]
    streams = [divmod(bh, H) for bh in range(nbatch * H)]
    lanes = lambda hd: slice(hd * HEAD_DIM, (hd + 1) * HEAD_DIM)

    gts = [gt_ref[bi] for bi in range(nbatch)]
    b_rows = [lax.dot_general(gt, tri, (((1,), (1,)), ((), ())), precision=lax.Precision.HIGHEST,
                              preferred_element_type=F32) for gt in gts]

    qs = [mq_ref[bi, :, lanes(hd)] for bi, hd in streams]
    ks = [mk_ref[bi, :, lanes(hd)] for bi, hd in streams]
    vexts = [jnp.concatenate([mv_ref[bi, :, lanes(hd)], ones], axis=1) for bi, hd in streams]
    states = [state_ref[bh] for bh in range(len(streams))]
    m_prevs = [m_ref[bh, 0:1, 0:1] for bh in range(len(streams))]
    kts = [k.T for k in ks]
    qk = [_dot(q, kt) for q, kt in zip(qs, kts)]
    q_state = [_dot(q, st.astype(BF16)) for q, st in zip(qs, states)]

    c_rows, upd, decays, m_news = [], [], [], []
    for bh, (bi, hd) in enumerate(streams):
        b_row = b_rows[bi][H + hd:H + hd + 1, :]
        c_row = gts[bi][hd:hd + 1, :] - b_row
        b_last = b_row[:, L - 1:L]
        g_row = b_last + c_row
        m_new = jnp.maximum(b_last + m_prevs[bh], jnp.max(g_row, axis=-1, keepdims=True))
        wk_row = jnp.exp(g_row - m_new)
        c_rows.append(c_row)
        decays.append(jnp.exp(b_last + m_prevs[bh] - m_new))
        m_news.append(m_new)
        upd.append(_dot((kts[bh] * wk_row).astype(BF16), vexts[bh]))

    s_list, a_list, floor_list = [], [], []
    for bh, (bi, hd) in enumerate(streams):
        cm = jnp.where(causal, c_rows[bh], -jnp.inf)
        n_t = jnp.maximum(m_prevs[bh], jnp.max(cm, axis=-1, keepdims=True))
        b_t = jnp.sum(jnp.where(causal, gts[bi][H + hd:H + hd + 1, :], 0.0), axis=-1, keepdims=True)
        s_list.append((qk[bh] * jnp.exp(cm - n_t)).astype(BF16))
        a_list.append(jnp.exp(m_prevs[bh] - n_t))
        floor_list.append(jnp.exp(-(b_t + n_t)))

    sv = [_dot(s, vext) for s, vext in zip(s_list, vexts)]
    for bh, (bi, hd) in enumerate(streams):
        ext = a_list[bh] * q_state[bh] + sv[bh]
        num = ext[:, :HEAD_DIM]
        den = ext[:, HEAD_DIM:]
        o_ref[bi, :, lanes(hd)] = (num / jnp.maximum(jnp.abs(den), floor_list[bh])).astype(o_ref.dtype)
        state_ref[bh] = decays[bh] * states[bh] + upd[bh]
        m_ref[bh] = jnp.broadcast_to(m_news[bh], m_ref.shape[1:])


def _mlstm_call(mq, mk, mv, gt, batch, seq):
    nc = seq // MLSTM_CHUNK
    W = MLSTM_WIDTH
    blk = lambda w: pl.BlockSpec((batch, MLSTM_CHUNK, w), lambda c: (0, c, 0))
    out = pl.pallas_call(
        _mlstm_kernel,
        grid=(nc,),
        in_specs=[blk(W), blk(W), blk(W),
                  pl.BlockSpec((batch, 2 * MLSTM_HEADS, MLSTM_CHUNK), lambda c: (0, 0, c))],
        out_specs=blk(W),
        out_shape=jax.ShapeDtypeStruct((batch, seq, W), BF16),
        scratch_shapes=[pltpu.VMEM((batch * MLSTM_HEADS, HEAD_DIM, 2 * HEAD_DIM), F32),
                        pltpu.VMEM((batch * MLSTM_HEADS, 8, 128), F32)],
        compiler_params=_params(("arbitrary",)),
        name="mlstm",
    )(mq.reshape(batch, seq, W), mk.reshape(batch, seq, W), mv.reshape(batch, seq, W), gt)
    return out.reshape(batch * seq, W)


def _mix_out_kernel(h_ref, a0_ref, a1_ref, a2_ref, ml_ref, g_ref, wg_ref, wua_ref, wub_ref, wo_ref, out_ref):
    a_refs = (a0_ref, a1_ref, a2_ref)
    subs = [slice(s * MIX_SUB, (s + 1) * MIX_SUB) for s in range(h_ref.shape[0] // MIX_SUB)]

    def gates(rows):
        u = _rms_norm(h_ref[rows, :], g_ref[...]).astype(BF16)
        return _dot(u, wg_ref[...])

    def attention_up(rows):
        stats = [a[ATT_SLOTS, rows, :] for a in a_refs]
        mx = jnp.maximum(jnp.maximum(stats[0], stats[1]), stats[2])
        ws = [jnp.exp(s - mx) for s in stats]
        dens = [pltpu.roll(s, HEAD_DIM - ATT_SLOTS, 1) for s in stats]
        inv = 1.0 / (ws[0] * dens[0] + ws[1] * dens[1] + ws[2] * dens[2])
        alphas = [w * inv for w in ws]
        merged_heads = []
        for j in range(ATT_SLOTS):
            acc = alphas[0][:, j:j + 1] * a_refs[0][j, rows, :]
            acc = acc + alphas[1][:, j:j + 1] * a_refs[1][j, rows, :]
            acc = acc + alphas[2][:, j:j + 1] * a_refs[2][j, rows, :]
            merged_heads.append(acc.astype(BF16))
        return _dot(jnp.concatenate(merged_heads, axis=1), wua_ref[...])

    zgs = [gates(rows) for rows in subs]
    y_atts = [attention_up(rows) for rows in subs]
    y_mls = [_dot((jax.nn.sigmoid(zg[:, :MLSTM_WIDTH]) * ml_ref[rows, :]).astype(BF16), wub_ref[...])
             for zg, rows in zip(zgs, subs)]
    for zg, y_att, y_ml, rows in zip(zgs, y_atts, y_mls, subs):
        ga = zg[:, MLSTM_WIDTH:MLSTM_WIDTH + D_MODEL]
        gb = zg[:, MLSTM_WIDTH + D_MODEL:]
        merged = jax.nn.sigmoid(ga) * y_att + jax.nn.sigmoid(gb) * y_ml
        out_ref[rows, :] = h_ref[rows, :] + _dot(merged.astype(BF16), wo_ref[...])


def _mix_out_call(h2d, att, ml, mix_norm, w_in, w_up_att, w_up_mlstm, w_out, tm):
    n_rows = h2d.shape[0]
    g2 = mix_norm.reshape(1, D_MODEL)
    o_mo = 3 * ATT_WIDTH + 3 * MLSTM_WIDTH
    o_g = o_mo + MLSTM_WIDTH + 2 * MLSTM_HEADS
    wg = jnp.concatenate([w_in[:, o_mo:o_mo + MLSTM_WIDTH], w_in[:, o_g:o_g + 2 * D_MODEL]], axis=1).astype(BF16)
    wua = w_up_att.astype(BF16)
    wub = w_up_mlstm.astype(BF16)
    wo = w_out.astype(BF16)
    row = lambda w: pl.BlockSpec((tm, w), lambda i: (i, 0))
    slabs = pl.BlockSpec((ATT_SLABS, tm, HEAD_DIM), lambda i: (0, i, 0))
    return pl.pallas_call(
        _mix_out_kernel,
        grid=(n_rows // tm,),
        in_specs=[row(D_MODEL), slabs, slabs, slabs, row(MLSTM_WIDTH),
                  _resident(g2.shape), _resident(wg.shape), _resident(wua.shape), _resident(wub.shape),
                  _resident(wo.shape)],
        out_specs=row(D_MODEL),
        out_shape=jax.ShapeDtypeStruct((n_rows, D_MODEL), F32),
        compiler_params=_params(("parallel",)),
        name="mix_out",
    )(h2d, *att, ml, g2, wg, wua, wub, wo)


def _layer(x2d, p2d, batch, seq, ffn1_norm, ffn1_w_in, ffn1_w_out, mix_norm, w_in, q_gain, k_gain, conv_w, conv_b,
           i_bias, f_bias, w_up_att, w_up_mlstm, w_out, ffn2_norm, ffn2_w_in, ffn2_w_out, ple_norm, w_ple_gate,
           w_ple_proj):
    tm = 512
    h = _ffn_call(x2d, ffn1_norm, ffn1_w_in, ffn1_w_out, 1024)
    qkv, (mq, mk, mv, gt) = _in_proj_call(h, mix_norm, w_in, q_gain, k_gain, conv_w, conv_b, i_bias, f_bias,
                                              batch, seq, 512)
    att = []
    for g, (window, dilation) in enumerate(ATT_GROUPS):
        assert window // dilation == ATT_BLOCK
        att.append(_attention_call(*qkv[g], g, dilation, ATT_BLOCKS_PER_STEP[g], batch, seq))
    ml = _mlstm_call(mq, mk, mv, gt, batch, seq)
    h = _mix_out_call(h, att, ml, mix_norm, w_in, w_up_att, w_up_mlstm, w_out, tm)
    ple = (p2d, ple_norm, w_ple_gate.astype(BF16), w_ple_proj.astype(BF16))
    return _ffn_call(h, ffn2_norm, ffn2_w_in, ffn2_w_out, 1024, ple=ple)


def kernel(x, p, ffn1_norm, ffn1_w_in, ffn1_w_out, mix_norm, w_in, q_gain, k_gain, conv_w, conv_b, i_bias, f_bias,
           w_up_att, w_up_mlstm, w_out, ffn2_norm, ffn2_w_in, ffn2_w_out, ple_norm, w_ple_gate, w_ple_proj):
    batch, seq, d = x.shape
    depth = p.shape[0]
    assert d == D_MODEL and seq % (ATT_GROUPS[-1][1] * ATT_BLOCK) == 0
    h = x.reshape(batch * seq, d)
    for i in range(depth):
        h = _layer(h, p[i].reshape(batch * seq, PLE_DIM), batch, seq, ffn1_norm[i], ffn1_w_in[i], ffn1_w_out[i],
                   mix_norm[i], w_in[i], q_gain[i], k_gain[i], conv_w[i], conv_b[i], i_bias[i], f_bias[i],
                   w_up_att[i], w_up_mlstm[i], w_out[i], ffn2_norm[i], ffn2_w_in[i], ffn2_w_out[i],
                   ple_norm[i], w_ple_gate[i], w_ple_proj[i])
    return h.reshape(batch, seq, d)
```

```python
import functools

import jax
import jax.numpy as jnp
import numpy as np
from jax import lax
from jax.experimental import pallas as pl
from jax.experimental.pallas import tpu as pltpu

D_MODEL = 1024
PLE_DIM = 256
ATT_GROUPS = ((128, 1), (512, 4), (2048, 16))
N_ATT_GROUPS = 3
ATT_SLOTS = 4
ATT_HEADS = N_ATT_GROUPS * ATT_SLOTS
HEAD_DIM = 128
ATT_WIDTH = ATT_HEADS * HEAD_DIM
ATT_OUT_WIDTH = ATT_SLOTS * HEAD_DIM
ROPE_THETA = 500000.0
ROPE_DIM = HEAD_DIM // 4
ROPE_HALF = ROPE_DIM // 2
ROPE_SPLIT = 64
MLSTM_HEADS = 4
MLSTM_WIDTH = MLSTM_HEADS * HEAD_DIM
MLSTM_CHUNK = 128
CONV_WIDTH = 4
D_FF = 2816
NORM_EPS = 1e-6

FF_CHUNKS = ((0, 512), (512, 512), (1024, 512), (1536, 512), (2048, 512), (2560, 256))
ATT_BLOCK = 128
ATT_BLOCKS_PER_STEP = (8, 2, 1)
ATT_TILES_PER_ITER = 8
ATT_SLABS = ATT_SLOTS + 1
PERM_ROWS = 256
MIX_SUB = 256
FFN_SUB = 256
CONV_PAD = 8
V7X_VMEM_LIMIT = 56 * 1024 * 1024

BF16 = jnp.bfloat16
F32 = jnp.float32


def _rms_norm(x, g):
    ms = jnp.mean(x * x, axis=-1, keepdims=True)
    return x * lax.rsqrt(ms + NORM_EPS) * g


def _silu(x):
    return x * jax.nn.sigmoid(x)


def _dot(a, b):
    return jnp.dot(a, b, preferred_element_type=F32)


def _dot_nt(a, b):
    return lax.dot_general(a, b, (((1,), (1,)), ((), ())), preferred_element_type=F32)


def _dot_tn(a, b):
    return lax.dot_general(a, b, (((0,), (0,)), ((), ())), preferred_element_type=F32)


def _resident(shape):
    nd = len(shape)
    return pl.BlockSpec(shape, lambda *_: (0,) * nd, pipeline_mode=pl.Buffered(1))


def _params(semantics):
    return pltpu.CompilerParams(dimension_semantics=semantics, vmem_limit_bytes=V7X_VMEM_LIMIT)


def _ffn_subs(x_ref):
    return [slice(s * FFN_SUB, (s + 1) * FFN_SUB) for s in range(x_ref.shape[0] // FFN_SUB)]


def _swiglu_update(x_ref, g_ref, w_in_ref, w_out_ref, acc_ref, subs):
    us = [_rms_norm(x_ref[rows, :], g_ref[...]).astype(BF16) for rows in subs]
    for idx, (off, width) in enumerate(FF_CHUNKS):
        ups = [(_dot(u, w_in_ref[:, off:off + width]), _dot(u, w_in_ref[:, D_FF + off:D_FF + off + width]))
               for u in us]
        for (a, b), rows in zip(ups, subs):
            part = _dot((_silu(a) * b).astype(BF16), w_out_ref[off:off + width, :])
            if idx == 0:
                acc_ref[rows, :] = part
            else:
                acc_ref[rows, :] += part
    return [x_ref[rows, :] + 0.5 * acc_ref[rows, :] for rows in subs]


def _ffn_kernel(x_ref, g_ref, w_in_ref, w_out_ref, o_ref, acc_ref):
    subs = _ffn_subs(x_ref)
    for h, rows in zip(_swiglu_update(x_ref, g_ref, w_in_ref, w_out_ref, acc_ref, subs), subs):
        o_ref[rows, :] = h


def _ffn_ple_kernel(x_ref, p_ref, g_ref, w_in_ref, w_out_ref, pg_ref, w_pg_ref, w_pe_ref, o_ref, acc_ref):
    subs = _ffn_subs(x_ref)
    hs = _swiglu_update(x_ref, g_ref, w_in_ref, w_out_ref, acc_ref, subs)
    us = [_rms_norm(h, pg_ref[...]).astype(BF16) for h in hs]
    gates = [jax.nn.sigmoid(_dot(u, w_pg_ref[...])) for u in us]
    for h, gate, rows in zip(hs, gates, subs):
        pe = _dot(p_ref[rows, :].astype(BF16), w_pe_ref[...])
        o_ref[rows, :] = h + pe * gate


def _ffn_call(x2d, norm_g, w_in, w_out, tm, ple=None):
    n_rows = x2d.shape[0]
    row = lambda w: pl.BlockSpec((tm, w), lambda i: (i, 0))
    g2 = norm_g.reshape(1, D_MODEL)
    w_in = w_in.astype(BF16)
    w_out = w_out.astype(BF16)
    if ple is None:
        kern = _ffn_kernel
        args = (x2d, g2, w_in, w_out)
        specs = [row(D_MODEL), _resident(g2.shape), _resident(w_in.shape), _resident(w_out.shape)]
        name = "ffn"
    else:
        p2d, ple_g, w_pg, w_pe = ple
        pg2 = ple_g.reshape(1, D_MODEL)
        kern = _ffn_ple_kernel
        args = (x2d, p2d, g2, w_in, w_out, pg2, w_pg, w_pe)
        specs = [row(D_MODEL), row(PLE_DIM), _resident(g2.shape), _resident(w_in.shape),
                 _resident(w_out.shape), _resident(pg2.shape), _resident(w_pg.shape), _resident(w_pe.shape)]
        name = "ffn_ple"
    return pl.pallas_call(
        kern,
        grid=(n_rows // tm,),
        in_specs=specs,
        out_specs=row(D_MODEL),
        out_shape=jax.ShapeDtypeStruct((n_rows, D_MODEL), F32),
        scratch_shapes=[pltpu.VMEM((tm, D_MODEL), F32)],
        compiler_params=_params(("parallel",)),
        name=name,
    )(*args)


def _log_sigmoid(x):
    return jnp.minimum(x, 0.0) - jnp.log1p(jnp.exp(-jnp.abs(x)))


def _in_proj_kernel(h_ref, hp_ref, g_ref, watt_ref, wm_ref, wgt_ref, qg_ref, kg_ref,
                    cos_ref, sin_ref, perm_ref, cw_ref, cb_ref, gbt_ref,
                    q0_ref, k0_ref, v0_ref, q1_ref, k1_ref, v1_ref, q2_ref, k2_ref, v2_ref,
                    mq_ref, mk_ref, mv_ref, gt_ref, xpad_ref, u_ref):
    n_sub = h_ref.shape[0] // PERM_ROWS
    gnorm = g_ref[...]
    qgain = qg_ref[...]
    kgain = kg_ref[...]
    att_refs = ((q0_ref, k0_ref, v0_ref), (q1_ref, k1_ref, v1_ref), (q2_ref, k2_ref, v2_ref))
    sub_rows = lambda sub: slice(sub * PERM_ROWS, (sub + 1) * PERM_ROWS)

    def prepare(sub):
        rows = sub_rows(sub)
        u = _rms_norm(h_ref[rows, :], gnorm).astype(BF16)
        u_ref[0, rows, :] = u
        for grp in range(1, N_ATT_GROUPS):
            u_ref[grp, rows, :] = _dot(perm_ref[grp - 1], u).astype(BF16)

    def qk_head(zh, gain, cos, sin, scale):
        y = _rms_norm(zh, gain)
        y = y * cos + pltpu.roll(y, HEAD_DIM // 2, 1) * sin
        if scale != 1.0:
            y = y * scale
        return y.astype(BF16)

    def store_att(o_ref, val, dil, sub):
        if dil == 1:
            o_ref[0, 0, sub_rows(sub), :] = val
            return
        rows_per_res = PERM_ROWS // dil
        for r in range(dil):
            o_ref[0, r, sub * rows_per_res:(sub + 1) * rows_per_res, :] = val[r * rows_per_res:(r + 1) * rows_per_res, :]

    def att_item(sub, grp, dil, part):
        rows = sub_rows(sub)
        cols = slice(part * ATT_OUT_WIDTH, (part + 1) * ATT_OUT_WIDTH)
        matmul = lambda: _dot(u_ref[grp, rows, :], watt_ref[grp, :, cols])

        def epilogue(z):
            if part == 2:
                val = z.astype(BF16)
            else:
                gain, scale = ((qgain, HEAD_DIM ** -0.5), (kgain, 1.0))[part]
                cos = cos_ref[grp, rows, :]
                sin = sin_ref[grp, rows, :]
                val = jnp.concatenate([qk_head(z[:, j * HEAD_DIM:(j + 1) * HEAD_DIM], gain, cos, sin, scale)
                                       for j in range(ATT_SLOTS)], axis=1)
            store_att(att_refs[grp][part], val, dil, sub)
        return matmul, epilogue

    def conv_item(sub, part, o_ref, scale):
        rows = sub_rows(sub)
        cols = slice(part * MLSTM_WIDTH, (part + 1) * MLSTM_WIDTH)

        def matmul():
            z = _dot(u_ref[0, rows, :], wm_ref[:, cols])
            if sub > 0:
                return z, None
            up = _rms_norm(hp_ref[...], gnorm).astype(BF16)
            return z, _dot(up, wm_ref[:, cols])

        def epilogue(zz):
            z, zp = zz
            if sub == 0:
                zp = jnp.where(pl.program_id(0) > 0, zp, 0.0)
            for c in range(MLSTM_WIDTH // HEAD_DIM):
                slab = part * (MLSTM_WIDTH // HEAD_DIM) + c
                lanes = slice(part * MLSTM_WIDTH + c * HEAD_DIM, part * MLSTM_WIDTH + (c + 1) * HEAD_DIM)
                if sub == 0:
                    xpad_ref[sub, slab, 0:CONV_PAD, :] = zp[:, c * HEAD_DIM:(c + 1) * HEAD_DIM]
                else:
                    xpad_ref[sub, slab, 0:CONV_PAD, :] = xpad_ref[sub - 1, slab, PERM_ROWS:PERM_ROWS + CONV_PAD, :]
                xpad_ref[sub, slab, CONV_PAD:CONV_PAD + PERM_ROWS, :] = z[:, c * HEAD_DIM:(c + 1) * HEAD_DIM]
                acc = cb_ref[:, lanes]
                for j in range(CONV_WIDTH):
                    off = CONV_PAD - (CONV_WIDTH - 1) + j
                    acc = acc + xpad_ref[sub, slab, off:off + PERM_ROWS, :] * cw_ref[j:j + 1, lanes]
                y = _silu(acc)
                if scale != 1.0:
                    y = y * scale
                o_ref[rows, c * HEAD_DIM:(c + 1) * HEAD_DIM] = y.astype(BF16)
        return matmul, epilogue

    def value_gate_item(sub):
        rows = sub_rows(sub)

        def matmul():
            uu = u_ref[0, rows, :]
            return _dot(uu, wm_ref[:, 2 * MLSTM_WIDTH:]), _dot_nt(wgt_ref[...], uu)

        def epilogue(zz):
            zv, zt = zz
            mv_ref[rows, :] = zv.astype(BF16)
            zt = zt + gbt_ref[...]
            head = lax.broadcasted_iota(jnp.int32, zt.shape, 0)
            gt_ref[0, :, rows] = jnp.where(head < MLSTM_HEADS, zt, _log_sigmoid(zt))
        return matmul, epilogue

    def with_prepare(sub, item):
        matmul, epilogue = item

        def prepared_matmul():
            prepare(sub)
            return matmul()
        return prepared_matmul, epilogue

    per_sub = []
    for sub in range(n_sub):
        light = [conv_item(sub, 0, mq_ref, 1.0), conv_item(sub, 1, mk_ref, HEAD_DIM ** -0.5), value_gate_item(sub)]
        seq_items = []
        for grp, (_, dil) in enumerate(ATT_GROUPS):
            seq_items += [att_item(sub, grp, dil, 0), att_item(sub, grp, dil, 2), att_item(sub, grp, dil, 1),
                          light[grp]]
        seq_items[0] = with_prepare(sub, seq_items[0])
        per_sub.append(seq_items)
    items = [per_sub[sub][k] for k in range(len(per_sub[0])) for sub in range(n_sub)]
    pending = items[0][0]()
    for idx, (_, epilogue) in enumerate(items):
        current = pending
        if idx + 1 < len(items):
            pending = items[idx + 1][0]()
        epilogue(current)


def _rope_lane_perm():
    return (list(range(0, ROPE_HALF)) + list(range(ROPE_DIM, ROPE_DIM + 48))
            + list(range(ROPE_HALF, ROPE_DIM)) + list(range(ROPE_DIM + 48, HEAD_DIM)))


def _deinterleave_rows(t, dil):
    n, w = t.shape
    return t.reshape(n // PERM_ROWS, PERM_ROWS // dil, dil, w).transpose(0, 2, 1, 3).reshape(n, w)


def _rope_tables(seq):
    inv_freq = 1.0 / (ROPE_THETA ** (jnp.arange(ROPE_HALF, dtype=F32) / ROPE_HALF))
    rest = jnp.zeros((HEAD_DIM // 2 - ROPE_HALF,), F32)
    freq = jnp.concatenate([inv_freq, rest, inv_freq, rest])
    sign = jnp.concatenate([-jnp.ones((ROPE_HALF,), F32), rest, jnp.ones((ROPE_HALF,), F32), rest])
    coarse = (jnp.arange(seq // ROPE_SPLIT, dtype=F32) * ROPE_SPLIT).reshape(-1, 1) * freq
    fine = jnp.arange(ROPE_SPLIT, dtype=F32).reshape(-1, 1) * freq
    ca, sa = jnp.cos(coarse)[:, None, :], jnp.sin(coarse)[:, None, :]
    cb, sb = jnp.cos(fine)[None, :, :], jnp.sin(fine)[None, :, :]
    cos_t = (ca * cb - sa * sb).reshape(seq, HEAD_DIM)
    sin_t = ((sa * cb + ca * sb) * sign).reshape(seq, HEAD_DIM)
    cos_g = jnp.stack([_deinterleave_rows(cos_t, d) for _, d in ATT_GROUPS])
    sin_g = jnp.stack([_deinterleave_rows(sin_t, d) for _, d in ATT_GROUPS])
    return cos_g, sin_g


def _in_proj_call(h2d, mix_norm, w_in, q_gain, k_gain, conv_w, conv_b, i_bias, f_bias, batch, seq, tm):
    n_rows = h2d.shape[0]
    perm = np.asarray(_rope_lane_perm(), dtype=np.int32)
    slot_perm = (np.arange(ATT_SLOTS, dtype=np.int32)[:, None] * HEAD_DIM + perm[None, :]).reshape(-1)
    wq, wk, wv = (w_in[:, i * ATT_WIDTH:(i + 1) * ATT_WIDTH] for i in range(3))
    w_att = []
    for g in range(N_ATT_GROUPS):
        cols = slice(g * ATT_OUT_WIDTH, (g + 1) * ATT_OUT_WIDTH)
        w_att.append(jnp.concatenate([wq[:, cols][:, slot_perm], wk[:, cols][:, slot_perm], wv[:, cols]], axis=1))
    w_att = jnp.stack(w_att).astype(BF16)
    o = 3 * ATT_WIDTH
    wm = w_in[:, o:o + 3 * MLSTM_WIDTH].astype(BF16)
    o += 4 * MLSTM_WIDTH
    wgt = w_in[:, o:o + 2 * MLSTM_HEADS].T.astype(BF16)
    gbt = jnp.concatenate([i_bias, f_bias]).astype(F32).reshape(2 * MLSTM_HEADS, 1)
    qg = q_gain[perm].reshape(1, HEAD_DIM)
    kg = k_gain[perm].reshape(1, HEAD_DIM)
    cos_g, sin_g = _rope_tables(seq)
    eye = np.eye(PERM_ROWS, dtype=np.float32)
    perms = jnp.asarray(np.stack([
        eye.reshape(PERM_ROWS // d, d, PERM_ROWS).transpose(1, 0, 2).reshape(PERM_ROWS, PERM_ROWS)
        for _, d in ATT_GROUPS[1:]]), dtype=BF16)
    g2 = mix_norm.reshape(1, D_MODEL)
    cb2 = conv_b.reshape(1, 2 * MLSTM_WIDTH)
    tiles_per_seq = seq // tm

    row = lambda w: pl.BlockSpec((tm, w), lambda t, b: (b * tiles_per_seq + t, 0))
    prev_rows = pl.BlockSpec(
        (CONV_PAD, D_MODEL), lambda t, b: (jnp.maximum((b * tiles_per_seq + t) * (tm // CONV_PAD) - 1, 0), 0))
    pos = pl.BlockSpec((N_ATT_GROUPS, tm, HEAD_DIM), lambda t, b: (0, t, 0))
    in_specs = [row(D_MODEL), prev_rows, _resident(g2.shape), _resident(w_att.shape), _resident(wm.shape),
                _resident(wgt.shape), _resident(qg.shape), _resident(kg.shape), pos, pos,
                _resident(perms.shape), _resident(conv_w.shape), _resident(cb2.shape), _resident(gbt.shape)]
    out_specs, out_shape = [], []
    for _, d in ATT_GROUPS:
        spec = pl.BlockSpec((1, d, tm // d, ATT_OUT_WIDTH), lambda t, b: (b, 0, t, 0))
        out_specs += [spec] * 3
        out_shape += [jax.ShapeDtypeStruct((batch, d, seq // d, ATT_OUT_WIDTH), BF16)] * 3
    out_specs += [row(MLSTM_WIDTH)] * 3 + [pl.BlockSpec((1, 2 * MLSTM_HEADS, tm), lambda t, b: (b, 0, t))]
    out_shape += [jax.ShapeDtypeStruct((n_rows, MLSTM_WIDTH), BF16)] * 3 + [
        jax.ShapeDtypeStruct((batch, 2 * MLSTM_HEADS, seq), F32)]
    outs = pl.pallas_call(
        _in_proj_kernel,
        grid=(tiles_per_seq, batch),
        in_specs=in_specs,
        out_specs=out_specs,
        out_shape=out_shape,
        scratch_shapes=[pltpu.VMEM((tm // PERM_ROWS, 2 * MLSTM_WIDTH // HEAD_DIM, PERM_ROWS + CONV_PAD, HEAD_DIM), F32),
                        pltpu.VMEM((N_ATT_GROUPS, tm, D_MODEL), BF16)],
        compiler_params=_params(("parallel", "parallel")),
        name="in_proj",
    )(h2d, h2d, g2, w_att, wm, wgt, qg, kg, cos_g, sin_g, perms, conv_w, cb2, gbt)
    qkv = [tuple(outs[3 * g:3 * g + 3]) for g in range(N_ATT_GROUPS)]
    return qkv, outs[9:]


def _attention_kernel(dil, nblk, q_ref, kc_ref, vc_ref, o_ref, kp_ref, vp_ref):
    slot_in = pl.program_id(1) % 2
    slot_out = 1 - slot_in

    @pl.when(pl.program_id(1) == 0)
    def _():
        kp_ref[0] = jnp.zeros(kp_ref.shape[1:], kp_ref.dtype)
        vp_ref[0] = jnp.zeros(vp_ref.shape[1:], vp_ref.dtype)

    qi = lax.broadcasted_iota(jnp.int32, (ATT_BLOCK, 2 * ATT_BLOCK), 0)
    kj = lax.broadcasted_iota(jnp.int32, (ATT_BLOCK, 2 * ATT_BLOCK), 1)
    band = (kj >= qi) & (kj <= qi + ATT_BLOCK)
    bias = jnp.where(band, 0.0, -jnp.inf)
    first_key = jnp.where(pl.program_id(1) > 0, 0, ATT_BLOCK)
    bias_head = jnp.where(kj >= first_key, bias, -jnp.inf)
    lane = lax.broadcasted_iota(jnp.int32, (ATT_BLOCK, HEAD_DIM), 1)

    heads = [slice(j * HEAD_DIM, (j + 1) * HEAD_DIM) for j in range(ATT_SLOTS)]

    def scores(r, blk):
        q = q_ref[0, r, blk * ATT_BLOCK:(blk + 1) * ATT_BLOCK, :]
        if blk == 0:
            k = jnp.concatenate([kp_ref[slot_in, r], kc_ref[0, r, 0:ATT_BLOCK, :]], axis=0)
            tile_bias = bias_head
        else:
            k = kc_ref[0, r, (blk - 1) * ATT_BLOCK:(blk + 1) * ATT_BLOCK, :]
            tile_bias = bias
        return [_dot_nt(q[:, sl], k[:, sl]) + tile_bias for sl in heads]

    def softmax(s_list):
        out = []
        for s in s_list:
            m = jnp.max(s, axis=-1, keepdims=True)
            e = jnp.exp(s - m)
            den = jnp.sum(e, axis=-1, keepdims=True)
            out.append((e.astype(BF16), den, m))
        return out

    def weighted_values(r, blk, p_list):
        if blk == 0:
            v = jnp.concatenate([vp_ref[slot_in, r], vc_ref[0, r, 0:ATT_BLOCK, :]], axis=0)
        else:
            v = vc_ref[0, r, (blk - 1) * ATT_BLOCK:(blk + 1) * ATT_BLOCK, :]
        return [_dot(p, v[:, sl]) for (p, _, _), sl in zip(p_list, heads)]

    def store(r, blk, p_list, pv_list):
        if dil == 1:
            rows = pl.ds(blk * ATT_BLOCK, ATT_BLOCK)
        else:
            rows = pl.ds(blk * ATT_BLOCK * dil + r, ATT_BLOCK, stride=dil)
        stats = jnp.ones((ATT_BLOCK, HEAD_DIM), F32)
        for j, ((_, den, m), pv) in enumerate(zip(p_list, pv_list)):
            o_ref[j, rows, :] = pv
            stats = jnp.where(lane == j, m, jnp.where(lane == ATT_SLOTS + j, den, stats))
        o_ref[ATT_SLOTS, rows, :] = stats

    def run_tiles(tiles):
        s_next = scores(*tiles[0])
        for idx, (r, blk) in enumerate(tiles):
            s_cur = s_next
            if idx + 1 < len(tiles):
                s_next = scores(*tiles[idx + 1])
            p_list = softmax(s_cur)
            store(r, blk, p_list, weighted_values(r, blk, p_list))

    res_per_iter = ATT_TILES_PER_ITER // nblk
    if dil == res_per_iter:
        run_tiles([(r, blk) for r in range(dil) for blk in range(nblk)])
    else:
        def body(it, carry):
            run_tiles([(it * res_per_iter + rr, blk) for rr in range(res_per_iter) for blk in range(nblk)])
            return carry
        lax.fori_loop(0, dil // res_per_iter, body, 0)

    last = slice((nblk - 1) * ATT_BLOCK, nblk * ATT_BLOCK)
    kp_ref[slot_out] = kc_ref[0, :, last, :]
    vp_ref[slot_out] = vc_ref[0, :, last, :]


def _attention_call(q, k, v, group, dilation, nblk, batch, seq):
    n = seq // dilation
    steps = n // (nblk * ATT_BLOCK)
    span = nblk * ATT_BLOCK * dilation
    cur = pl.BlockSpec((1, dilation, nblk * ATT_BLOCK, ATT_OUT_WIDTH), lambda b, i: (b, 0, i, 0))
    carry = pltpu.VMEM((2, dilation, ATT_BLOCK, ATT_OUT_WIDTH), BF16)
    return pl.pallas_call(
        functools.partial(_attention_kernel, dilation, nblk),
        grid=(batch, steps),
        in_specs=[cur, cur, cur],
        out_specs=pl.BlockSpec((ATT_SLABS, span, HEAD_DIM), lambda b, i: (0, b * steps + i, 0)),
        out_shape=jax.ShapeDtypeStruct((ATT_SLABS, batch * seq, HEAD_DIM), F32),
        scratch_shapes=[carry, carry],
        compiler_params=_params(("arbitrary", "arbitrary")),
        name=f"attention_g{group}",
    )(q, k, v)


def _mlstm_kernel(mq_ref, mk_ref, mv_ref, gt_ref, o_ref, state_ref, m_ref):
    L = MLSTM_CHUNK
    H = MLSTM_HEADS

    @pl.when(pl.program_id(0) == 0)
    def _():
        state_ref[...] = jnp.zeros_like(state_ref)
        m_ref[...] = jnp.zeros_like(m_ref)

    ti = lax.broadcasted_iota(jnp.int32, (L, L), 0)
    si = lax.broadcasted_iota(jnp.int32, (L, L), 1)
    causal = ti >= si
    tri = causal.astype(F32)
    ones = jnp.ones((L, HEAD_DIM), BF16)

    nbatch = mq_ref.shape[0]
    streams = [divmod(bh, H) for bh in range(nbatch * H)]
    lanes = lambda hd: slice(hd * HEAD_DIM, (hd + 1) * HEAD_DIM)

    gts = [gt_ref[bi] for bi in range(nbatch)]
    b_rows = [lax.dot_general(gt, tri, (((1,), (1,)), ((), ())), precision=lax.Precision.HIGHEST,
                              preferred_element_type=F32) for gt in gts]

    qs = [mq_ref[bi, :, lanes(hd)] for bi, hd in streams]
    ks = [mk_ref[bi, :, lanes(hd)] for bi, hd in streams]
    vexts = [jnp.concatenate([mv_ref[bi, :, lanes(hd)], ones], axis=1) for bi, hd in streams]
    states = [state_ref[bh] for bh in range(len(streams))]
    m_prevs = [m_ref[bh, 0:1, 0:1] for bh in range(len(streams))]
    kts = [k.T for k in ks]
    qk = [_dot(q, kt) for q, kt in zip(qs, kts)]
    q_state = [_dot(q, st.astype(BF16)) for q, st in zip(qs, states)]

    c_rows, upd, decays, m_news = [], [], [], []
    for bh, (bi, hd) in enumerate(streams):
        b_row = b_rows[bi][H + hd:H + hd + 1, :]
        c_row = gts[bi][hd:hd + 1, :] - b_row
        b_last = b_row[:, L - 1:L]
        g_row = b_last + c_row
        m_new = jnp.maximum(b_last + m_prevs[bh], jnp.max(g_row, axis=-1, keepdims=True))
        wk_row = jnp.exp(g_row - m_new)
        c_rows.append(c_row)
        decays.append(jnp.exp(b_last + m_prevs[bh] - m_new))
        m_news.append(m_new)
        upd.append(_dot((kts[bh] * wk_row).astype(BF16), vexts[bh]))

    s_list, a_list, floor_list = [], [], []
    for bh, (bi, hd) in enumerate(streams):
        cm = jnp.where(causal, c_rows[bh], -jnp.inf)
        n_t = jnp.maximum(m_prevs[bh], jnp.max(cm, axis=-1, keepdims=True))
        b_t = jnp.sum(jnp.where(causal, gts[bi][H + hd:H + hd + 1, :], 0.0), axis=-1, keepdims=True)
        s_list.append((qk[bh] * jnp.exp(cm - n_t)).astype(BF16))
        a_list.append(jnp.exp(m_prevs[bh] - n_t))
        floor_list.append(jnp.exp(-(b_t + n_t)))

    sv = [_dot(s, vext) for s, vext in zip(s_list, vexts)]
    for bh, (bi, hd) in enumerate(streams):
        ext = a_list[bh] * q_state[bh] + sv[bh]
        num = ext[:, :HEAD_DIM]
        den = ext[:, HEAD_DIM:]
        o_ref[bi, :, lanes(hd)] = (num / jnp.maximum(jnp.abs(den), floor_list[bh])).astype(o_ref.dtype)
        state_ref[bh] = decays[bh] * states[bh] + upd[bh]
        m_ref[bh] = jnp.broadcast_to(m_news[bh], m_ref.shape[1:])


def _mlstm_call(mq, mk, mv, gt, batch, seq):
    nc = seq // MLSTM_CHUNK
    W = MLSTM_WIDTH
    blk = lambda w: pl.BlockSpec((batch, MLSTM_CHUNK, w), lambda c: (0, c, 0))
    out = pl.pallas_call(
        _mlstm_kernel,
        grid=(nc,),
        in_specs=[blk(W), blk(W), blk(W),
                  pl.BlockSpec((batch, 2 * MLSTM_HEADS, MLSTM_CHUNK), lambda c: (0, 0, c))],
        out_specs=blk(W),
        out_shape=jax.ShapeDtypeStruct((batch, seq, W), BF16),
        scratch_shapes=[pltpu.VMEM((batch * MLSTM_HEADS, HEAD_DIM, 2 * HEAD_DIM), F32),
                        pltpu.VMEM((batch * MLSTM_HEADS, 8, 128), F32)],
        compiler_params=_params(("arbitrary",)),
        name="mlstm",
    )(mq.reshape(batch, seq, W), mk.reshape(batch, seq, W), mv.reshape(batch, seq, W), gt)
    return out.reshape(batch * seq, W)


def _mix_out_kernel(h_ref, a0_ref, a1_ref, a2_ref, ml_ref, g_ref, wg_ref, wua_ref, wub_ref, wo_ref, out_ref):
    a_refs = (a0_ref, a1_ref, a2_ref)
    subs = [slice(s * MIX_SUB, (s + 1) * MIX_SUB) for s in range(h_ref.shape[0] // MIX_SUB)]

    def gates(rows):
        u = _rms_norm(h_ref[rows, :], g_ref[...]).astype(BF16)
        return _dot(u, wg_ref[...])

    def attention_up(rows):
        stats = [a[ATT_SLOTS, rows, :] for a in a_refs]
        mx = jnp.maximum(jnp.maximum(stats[0], stats[1]), stats[2])
        ws = [jnp.exp(s - mx) for s in stats]
        dens = [pltpu.roll(s, HEAD_DIM - ATT_SLOTS, 1) for s in stats]
        inv = 1.0 / (ws[0] * dens[0] + ws[1] * dens[1] + ws[2] * dens[2])
        alphas = [w * inv for w in ws]
        merged_heads = []
        for j in range(ATT_SLOTS):
            acc = alphas[0][:, j:j + 1] * a_refs[0][j, rows, :]
            acc = acc + alphas[1][:, j:j + 1] * a_refs[1][j, rows, :]
            acc = acc + alphas[2][:, j:j + 1] * a_refs[2][j, rows, :]
            merged_heads.append(acc.astype(BF16))
        return _dot(jnp.concatenate(merged_heads, axis=1), wua_ref[...])

    zgs = [gates(rows) for rows in subs]
    y_atts = [attention_up(rows) for rows in subs]
    y_mls = [_dot((jax.nn.sigmoid(zg[:, :MLSTM_WIDTH]) * ml_ref[rows, :]).astype(BF16), wub_ref[...])
             for zg, rows in zip(zgs, subs)]
    for zg, y_att, y_ml, rows in zip(zgs, y_atts, y_mls, subs):
        ga = zg[:, MLSTM_WIDTH:MLSTM_WIDTH + D_MODEL]
        gb = zg[:, MLSTM_WIDTH + D_MODEL:]
        merged = jax.nn.sigmoid(ga) * y_att + jax.nn.sigmoid(gb) * y_ml
        out_ref[rows, :] = h_ref[rows, :] + _dot(merged.astype(BF16), wo_ref[...])


def _mix_out_call(h2d, att, ml, mix_norm, w_in, w_up_att, w_up_mlstm, w_out, tm):
    n_rows = h2d.shape[0]
    g2 = mix_norm.reshape(1, D_MODEL)
    o_mo = 3 * ATT_WIDTH + 3 * MLSTM_WIDTH
    o_g = o_mo + MLSTM_WIDTH + 2 * MLSTM_HEADS
    wg = jnp.concatenate([w_in[:, o_mo:o_mo + MLSTM_WIDTH], w_in[:, o_g:o_g + 2 * D_MODEL]], axis=1).astype(BF16)
    wua = w_up_att.astype(BF16)
    wub = w_up_mlstm.astype(BF16)
    wo = w_out.astype(BF16)
    row = lambda w: pl.BlockSpec((tm, w), lambda i: (i, 0))
    slabs = pl.BlockSpec((ATT_SLABS, tm, HEAD_DIM), lambda i: (0, i, 0))
    return pl.pallas_call(
        _mix_out_kernel,
        grid=(n_rows // tm,),
        in_specs=[row(D_MODEL), slabs, slabs, slabs, row(MLSTM_WIDTH),
                  _resident(g2.shape), _resident(wg.shape), _resident(wua.shape), _resident(wub.shape),
                  _resident(wo.shape)],
        out_specs=row(D_MODEL),
        out_shape=jax.ShapeDtypeStruct((n_rows, D_MODEL), F32),
        compiler_params=_params(("parallel",)),
        name="mix_out",
    )(h2d, *att, ml, g2, wg, wua, wub, wo)


def _layer(x2d, p2d, batch, seq, ffn1_norm, ffn1_w_in, ffn1_w_out, mix_norm, w_in, q_gain, k_gain, conv_w, conv_b,
           i_bias, f_bias, w_up_att, w_up_mlstm, w_out, ffn2_norm, ffn2_w_in, ffn2_w_out, ple_norm, w_ple_gate,
           w_ple_proj):
    tm = 512
    h = _ffn_call(x2d, ffn1_norm, ffn1_w_in, ffn1_w_out, 1024)
    qkv, (mq, mk, mv, gt) = _in_proj_call(h, mix_norm, w_in, q_gain, k_gain, conv_w, conv_b, i_bias, f_bias,
                                              batch, seq, 512)
    att = []
    for g, (window, dilation) in enumerate(ATT_GROUPS):
        assert window // dilation == ATT_BLOCK
        att.append(_attention_call(*qkv[g], g, dilation, ATT_BLOCKS_PER_STEP[g], batch, seq))
    ml = _mlstm_call(mq, mk, mv, gt, batch, seq)
    h = _mix_out_call(h, att, ml, mix_norm, w_in, w_up_att, w_up_mlstm, w_out, tm)
    ple = (p2d, ple_norm, w_ple_gate.astype(BF16), w_ple_proj.astype(BF16))
    return _ffn_call(h, ffn2_norm, ffn2_w_in, ffn2_w_out, 1024, ple=ple)


def kernel(x, p, ffn1_norm, ffn1_w_in, ffn1_w_out, mix_norm, w_in, q_gain, k_gain, conv_w, conv_b, i_bias, f_bias,
           w_up_att, w_up_mlstm, w_out, ffn2_norm, ffn2_w_in, ffn2_w_out, ple_norm, w_ple_gate, w_ple_proj):
    batch, seq, d = x.shape
    depth = p.shape[0]
    assert d == D_MODEL and seq % (ATT_GROUPS[-1][1] * ATT_BLOCK) == 0
    h = x.reshape(batch * seq, d)
    for i in range(depth):
        h = _layer(h, p[i].reshape(batch * seq, PLE_DIM), batch, seq, ffn1_norm[i], ffn1_w_in[i], ffn1_w_out[i],
                   mix_norm[i], w_in[i], q_gain[i], k_gain[i], conv_w[i], conv_b[i], i_bias[i], f_bias[i],
                   w_up_att[i], w_up_mlstm[i], w_out[i], ffn2_norm[i], ffn2_w_in[i], ffn2_w_out[i],
                   ple_norm[i], w_ple_gate[i], w_ple_proj[i])
    return h.reshape(batch, seq, d)
```

```python
import functools

import jax
import jax.numpy as jnp
import numpy as np
from jax import lax
from jax.experimental import pallas as pl
from jax.experimental.pallas import tpu as pltpu

D_MODEL = 1024
PLE_DIM = 256
ATT_GROUPS = ((128, 1), (512, 4), (2048, 16))
N_ATT_GROUPS = 3
ATT_SLOTS = 4
ATT_HEADS = N_ATT_GROUPS * ATT_SLOTS
HEAD_DIM = 128
ATT_WIDTH = ATT_HEADS * HEAD_DIM
ATT_OUT_WIDTH = ATT_SLOTS * HEAD_DIM
ROPE_THETA = 500000.0
ROPE_DIM = HEAD_DIM // 4
ROPE_HALF = ROPE_DIM // 2
ROPE_SPLIT = 64
MLSTM_HEADS = 4
MLSTM_WIDTH = MLSTM_HEADS * HEAD_DIM
MLSTM_CHUNK = 128
CONV_WIDTH = 4
D_FF = 2816
NORM_EPS = 1e-6

FF_CHUNKS = ((0, 512), (512, 512), (1024, 512), (1536, 512), (2048, 512), (2560, 256))
ATT_BLOCK = 128
ATT_BLOCKS_PER_STEP = (8, 2, 2)
ATT_TILES_PER_ITER = 8
ATT_SLABS = ATT_SLOTS + 1
PERM_ROWS = 256
MIX_SUB = 256
FFN_SUB = 256
CONV_PAD = 8
V7X_VMEM_LIMIT = 56 * 1024 * 1024

BF16 = jnp.bfloat16
F32 = jnp.float32


def _rms_norm(x, g):
    ms = jnp.mean(x * x, axis=-1, keepdims=True)
    return x * lax.rsqrt(ms + NORM_EPS) * g


def _silu(x):
    return x * jax.nn.sigmoid(x)


def _dot(a, b):
    return jnp.dot(a, b, preferred_element_type=F32)


def _dot_nt(a, b):
    return lax.dot_general(a, b, (((1,), (1,)), ((), ())), preferred_element_type=F32)


def _dot_tn(a, b):
    return lax.dot_general(a, b, (((0,), (0,)), ((), ())), preferred_element_type=F32)


def _resident(shape):
    nd = len(shape)
    return pl.BlockSpec(shape, lambda *_: (0,) * nd, pipeline_mode=pl.Buffered(1))


def _params(semantics):
    return pltpu.CompilerParams(dimension_semantics=semantics, vmem_limit_bytes=V7X_VMEM_LIMIT)


def _ffn_subs(x_ref):
    return [slice(s * FFN_SUB, (s + 1) * FFN_SUB) for s in range(x_ref.shape[0] // FFN_SUB)]


def _swiglu_update(x_ref, g_ref, w_in_ref, w_out_ref, acc_ref, subs):
    us = [_rms_norm(x_ref[rows, :], g_ref[...]).astype(BF16) for rows in subs]
    for idx, (off, width) in enumerate(FF_CHUNKS):
        ups = [(_dot(u, w_in_ref[:, off:off + width]), _dot(u, w_in_ref[:, D_FF + off:D_FF + off + width]))
               for u in us]
        for (a, b), rows in zip(ups, subs):
            part = _dot((_silu(a) * b).astype(BF16), w_out_ref[off:off + width, :])
            if idx == 0:
                acc_ref[rows, :] = part
            else:
                acc_ref[rows, :] += part
    return [x_ref[rows, :] + 0.5 * acc_ref[rows, :] for rows in subs]


def _ffn_kernel(x_ref, g_ref, w_in_ref, w_out_ref, o_ref, acc_ref):
    subs = _ffn_subs(x_ref)
    for h, rows in zip(_swiglu_update(x_ref, g_ref, w_in_ref, w_out_ref, acc_ref, subs), subs):
        o_ref[rows, :] = h


def _ffn_ple_kernel(x_ref, p_ref, g_ref, w_in_ref, w_out_ref, pg_ref, w_pg_ref, w_pe_ref, o_ref, acc_ref):
    subs = _ffn_subs(x_ref)
    hs = _swiglu_update(x_ref, g_ref, w_in_ref, w_out_ref, acc_ref, subs)
    us = [_rms_norm(h, pg_ref[...]).astype(BF16) for h in hs]
    gates = [jax.nn.sigmoid(_dot(u, w_pg_ref[...])) for u in us]
    for h, gate, rows in zip(hs, gates, subs):
        pe = _dot(p_ref[rows, :].astype(BF16), w_pe_ref[...])
        o_ref[rows, :] = h + pe * gate


def _ffn_call(x2d, norm_g, w_in, w_out, tm, ple=None):
    n_rows = x2d.shape[0]
    row = lambda w: pl.BlockSpec((tm, w), lambda i: (i, 0))
    g2 = norm_g.reshape(1, D_MODEL)
    w_in = w_in.astype(BF16)
    w_out = w_out.astype(BF16)
    if ple is None:
        kern = _ffn_kernel
        args = (x2d, g2, w_in, w_out)
        specs = [row(D_MODEL), _resident(g2.shape), _resident(w_in.shape), _resident(w_out.shape)]
        name = "ffn"
    else:
        p2d, ple_g, w_pg, w_pe = ple
        pg2 = ple_g.reshape(1, D_MODEL)
        kern = _ffn_ple_kernel
        args = (x2d, p2d, g2, w_in, w_out, pg2, w_pg, w_pe)
        specs = [row(D_MODEL), row(PLE_DIM), _resident(g2.shape), _resident(w_in.shape),
                 _resident(w_out.shape), _resident(pg2.shape), _resident(w_pg.shape), _resident(w_pe.shape)]
        name = "ffn_ple"
    return pl.pallas_call(
        kern,
        grid=(n_rows // tm,),
        in_specs=specs,
        out_specs=row(D_MODEL),
        out_shape=jax.ShapeDtypeStruct((n_rows, D_MODEL), F32),
        scratch_shapes=[pltpu.VMEM((tm, D_MODEL), F32)],
        compiler_params=_params(("parallel",)),
        name=name,
    )(*args)


def _log_sigmoid(x):
    return jnp.minimum(x, 0.0) - jnp.log1p(jnp.exp(-jnp.abs(x)))


def _in_proj_kernel(h_ref, hp_ref, g_ref, watt_ref, wm_ref, wgt_ref, qg_ref, kg_ref,
                    cos_ref, sin_ref, perm_ref, cw_ref, cb_ref, gbt_ref,
                    q0_ref, k0_ref, v0_ref, q1_ref, k1_ref, v1_ref, q2_ref, k2_ref, v2_ref,
                    mq_ref, mk_ref, mv_ref, gt_ref, xpad_ref, u_ref):
    n_sub = h_ref.shape[0] // PERM_ROWS
    gnorm = g_ref[...]
    qgain = qg_ref[...]
    kgain = kg_ref[...]
    att_refs = ((q0_ref, k0_ref, v0_ref), (q1_ref, k1_ref, v1_ref), (q2_ref, k2_ref, v2_ref))
    sub_rows = lambda sub: slice(sub * PERM_ROWS, (sub + 1) * PERM_ROWS)

    def prepare(sub):
        rows = sub_rows(sub)
        u = _rms_norm(h_ref[rows, :], gnorm).astype(BF16)
        u_ref[0, rows, :] = u
        for grp in range(1, N_ATT_GROUPS):
            u_ref[grp, rows, :] = _dot(perm_ref[grp - 1], u).astype(BF16)

    def qk_head(zh, gain, cos, sin, scale):
        y = _rms_norm(zh, gain)
        y = y * cos + pltpu.roll(y, HEAD_DIM // 2, 1) * sin
        if scale != 1.0:
            y = y * scale
        return y.astype(BF16)

    def store_att(o_ref, val, dil, sub):
        if dil == 1:
            o_ref[0, 0, sub_rows(sub), :] = val
            return
        rows_per_res = PERM_ROWS // dil
        for r in range(dil):
            o_ref[0, r, sub * rows_per_res:(sub + 1) * rows_per_res, :] = val[r * rows_per_res:(r + 1) * rows_per_res, :]

    def att_item(sub, grp, dil, part):
        rows = sub_rows(sub)
        cols = slice(part * ATT_OUT_WIDTH, (part + 1) * ATT_OUT_WIDTH)
        matmul = lambda: _dot(u_ref[grp, rows, :], watt_ref[grp, :, cols])

        def epilogue(z):
            if part == 2:
                val = z.astype(BF16)
            else:
                gain, scale = ((qgain, HEAD_DIM ** -0.5), (kgain, 1.0))[part]
                cos = cos_ref[grp, rows, :]
                sin = sin_ref[grp, rows, :]
                val = jnp.concatenate([qk_head(z[:, j * HEAD_DIM:(j + 1) * HEAD_DIM], gain, cos, sin, scale)
                                       for j in range(ATT_SLOTS)], axis=1)
            store_att(att_refs[grp][part], val, dil, sub)
        return matmul, epilogue

    def conv_item(sub, part, o_ref, scale):
        rows = sub_rows(sub)
        cols = slice(part * MLSTM_WIDTH, (part + 1) * MLSTM_WIDTH)

        def matmul():
            z = _dot(u_ref[0, rows, :], wm_ref[:, cols])
            if sub > 0:
                return z, None
            up = _rms_norm(hp_ref[...], gnorm).astype(BF16)
            return z, _dot(up, wm_ref[:, cols])

        def epilogue(zz):
            z, zp = zz
            if sub == 0:
                zp = jnp.where(pl.program_id(0) > 0, zp, 0.0)
            for c in range(MLSTM_WIDTH // HEAD_DIM):
                slab = part * (MLSTM_WIDTH // HEAD_DIM) + c
                lanes = slice(part * MLSTM_WIDTH + c * HEAD_DIM, part * MLSTM_WIDTH + (c + 1) * HEAD_DIM)
                if sub == 0:
                    xpad_ref[sub, slab, 0:CONV_PAD, :] = zp[:, c * HEAD_DIM:(c + 1) * HEAD_DIM]
                else:
                    xpad_ref[sub, slab, 0:CONV_PAD, :] = xpad_ref[sub - 1, slab, PERM_ROWS:PERM_ROWS + CONV_PAD, :]
                xpad_ref[sub, slab, CONV_PAD:CONV_PAD + PERM_ROWS, :] = z[:, c * HEAD_DIM:(c + 1) * HEAD_DIM]
                acc = cb_ref[:, lanes]
                for j in range(CONV_WIDTH):
                    off = CONV_PAD - (CONV_WIDTH - 1) + j
                    acc = acc + xpad_ref[sub, slab, off:off + PERM_ROWS, :] * cw_ref[j:j + 1, lanes]
                y = _silu(acc)
                if scale != 1.0:
                    y = y * scale
                o_ref[rows, c * HEAD_DIM:(c + 1) * HEAD_DIM] = y.astype(BF16)
        return matmul, epilogue

    def value_gate_item(sub):
        rows = sub_rows(sub)

        def matmul():
            uu = u_ref[0, rows, :]
            return _dot(uu, wm_ref[:, 2 * MLSTM_WIDTH:]), _dot_nt(wgt_ref[...], uu)

        def epilogue(zz):
            zv, zt = zz
            mv_ref[rows, :] = zv.astype(BF16)
            zt = zt + gbt_ref[...]
            head = lax.broadcasted_iota(jnp.int32, zt.shape, 0)
            gt_ref[0, :, rows] = jnp.where(head < MLSTM_HEADS, zt, _log_sigmoid(zt))
        return matmul, epilogue

    def with_prepare(sub, item):
        matmul, epilogue = item

        def prepared_matmul():
            prepare(sub)
            return matmul()
        return prepared_matmul, epilogue

    per_sub = []
    for sub in range(n_sub):
        light = [conv_item(sub, 0, mq_ref, 1.0), conv_item(sub, 1, mk_ref, HEAD_DIM ** -0.5), value_gate_item(sub)]
        seq_items = []
        for grp, (_, dil) in enumerate(ATT_GROUPS):
            seq_items += [att_item(sub, grp, dil, 0), att_item(sub, grp, dil, 2), att_item(sub, grp, dil, 1),
                          light[grp]]
        seq_items[0] = with_prepare(sub, seq_items[0])
        per_sub.append(seq_items)
    items = [per_sub[sub][k] for k in range(len(per_sub[0])) for sub in range(n_sub)]
    pending = items[0][0]()
    for idx, (_, epilogue) in enumerate(items):
        current = pending
        if idx + 1 < len(items):
            pending = items[idx + 1][0]()
        epilogue(current)


def _rope_lane_perm():
    return (list(range(0, ROPE_HALF)) + list(range(ROPE_DIM, ROPE_DIM + 48))
            + list(range(ROPE_HALF, ROPE_DIM)) + list(range(ROPE_DIM + 48, HEAD_DIM)))


def _deinterleave_rows(t, dil):
    n, w = t.shape
    return t.reshape(n // PERM_ROWS, PERM_ROWS // dil, dil, w).transpose(0, 2, 1, 3).reshape(n, w)


def _rope_tables(seq):
    inv_freq = 1.0 / (ROPE_THETA ** (jnp.arange(ROPE_HALF, dtype=F32) / ROPE_HALF))
    rest = jnp.zeros((HEAD_DIM // 2 - ROPE_HALF,), F32)
    freq = jnp.concatenate([inv_freq, rest, inv_freq, rest])
    sign = jnp.concatenate([-jnp.ones((ROPE_HALF,), F32), rest, jnp.ones((ROPE_HALF,), F32), rest])
    coarse = (jnp.arange(seq // ROPE_SPLIT, dtype=F32) * ROPE_SPLIT).reshape(-1, 1) * freq
    fine = jnp.arange(ROPE_SPLIT, dtype=F32).reshape(-1, 1) * freq
    ca, sa = jnp.cos(coarse)[:, None, :], jnp.sin(coarse)[:, None, :]
    cb, sb = jnp.cos(fine)[None, :, :], jnp.sin(fine)[None, :, :]
    cos_t = (ca * cb - sa * sb).reshape(seq, HEAD_DIM)
    sin_t = ((sa * cb + ca * sb) * sign).reshape(seq, HEAD_DIM)
    cos_g = jnp.stack([_deinterleave_rows(cos_t, d) for _, d in ATT_GROUPS])
    sin_g = jnp.stack([_deinterleave_rows(sin_t, d) for _, d in ATT_GROUPS])
    return cos_g, sin_g


def _in_proj_call(h2d, mix_norm, w_in, q_gain, k_gain, conv_w, conv_b, i_bias, f_bias, batch, seq, tm):
    n_rows = h2d.shape[0]
    perm = np.asarray(_rope_lane_perm(), dtype=np.int32)
    slot_perm = (np.arange(ATT_SLOTS, dtype=np.int32)[:, None] * HEAD_DIM + perm[None, :]).reshape(-1)
    wq, wk, wv = (w_in[:, i * ATT_WIDTH:(i + 1) * ATT_WIDTH] for i in range(3))
    w_att = []
    for g in range(N_ATT_GROUPS):
        cols = slice(g * ATT_OUT_WIDTH, (g + 1) * ATT_OUT_WIDTH)
        w_att.append(jnp.concatenate([wq[:, cols][:, slot_perm], wk[:, cols][:, slot_perm], wv[:, cols]], axis=1))
    w_att = jnp.stack(w_att).astype(BF16)
    o = 3 * ATT_WIDTH
    wm = w_in[:, o:o + 3 * MLSTM_WIDTH].astype(BF16)
    o += 4 * MLSTM_WIDTH
    wgt = w_in[:, o:o + 2 * MLSTM_HEADS].T.astype(BF16)
    gbt = jnp.concatenate([i_bias, f_bias]).astype(F32).reshape(2 * MLSTM_HEADS, 1)
    qg = q_gain[perm].reshape(1, HEAD_DIM)
    kg = k_gain[perm].reshape(1, HEAD_DIM)
    cos_g, sin_g = _rope_tables(seq)
    eye = np.eye(PERM_ROWS, dtype=np.float32)
    perms = jnp.asarray(np.stack([
        eye.reshape(PERM_ROWS // d, d, PERM_ROWS).transpose(1, 0, 2).reshape(PERM_ROWS, PERM_ROWS)
        for _, d in ATT_GROUPS[1:]]), dtype=BF16)
    g2 = mix_norm.reshape(1, D_MODEL)
    cb2 = conv_b.reshape(1, 2 * MLSTM_WIDTH)
    tiles_per_seq = seq // tm

    row = lambda w: pl.BlockSpec((tm, w), lambda t, b: (b * tiles_per_seq + t, 0))
    prev_rows = pl.BlockSpec(
        (CONV_PAD, D_MODEL), lambda t, b: (jnp.maximum((b * tiles_per_seq + t) * (tm // CONV_PAD) - 1, 0), 0))
    pos = pl.BlockSpec((N_ATT_GROUPS, tm, HEAD_DIM), lambda t, b: (0, t, 0))
    in_specs = [row(D_MODEL), prev_rows, _resident(g2.shape), _resident(w_att.shape), _resident(wm.shape),
                _resident(wgt.shape), _resident(qg.shape), _resident(kg.shape), pos, pos,
                _resident(perms.shape), _resident(conv_w.shape), _resident(cb2.shape), _resident(gbt.shape)]
    out_specs, out_shape = [], []
    for _, d in ATT_GROUPS:
        spec = pl.BlockSpec((1, d, tm // d, ATT_OUT_WIDTH), lambda t, b: (b, 0, t, 0))
        out_specs += [spec] * 3
        out_shape += [jax.ShapeDtypeStruct((batch, d, seq // d, ATT_OUT_WIDTH), BF16)] * 3
    out_specs += [row(MLSTM_WIDTH)] * 3 + [pl.BlockSpec((1, 2 * MLSTM_HEADS, tm), lambda t, b: (b, 0, t))]
    out_shape += [jax.ShapeDtypeStruct((n_rows, MLSTM_WIDTH), BF16)] * 3 + [
        jax.ShapeDtypeStruct((batch, 2 * MLSTM_HEADS, seq), F32)]
    outs = pl.pallas_call(
        _in_proj_kernel,
        grid=(tiles_per_seq, batch),
        in_specs=in_specs,
        out_specs=out_specs,
        out_shape=out_shape,
        scratch_shapes=[pltpu.VMEM((tm // PERM_ROWS, 2 * MLSTM_WIDTH // HEAD_DIM, PERM_ROWS + CONV_PAD, HEAD_DIM), F32),
                        pltpu.VMEM((N_ATT_GROUPS, tm, D_MODEL), BF16)],
        compiler_params=_params(("parallel", "parallel")),
        name="in_proj",
    )(h2d, h2d, g2, w_att, wm, wgt, qg, kg, cos_g, sin_g, perms, conv_w, cb2, gbt)
    qkv = [tuple(outs[3 * g:3 * g + 3]) for g in range(N_ATT_GROUPS)]
    return qkv, outs[9:]


def _attention_kernel(dil, nblk, q_ref, kc_ref, vc_ref, o_ref, kp_ref, vp_ref):
    slot_in = pl.program_id(1) % 2
    slot_out = 1 - slot_in

    @pl.when(pl.program_id(1) == 0)
    def _():
        kp_ref[0] = jnp.zeros(kp_ref.shape[1:], kp_ref.dtype)
        vp_ref[0] = jnp.zeros(vp_ref.shape[1:], vp_ref.dtype)

    qi = lax.broadcasted_iota(jnp.int32, (ATT_BLOCK, 2 * ATT_BLOCK), 0)
    kj = lax.broadcasted_iota(jnp.int32, (ATT_BLOCK, 2 * ATT_BLOCK), 1)
    band = (kj >= qi) & (kj <= qi + ATT_BLOCK)
    bias = jnp.where(band, 0.0, -jnp.inf)
    first_key = jnp.where(pl.program_id(1) > 0, 0, ATT_BLOCK)
    bias_head = jnp.where(kj >= first_key, bias, -jnp.inf)
    lane = lax.broadcasted_iota(jnp.int32, (ATT_BLOCK, HEAD_DIM), 1)

    heads = [slice(j * HEAD_DIM, (j + 1) * HEAD_DIM) for j in range(ATT_SLOTS)]

    def scores(r, blk):
        q = q_ref[0, r, blk * ATT_BLOCK:(blk + 1) * ATT_BLOCK, :]
        if blk == 0:
            k = jnp.concatenate([kp_ref[slot_in, r], kc_ref[0, r, 0:ATT_BLOCK, :]], axis=0)
            tile_bias = bias_head
        else:
            k = kc_ref[0, r, (blk - 1) * ATT_BLOCK:(blk + 1) * ATT_BLOCK, :]
            tile_bias = bias
        return [_dot_nt(q[:, sl], k[:, sl]) + tile_bias for sl in heads]

    def softmax(s_list):
        out = []
        for s in s_list:
            m = jnp.max(s, axis=-1, keepdims=True)
            e = jnp.exp(s - m)
            den = jnp.sum(e, axis=-1, keepdims=True)
            out.append((e.astype(BF16), den, m))
        return out

    def weighted_values(r, blk, p_list):
        if blk == 0:
            v = jnp.concatenate([vp_ref[slot_in, r], vc_ref[0, r, 0:ATT_BLOCK, :]], axis=0)
        else:
            v = vc_ref[0, r, (blk - 1) * ATT_BLOCK:(blk + 1) * ATT_BLOCK, :]
        return [_dot(p, v[:, sl]) for (p, _, _), sl in zip(p_list, heads)]

    def store(r, blk, p_list, pv_list):
        if dil == 1:
            rows = pl.ds(blk * ATT_BLOCK, ATT_BLOCK)
        else:
            rows = pl.ds(blk * ATT_BLOCK * dil + r, ATT_BLOCK, stride=dil)
        stats = jnp.ones((ATT_BLOCK, HEAD_DIM), F32)
        for j, ((_, den, m), pv) in enumerate(zip(p_list, pv_list)):
            o_ref[j, rows, :] = pv
            stats = jnp.where(lane == j, m, jnp.where(lane == ATT_SLOTS + j, den, stats))
        o_ref[ATT_SLOTS, rows, :] = stats

    def run_tiles(tiles):
        s_next = scores(*tiles[0])
        for idx, (r, blk) in enumerate(tiles):
            s_cur = s_next
            if idx + 1 < len(tiles):
                s_next = scores(*tiles[idx + 1])
            p_list = softmax(s_cur)
            store(r, blk, p_list, weighted_values(r, blk, p_list))

    res_per_iter = ATT_TILES_PER_ITER // nblk
    if dil == res_per_iter:
        run_tiles([(r, blk) for r in range(dil) for blk in range(nblk)])
    else:
        def body(it, carry):
            run_tiles([(it * res_per_iter + rr, blk) for rr in range(res_per_iter) for blk in range(nblk)])
            return carry
        lax.fori_loop(0, dil // res_per_iter, body, 0)

    last = slice((nblk - 1) * ATT_BLOCK, nblk * ATT_BLOCK)
    kp_ref[slot_out] = kc_ref[0, :, last, :]
    vp_ref[slot_out] = vc_ref[0, :, last, :]


def _attention_call(q, k, v, group, dilation, nblk, batch, seq):
    n = seq // dilation
    steps = n // (nblk * ATT_BLOCK)
    span = nblk * ATT_BLOCK * dilation
    cur = pl.BlockSpec((1, dilation, nblk * ATT_BLOCK, ATT_OUT_WIDTH), lambda b, i: (b, 0, i, 0))
    carry = pltpu.VMEM((2, dilation, ATT_BLOCK, ATT_OUT_WIDTH), BF16)
    return pl.pallas_call(
        functools.partial(_attention_kernel, dilation, nblk),
        grid=(batch, steps),
        in_specs=[cur, cur, cur],
        out_specs=pl.BlockSpec((ATT_SLABS, span, HEAD_DIM), lambda b, i: (0, b * steps + i, 0)),
        out_shape=jax.ShapeDtypeStruct((ATT_SLABS, batch * seq, HEAD_DIM), F32),
        scratch_shapes=[carry, carry],
        compiler_params=_params(("arbitrary", "arbitrary")),
        name=f"attention_g{group}",
    )(q, k, v)


def _mlstm_kernel(mq_ref, mk_ref, mv_ref, gt_ref, o_ref, state_ref, m_ref):
    L = MLSTM_CHUNK
    H = MLSTM_HEADS

    @pl.when(pl.program_id(0) == 0)
    def _():
        state_ref[...] = jnp.zeros_like(state_ref)
        m_ref[...] = jnp.zeros_like(m_ref)

    ti = lax.broadcasted_iota(jnp.int32, (L, L), 0)
    si = lax.broadcasted_iota(jnp.int32, (L, L), 1)
    causal = ti >= si
    tri = causal.astype(F32)
    ones = jnp.ones((L, HEAD_DIM), BF16)

    nbatch = mq_ref.shape[0]
    streams = [divmod(bh, H) for bh in range(nbatch * H)]
    lanes = lambda hd: slice(hd * HEAD_DIM, (hd + 1) * HEAD_DIM)

    gts = [gt_ref[bi] for bi in range(nbatch)]
    b_rows = [lax.dot_general(gt, tri, (((1,), (1,)), ((), ())), precision=lax.Precision.HIGHEST,
                              preferred_element_type=F32) for gt in gts]

    qs = [mq_ref[bi, :, lanes(hd)] for bi, hd in streams]
    ks = [mk_ref[bi, :, lanes(hd)] for bi, hd in streams]
    vexts = [jnp.concatenate([mv_ref[bi, :, lanes(hd)], ones], axis=1) for bi, hd in streams]
    states = [state_ref[bh] for bh in range(len(streams))]
    m_prevs = [m_ref[bh, 0:1, 0:1] for bh in range(len(streams))]
    kts = [k.T for k in ks]
    qk = [_dot(q, kt) for q, kt in zip(qs, kts)]
    q_state = [_dot(q, st.astype(BF16)) for q, st in zip(qs, states)]

    c_rows, upd, decays, m_news = [], [], [], []
    for bh, (bi, hd) in enumerate(streams):
        b_row = b_rows[bi][H + hd:H + hd + 1, :]
        c_row = gts[bi][hd:hd + 1, :] - b_row
        b_last = b_row[:, L - 1:L]
        g_row = b_last + c_row
        m_new = jnp.maximum(b_last + m_prevs[bh], jnp.max(g_row, axis=-1, keepdims=True))
        wk_row = jnp.exp(g_row - m_new)
        c_rows.append(c_row)
        decays.append(jnp.exp(b_last + m_prevs[bh] - m_new))
        m_news.append(m_new)
        upd.append(_dot((kts[bh] * wk_row).astype(BF16), vexts[bh]))

    s_list, a_list, floor_list = [], [], []
    for bh, (bi, hd) in enumerate(streams):
        cm = jnp.where(causal, c_rows[bh], -jnp.inf)
        n_t = jnp.maximum(m_prevs[bh], jnp.max(cm, axis=-1, keepdims=True))
        b_t = jnp.sum(jnp.where(causal, gts[bi][H + hd:H + hd + 1, :], 0.0), axis=-1, keepdims=True)
        s_list.append((qk[bh] * jnp.exp(cm - n_t)).astype(BF16))
        a_list.append(jnp.exp(m_prevs[bh] - n_t))
        floor_list.append(jnp.exp(-(b_t + n_t)))

    sv = [_dot(s, vext) for s, vext in zip(s_list, vexts)]
    for bh, (bi, hd) in enumerate(streams):
        ext = a_list[bh] * q_state[bh] + sv[bh]
        num = ext[:, :HEAD_DIM]
        den = ext[:, HEAD_DIM:]
        o_ref[bi, :, lanes(hd)] = (num / jnp.maximum(jnp.abs(den), floor_list[bh])).astype(o_ref.dtype)
        state_ref[bh] = decays[bh] * states[bh] + upd[bh]
        m_ref[bh] = jnp.broadcast_to(m_news[bh], m_ref.shape[1:])


def _mlstm_call(mq, mk, mv, gt, batch, seq):
    nc = seq // MLSTM_CHUNK
    W = MLSTM_WIDTH
    blk = lambda w: pl.BlockSpec((batch, MLSTM_CHUNK, w), lambda c: (0, c, 0))
    out = pl.pallas_call(
        _mlstm_kernel,
        grid=(nc,),
        in_specs=[blk(W), blk(W), blk(W),
                  pl.BlockSpec((batch, 2 * MLSTM_HEADS, MLSTM_CHUNK), lambda c: (0, 0, c))],
        out_specs=blk(W),
        out_shape=jax.ShapeDtypeStruct((batch, seq, W), BF16),
        scratch_shapes=[pltpu.VMEM((batch * MLSTM_HEADS, HEAD_DIM, 2 * HEAD_DIM), F32),
                        pltpu.VMEM((batch * MLSTM_HEADS, 8, 128), F32)],
        compiler_params=_params(("arbitrary",)),
        name="mlstm",
    )(mq.reshape(batch, seq, W), mk.reshape(batch, seq, W), mv.reshape(batch, seq, W), gt)
    return out.reshape(batch * seq, W)


def _mix_out_kernel(h_ref, a0_ref, a1_ref, a2_ref, ml_ref, g_ref, wg_ref, wua_ref, wub_ref, wo_ref, out_ref):
    a_refs = (a0_ref, a1_ref, a2_ref)
    subs = [slice(s * MIX_SUB, (s + 1) * MIX_SUB) for s in range(h_ref.shape[0] // MIX_SUB)]

    def gates(rows):
        u = _rms_norm(h_ref[rows, :], g_ref[...]).astype(BF16)
        return _dot(u, wg_ref[...])

    def attention_up(rows):
        stats = [a[ATT_SLOTS, rows, :] for a in a_refs]
        mx = jnp.maximum(jnp.maximum(stats[0], stats[1]), stats[2])
        ws = [jnp.exp(s - mx) for s in stats]
        dens = [pltpu.roll(s, HEAD_DIM - ATT_SLOTS, 1) for s in stats]
        inv = 1.0 / (ws[0] * dens[0] + ws[1] * dens[1] + ws[2] * dens[2])
        alphas = [w * inv for w in ws]
        merged_heads = []
        for j in range(ATT_SLOTS):
            acc = alphas[0][:, j:j + 1] * a_refs[0][j, rows, :]
            acc = acc + alphas[1][:, j:j + 1] * a_refs[1][j, rows, :]
            acc = acc + alphas[2][:, j:j + 1] * a_refs[2][j, rows, :]
            merged_heads.append(acc.astype(BF16))
        return _dot(jnp.concatenate(merged_heads, axis=1), wua_ref[...])

    zgs = [gates(rows) for rows in subs]
    y_atts = [attention_up(rows) for rows in subs]
    y_mls = [_dot((jax.nn.sigmoid(zg[:, :MLSTM_WIDTH]) * ml_ref[rows, :]).astype(BF16), wub_ref[...])
             for zg, rows in zip(zgs, subs)]
    for zg, y_att, y_ml, rows in zip(zgs, y_atts, y_mls, subs):
        ga = zg[:, MLSTM_WIDTH:MLSTM_WIDTH + D_MODEL]
        gb = zg[:, MLSTM_WIDTH + D_MODEL:]
        merged = jax.nn.sigmoid(ga) * y_att + jax.nn.sigmoid(gb) * y_ml
        out_ref[rows, :] = h_ref[rows, :] + _dot(merged.astype(BF16), wo_ref[...])


def _mix_out_call(h2d, att, ml, mix_norm, w_in, w_up_att, w_up_mlstm, w_out, tm):
    n_rows = h2d.shape[0]
    g2 = mix_norm.reshape(1, D_MODEL)
    o_mo = 3 * ATT_WIDTH + 3 * MLSTM_WIDTH
    o_g = o_mo + MLSTM_WIDTH + 2 * MLSTM_HEADS
    wg = jnp.concatenate([w_in[:, o_mo:o_mo + MLSTM_WIDTH], w_in[:, o_g:o_g + 2 * D_MODEL]], axis=1).astype(BF16)
    wua = w_up_att.astype(BF16)
    wub = w_up_mlstm.astype(BF16)
    wo = w_out.astype(BF16)
    row = lambda w: pl.BlockSpec((tm, w), lambda i: (i, 0))
    slabs = pl.BlockSpec((ATT_SLABS, tm, HEAD_DIM), lambda i: (0, i, 0))
    return pl.pallas_call(
        _mix_out_kernel,
        grid=(n_rows // tm,),
        in_specs=[row(D_MODEL), slabs, slabs, slabs, row(MLSTM_WIDTH),
                  _resident(g2.shape), _resident(wg.shape), _resident(wua.shape), _resident(wub.shape),
                  _resident(wo.shape)],
        out_specs=row(D_MODEL),
        out_shape=jax.ShapeDtypeStruct((n_rows, D_MODEL), F32),
        compiler_params=_params(("parallel",)),
        name="mix_out",
    )(h2d, *att, ml, g2, wg, wua, wub, wo)


def _layer(x2d, p2d, batch, seq, ffn1_norm, ffn1_w_in, ffn1_w_out, mix_norm, w_in, q_gain, k_gain, conv_w, conv_b,
           i_bias, f_bias, w_up_att, w_up_mlstm, w_out, ffn2_norm, ffn2_w_in, ffn2_w_out, ple_norm, w_ple_gate,
           w_ple_proj):
    tm = 512
    h = _ffn_call(x2d, ffn1_norm, ffn1_w_in, ffn1_w_out, 1024)
    qkv, (mq, mk, mv, gt) = _in_proj_call(h, mix_norm, w_in, q_gain, k_gain, conv_w, conv_b, i_bias, f_bias,
                                              batch, seq, 512)
    att = []
    for g, (window, dilation) in enumerate(ATT_GROUPS):
        assert window // dilation == ATT_BLOCK
        att.append(_attention_call(*qkv[g], g, dilation, ATT_BLOCKS_PER_STEP[g], batch, seq))
    ml = _mlstm_call(mq, mk, mv, gt, batch, seq)
    h = _mix_out_call(h, att, ml, mix_norm, w_in, w_up_att, w_up_mlstm, w_out, tm)
    ple = (p2d, ple_norm, w_ple_gate.astype(BF16), w_ple_proj.astype(BF16))
    return _ffn_call(h, ffn2_norm, ffn2_w_in, ffn2_w_out, 1024, ple=ple)


def kernel(x, p, ffn1_norm, ffn1_w_in, ffn1_w_out, mix_norm, w_in, q_gain, k_gain, conv_w, conv_b, i_bias, f_bias,
           w_up_att, w_up_mlstm, w_out, ffn2_norm, ffn2_w_in, ffn2_w_out, ple_norm, w_ple_gate, w_ple_proj):
    batch, seq, d = x.shape
    depth = p.shape[0]
    assert d == D_MODEL and seq % (ATT_GROUPS[-1][1] * ATT_BLOCK) == 0
    h = x.reshape(batch * seq, d)
    for i in range(depth):
        h = _layer(h, p[i].reshape(batch * seq, PLE_DIM), batch, seq, ffn1_norm[i], ffn1_w_in[i], ffn1_w_out[i],
                   mix_norm[i], w_in[i], q_gain[i], k_gain[i], conv_w[i], conv_b[i], i_bias[i], f_bias[i],
                   w_up_att[i], w_up_mlstm[i], w_out[i], ffn2_norm[i], ffn2_w_in[i], ffn2_w_out[i],
                   ple_norm[i], w_ple_gate[i], w_ple_proj[i])
    return h.reshape(batch, seq, d)
```
